```python
import math
import jax
import jax.numpy as jnp
from jax import lax
import numpy as np

D_MODEL = 1024
BATCH = 16
SEQ = 4096
DEPTH = 4
DEC_BATCH = 4
DEC_SEQ = 4096
PAST_LEN = 128

RET_WIDTH = D_MODEL // 2
RET_HEADS = 4
RET_HEAD_DIM = RET_WIDTH // RET_HEADS
HY_WIDTH = D_MODEL // 4
POOL_WIDTH = D_MODEL // 4
POOL_GROUPS = 4
POOL_GROUP_DIM = POOL_WIDTH // POOL_GROUPS
POOL_WINDOWS = (2, 4, 8, 16)
IN_WIDTH = 4 * RET_WIDTH + 3 * HY_WIDTH + POOL_WIDTH
CHUNK = 128
ROPE_THETA = 10000.0
HY_ORDER = 2
HY_EMB_BANDS = 16
HY_EMB_DIM = 1 + 2 * HY_EMB_BANDS
HY_FILTER_HIDDEN = 64
HY_MIN_DECAY = math.log(1e-2) / 1.5
HY_MAX_DECAY = math.log(1e-2) / 0.3
D_FF = 2816
PLE_DIM = 256
EPS = 1e-6

kernel_name = "hybrid_retention_hyena_pool_encoder"


def _rmsnorm(x, g):
    xf = x.astype(jnp.float32)
    xf = xf * lax.rsqrt(jnp.mean(xf * xf, axis=-1, keepdims=True) + EPS)
    return (xf * g.astype(jnp.float32)).astype(x.dtype)


def _dwconv3(x, w):
    xp = jnp.pad(x, ((0, 0), (1, 1), (0, 0)))
    return xp[:, :-2] * w[0] + xp[:, 1:-1] * w[1] + xp[:, 2:] * w[2]


def _rope(x):
    L, dk = x.shape[1], x.shape[-1]
    half = dk // 2
    inv = ROPE_THETA ** (-jnp.arange(half, dtype=jnp.float32) / half)
    ang = jnp.arange(L, dtype=jnp.float32)[:, None] * inv[None, :]
    cos = jnp.cos(ang)[None, :, None, :]
    sin = jnp.sin(ang)[None, :, None, :]
    xf = x.astype(jnp.float32)
    x1, x2 = xf[..., :half], xf[..., half:]
    return jnp.concatenate([x1 * cos - x2 * sin, x2 * cos + x1 * sin], axis=-1)


def _retention(q, k, v, logit_f, logit_b):
    B, L, H, dk = q.shape
    dv = v.shape[-1]
    N = L // CHUNK
    f32 = jnp.float32
    lgf = jax.nn.log_sigmoid(logit_f.astype(f32))
    lgb = jax.nn.log_sigmoid(logit_b.astype(f32))
    qc = q.astype(f32).reshape(B, N, CHUNK, H, dk)
    kc = k.astype(f32).reshape(B, N, CHUNK, H, dk)
    vc = v.astype(f32).reshape(B, N, CHUNK, H, dv)
    pos = jnp.arange(CHUNK, dtype=f32)
    dist = jnp.abs(pos[:, None] - pos[None, :])
    lower = pos[:, None] >= pos[None, :]
    dmask = jnp.where(lower[None], jnp.exp(dist[None] * lgf[:, None, None]),
                      jnp.exp(dist[None] * lgb[:, None, None]))
    scores = jnp.einsum('bnqhd,bnkhd->bnhqk', qc, kc) * dmask[None, None]
    intra = jnp.einsum('bnhqk,bnkhe->bnqhe', scores, vc)
    kv_f = jnp.einsum('bnkhd,bnkhe,hk->nbhde', kc, vc, jnp.exp((CHUNK - 1 - pos)[None] * lgf[:, None]))
    kv_b = jnp.einsum('bnkhd,bnkhe,hk->nbhde', kc, vc, jnp.exp(pos[None] * lgb[:, None]))
    dec_f = jnp.exp(CHUNK * lgf)[None, :, None, None]
    dec_b = jnp.exp(CHUNK * lgb)[None, :, None, None]

    def step_f(s, kv):
        return s * dec_f + kv, s

    def step_b(s, kv):
        return s * dec_b + kv, s

    s0 = jnp.zeros((B, H, dk, dv), f32)
    _, s_f = lax.scan(step_f, s0, kv_f)
    _, s_b = lax.scan(step_b, s0, kv_b, reverse=True)
    cross_f = jnp.einsum('bnqhd,nbhde,hq->bnqhe', qc, s_f, jnp.exp((pos + 1.0)[None] * lgf[:, None]))
    cross_b = jnp.einsum('bnqhd,nbhde,hq->bnqhe', qc, s_b, jnp.exp((CHUNK - pos)[None] * lgb[:, None]))
    return (intra + cross_f + cross_b).reshape(B, L, H, dv)


def _head_layernorm(o, gain):
    mu = jnp.mean(o, axis=-1, keepdims=True)
    oc = o - mu
    var = jnp.mean(oc * oc, axis=-1, keepdims=True)
    return oc * lax.rsqrt(var + EPS) * gain.astype(jnp.float32).reshape(RET_HEADS, RET_HEAD_DIM)


def _hyena_filters(L, w1, b1, freq, w2, b2, w3):
    f32 = jnp.float32
    t = jnp.linspace(0.0, 1.0, L, dtype=f32)[:, None]
    bands = jnp.linspace(1e-4, HY_EMB_BANDS - 1, HY_EMB_BANDS, dtype=f32)
    w = (2.0 * math.pi / L) * jnp.arange(L, dtype=f32)[:, None] * bands[None, :]
    feats = jnp.concatenate([t, jnp.cos(w), -jnp.sin(w)], axis=-1)
    fr = freq.astype(f32)
    h = jnp.sin(fr * (feats @ w1.astype(f32) + b1.astype(f32)))
    h = jnp.sin(fr * (h @ w2.astype(f32) + b2.astype(f32)))
    h = (h @ w3.astype(f32)).reshape(L, HY_ORDER, 2, HY_WIDTH)
    deltas = jnp.abs(jnp.linspace(HY_MIN_DECAY, HY_MAX_DECAY, HY_WIDTH, dtype=f32))
    h = h * jnp.exp(-t[:, :, None, None] * deltas)
    h_f, h_b = h[:, :, 0], h[:, :, 1]
    k_full = jnp.concatenate([h_f, jnp.zeros((1, HY_ORDER, HY_WIDTH), f32), h_b[:0:-1]], axis=0)
    k_full = k_full / jnp.sum(jnp.abs(k_full), axis=0, keepdims=True)
    return jnp.fft.rfft(k_full, axis=0)


def _fftconv(u, khat):
    L = u.shape[1]
    uh = jnp.fft.rfft(u, n=2 * L, axis=1)
    return jnp.fft.irfft(uh * khat[None], n=2 * L, axis=1)[:, :L]


def _hyena(u, conv_w, w1, b1, freq, w2, b2, w3, bias):
    L = u.shape[1]
    u = _dwconv3(u, conv_w).astype(jnp.float32)
    hv, hx1, hx2 = jnp.split(u, 3, axis=-1)
    khat = _hyena_filters(L, w1, b1, freq, w2, b2, w3)
    bias = bias.astype(jnp.float32)
    z = hx1 * (_fftconv(hv, khat[:, 0]) + hv * bias[0])
    return hx2 * (_fftconv(z, khat[:, 1]) + z * bias[1])


def _pool_mixer(u, pool_w, pool_scale):
    B, L, _ = u.shape
    t = jnp.arange(L)
    outs = []
    for g, win in enumerate(POOL_WINDOWS):
        ug = u[..., g * POOL_GROUP_DIM:(g + 1) * POOL_GROUP_DIM].astype(jnp.float32)
        cs = jnp.concatenate([jnp.zeros((B, 1, POOL_GROUP_DIM), jnp.float32), jnp.cumsum(ug, axis=1)], axis=1)
        lo = jnp.clip(t - win // 2, 0, L - 1)
        hi = jnp.clip(t + win // 2 - 1, 0, L - 1)
        cnt = (hi - lo + 1).astype(jnp.float32)
        mean = (cs[:, hi + 1] - cs[:, lo]) / cnt[None, :, None]
        outs.append((mean - ug) @ pool_w[g].astype(jnp.float32))
    return jnp.concatenate(outs, axis=-1) * pool_scale.astype(jnp.float32)


def _layer(x, p, norm_mix, w_in, ret_decay_fwd, ret_decay_bwd, ret_gn, hy_short_conv,
           hy_w1, hy_b1, hy_freq, hy_w2, hy_b2, hy_w3, hy_bias, pool_w, pool_scale, w_out,
           norm_ffn, ffn_w_up, ffn_conv, ffn_w_down, ple_w, ple_gate_w, ple_norm):
    B, L, _ = x.shape
    h = _rmsnorm(x, norm_mix)
    proj = h @ w_in
    R = RET_WIDTH
    q, k, v, g, hy_in, pool_in = jnp.split(proj, [R, 2 * R, 3 * R, 4 * R, 4 * R + 3 * HY_WIDTH], axis=-1)
    q = _rope(q.reshape(B, L, RET_HEADS, RET_HEAD_DIM)) * (RET_HEAD_DIM ** -0.5)
    k = _rope(k.reshape(B, L, RET_HEADS, RET_HEAD_DIM))
    v = v.reshape(B, L, RET_HEADS, RET_HEAD_DIM)
    ret = _retention(q, k, v, ret_decay_fwd, ret_decay_bwd)
    ret = _head_layernorm(ret, ret_gn).reshape(B, L, RET_WIDTH) * jax.nn.silu(g.astype(jnp.float32))
    hy_out = _hyena(hy_in, hy_short_conv, hy_w1, hy_b1, hy_freq, hy_w2, hy_b2, hy_w3, hy_bias)
    pool_out = _pool_mixer(pool_in, pool_w, pool_scale)
    mix = jnp.concatenate([ret, hy_out, pool_out], axis=-1).astype(x.dtype) @ w_out
    x = x + mix
    h = _rmsnorm(x, norm_ffn)
    gate, up = jnp.split(h @ ffn_w_up, 2, axis=-1)
    gate = _dwconv3(gate, ffn_conv)
    x = x + (jax.nn.gelu(gate) * up) @ ffn_w_down
    e = _rmsnorm(p @ ple_w, ple_norm)
    return x + jax.nn.sigmoid(x @ ple_gate_w) * e


def _trunk(x, p, layer_weights, norm_final):
    for i in range(DEPTH):
        x = _layer(x, p[i], *[w[i] for w in layer_weights])
    return _rmsnorm(x, norm_final)


def setup_inputs(seed: int = 0) -> dict:
    key = jax.random.key(seed)
    ks = jax.random.split(key, 32)
    f32 = jnp.float32
    nrm = lambda k, shape, s: (jax.random.normal(k, shape, f32) * s).astype(f32)
    gain = lambda k, shape: (1.0 + 0.05 * jax.random.normal(k, shape, f32)).astype(f32)
    base_logit = jnp.asarray(np.log(2.0 ** (5 + np.arange(RET_HEADS)) - 1.0), dtype=f32)
    return {
        "x_prompt": nrm(ks[0], (BATCH, SEQ, D_MODEL), 1.0),
        "x_sample": nrm(ks[1], (DEC_BATCH, DEC_SEQ, D_MODEL), 1.0),
        "p_prompt": nrm(ks[2], (DEPTH, BATCH, SEQ, PLE_DIM), 1.0),
        "p_sample": nrm(ks[3], (DEPTH, DEC_BATCH, DEC_SEQ, PLE_DIM), 1.0),
        "norm_mix": gain(ks[4], (DEPTH, D_MODEL)),
        "w_in": nrm(ks[5], (DEPTH, D_MODEL, IN_WIDTH), D_MODEL ** -0.5),
        "ret_decay_fwd": base_logit[None] + nrm(ks[6], (DEPTH, RET_HEADS), 0.1),
        "ret_decay_bwd": base_logit[None] + nrm(ks[7], (DEPTH, RET_HEADS), 0.1),
        "ret_gn": gain(ks[8], (DEPTH, RET_WIDTH)),
        "hy_short_conv": nrm(ks[9], (DEPTH, 3, 3 * HY_WIDTH), 3 ** -0.5),
        "hy_w1": nrm(ks[10], (DEPTH, HY_EMB_DIM, HY_FILTER_HIDDEN), HY_EMB_DIM ** -0.5),
        "hy_b1": nrm(ks[11], (DEPTH, HY_FILTER_HIDDEN), 0.1),
        "hy_freq": gain(ks[12], (DEPTH, HY_FILTER_HIDDEN)),
        "hy_w2": nrm(ks[13], (DEPTH, HY_FILTER_HIDDEN, HY_FILTER_HIDDEN), HY_FILTER_HIDDEN ** -0.5),
        "hy_b2": nrm(ks[14], (DEPTH, HY_FILTER_HIDDEN), 0.1),
        "hy_w3": nrm(ks[15], (DEPTH, HY_FILTER_HIDDEN, HY_ORDER * 2 * HY_WIDTH), HY_FILTER_HIDDEN ** -0.5),
        "hy_bias": nrm(ks[16], (DEPTH, HY_ORDER, HY_WIDTH), 1.0),
        "pool_w": nrm(ks[17], (DEPTH, POOL_GROUPS, POOL_GROUP_DIM, POOL_GROUP_DIM), POOL_GROUP_DIM ** -0.5),
        "pool_scale": (0.5 + 0.05 * jax.random.normal(ks[18], (DEPTH, POOL_WIDTH), f32)).astype(f32),
        "w_out": nrm(ks[19], (DEPTH, D_MODEL, D_MODEL), D_MODEL ** -0.5),
        "norm_ffn": gain(ks[20], (DEPTH, D_MODEL)),
        "ffn_w_up": nrm(ks[21], (DEPTH, D_MODEL, 2 * D_FF), D_MODEL ** -0.5),
        "ffn_conv": nrm(ks[22], (DEPTH, 3, D_FF), 3 ** -0.5),
        "ffn_w_down": nrm(ks[23], (DEPTH, D_FF, D_MODEL), D_FF ** -0.5),
        "ple_w": nrm(ks[24], (DEPTH, PLE_DIM, D_MODEL), PLE_DIM ** -0.5),
        "ple_gate_w": nrm(ks[25], (DEPTH, D_MODEL, D_MODEL), D_MODEL ** -0.5),
        "ple_norm": gain(ks[26], (DEPTH, D_MODEL)),
        "norm_final": gain(ks[27], (D_MODEL,)),
    }


def reference(x_prompt, x_sample, p_prompt, p_sample, norm_mix, w_in, ret_decay_fwd, ret_decay_bwd,
              ret_gn, hy_short_conv, hy_w1, hy_b1, hy_freq, hy_w2, hy_b2, hy_w3, hy_bias, pool_w,
              pool_scale, w_out, norm_ffn, ffn_w_up, ffn_conv, ffn_w_down, ple_w, ple_gate_w,
              ple_norm, norm_final):
    layer_weights = (norm_mix, w_in, ret_decay_fwd, ret_decay_bwd, ret_gn, hy_short_conv,
                     hy_w1, hy_b1, hy_freq, hy_w2, hy_b2, hy_w3, hy_bias, pool_w, pool_scale, w_out,
                     norm_ffn, ffn_w_up, ffn_conv, ffn_w_down, ple_w, ple_gate_w, ple_norm)
    y_prompt = _trunk(x_prompt, p_prompt, layer_weights, norm_final)
    y_sample = _trunk(x_sample, p_sample, layer_weights, norm_final)
    return (y_prompt, y_sample)
```

```python
import functools
import math

import jax
import jax.numpy as jnp
from jax import lax
from jax.experimental import pallas as pl
from jax.experimental.pallas import tpu as pltpu

F32 = jnp.float32
BF16 = jnp.bfloat16

D_MODEL = 1024
RET_WIDTH = 512
RET_HEADS = 4
HEAD_DIM = 128
HY_WIDTH = 256
POOL_WIDTH = 256
POOL_GROUP_DIM = 64
POOL_WINDOWS = (2, 4, 8, 16)
POOL_PAD = 16
D_FF = 2816
PLE_DIM = 256
CHUNK = 128
ROPE_THETA = 10000.0
HY_EMB_BANDS = 16
HY_HIDDEN = 64
HY_MIN_DECAY = math.log(1e-2) / 1.5
HY_MAX_DECAY = math.log(1e-2) / 0.3
EPS = 1e-6

LANES = 128
SUBLANES = 8
BF16_ROWS = 16
MXU_DIM = 256
VMEM_LIMIT = 56 * 1024 * 1024

HY_BLOCK = MXU_DIM
QKVG_WIDTH = 4 * RET_WIDTH
HY_IN_WIDTH = 3 * HY_WIDTH
FFN_CHUNK = D_FF // 2


def _cparams(sem):
    return pltpu.CompilerParams(dimension_semantics=sem, vmem_limit_bytes=VMEM_LIMIT)


def _rms(x, g):
    return x * lax.rsqrt(jnp.mean(x * x, axis=-1, keepdims=True) + EPS) * g


def _in_proj_kernel(x_ref, g_ref, wa_ref, wh_ref, wp_ref, qkvg_ref, hy_ref, pool_ref):
    tm = x_ref.shape[1]
    h = _rms(x_ref[0], g_ref[...]).astype(BF16)
    qkvg_ref[0] = jnp.dot(h, wa_ref[...], preferred_element_type=F32).astype(BF16)
    pool_ref[0] = jnp.dot(h, wp_ref[...], preferred_element_type=F32).astype(BF16)
    hy_t = lax.dot_general(wh_ref[...], h, (((1,), (1,)), ((), ())), preferred_element_type=F32)
    hy_ref[...] = hy_t.reshape(HY_IN_WIDTH // SUBLANES, SUBLANES, tm)


def _in_proj(x, g, wa, wh_t, wp, tm):
    B, L, _ = x.shape
    nt = L // tm
    return pl.pallas_call(
        _in_proj_kernel,
        grid=(B, nt),
        in_specs=[
            pl.BlockSpec((1, tm, D_MODEL), lambda b, i: (b, i, 0)),
            pl.BlockSpec((1, D_MODEL), lambda b, i: (0, 0)),
            pl.BlockSpec((D_MODEL, QKVG_WIDTH), lambda b, i: (0, 0)),
            pl.BlockSpec((HY_IN_WIDTH, D_MODEL), lambda b, i: (0, 0)),
            pl.BlockSpec((D_MODEL, POOL_WIDTH), lambda b, i: (0, 0)),
        ],
        out_specs=[
            pl.BlockSpec((1, tm, QKVG_WIDTH), lambda b, i: (b, i, 0)),
            pl.BlockSpec((HY_IN_WIDTH // SUBLANES, SUBLANES, tm), lambda b, i: (0, b, i)),
            pl.BlockSpec((1, tm, POOL_WIDTH), lambda b, i: (b, i, 0)),
        ],
        out_shape=[
            jax.ShapeDtypeStruct((B, L, QKVG_WIDTH), BF16),
            jax.ShapeDtypeStruct((HY_IN_WIDTH // SUBLANES, B * SUBLANES, L), F32),
            jax.ShapeDtypeStruct((B, L, POOL_WIDTH), BF16),
        ],
        compiler_params=_cparams(("parallel", "parallel")),
        name="in_proj",
    )(x, g, wa, wh_t, wp)


def _log_sigmoid(x):
    return jnp.minimum(x, 0.0) - jnp.log(1.0 + jnp.exp(-jnp.abs(x)))


def _retention_kernel(dec_ref, cos_ref, sin_ref, gn_ref, q_ref, k_ref, v_ref, g_ref, o_ref,
                      qr_s, kr_s, sb_s, tab_s):
    L = q_ref.shape[1]
    n_chunks = L // CHUNK
    C = CHUNK

    lg = _log_sigmoid(dec_ref[0])
    lgf, lgb = lg[0:1], lg[1:2]
    ri = lax.broadcasted_iota(jnp.int32, (C, C), 0)
    ci = lax.broadcasted_iota(jnp.int32, (C, C), 1)
    rf = ri.astype(F32)
    dist = jnp.abs(ri - ci).astype(F32)
    tab_s[0] = jnp.exp(dist * jnp.where(ri >= ci, lgf, lgb))
    tab_s[1] = jnp.exp((rf + 1.0) * lgf)
    tab_s[2] = jnp.exp((C - rf) * lgb)
    tab_s[3] = jnp.exp((C - 1.0 - rf) * lgf)
    tab_s[4] = jnp.exp(rf * lgb)
    dec_f = jnp.exp(C * lgf)
    dec_b = jnp.exp(C * lgb)

    cos = cos_ref[...]
    sin = sin_ref[...]
    qf = q_ref[0].astype(F32)
    qr_s[...] = (qf * cos + pltpu.roll(qf, HEAD_DIM // 2, axis=1) * sin) * (HEAD_DIM ** -0.5)
    kf = k_ref[0].astype(F32)
    kr_s[...] = kf * cos + pltpu.roll(kf, HEAD_DIM // 2, axis=1) * sin

    tn = (((0,), (0,)), ((), ()))
    nt = (((1,), (1,)), ((), ()))

    def bwd_body(i, s):
        n = n_chunks - 1 - i
        r0 = pl.multiple_of(n * C, C)
        sb_s[n] = s.astype(BF16)
        kb = (kr_s[pl.ds(r0, C), :] * tab_s[4]).astype(BF16)
        kv = lax.dot_general(kb, v_ref[0, pl.ds(r0, C), :], tn, preferred_element_type=F32)
        return s * dec_b + kv

    lax.fori_loop(0, n_chunks, bwd_body, jnp.zeros((C, C), F32))

    gain = gn_ref[...]

    def fwd_body(n, s):
        r0 = pl.multiple_of(n * C, C)
        qn = qr_s[pl.ds(r0, C), :]
        kn = kr_s[pl.ds(r0, C), :]
        vn = v_ref[0, pl.ds(r0, C), :]
        sc = lax.dot_general(qn.astype(BF16), kn.astype(BF16), nt, preferred_element_type=F32)
        sc = (sc * tab_s[0]).astype(BF16)
        o = jnp.dot(sc, vn, preferred_element_type=F32)
        o += jnp.dot((qn * tab_s[1]).astype(BF16), s.astype(BF16), preferred_element_type=F32)
        o += jnp.dot((qn * tab_s[2]).astype(BF16), sb_s[n], preferred_element_type=F32)
        kv = lax.dot_general((kn * tab_s[3]).astype(BF16), vn, tn, preferred_element_type=F32)
        mu = jnp.mean(o, axis=-1, keepdims=True)
        oc = o - mu
        var = jnp.mean(oc * oc, axis=-1, keepdims=True)
        gate = g_ref[0, pl.ds(r0, C), :].astype(F32)
        y = oc * lax.rsqrt(var + EPS) * gain * (gate * jax.nn.sigmoid(gate))
        o_ref[0, pl.ds(r0, C), :] = y.astype(BF16)
        return s * dec_f + kv

    lax.fori_loop(0, n_chunks, fwd_body, jnp.zeros((C, C), F32))


def _retention(qkvg, dec, cos, sin, gn):
    B, L, _ = qkvg.shape
    H = RET_HEADS

    def col(off):
        return pl.BlockSpec((1, L, HEAD_DIM), lambda b, h: (b, 0, off + h))

    return pl.pallas_call(
        _retention_kernel,
        grid=(B, H),
        in_specs=[
            pl.BlockSpec((1, 2, LANES), lambda b, h: (h, 0, 0)),
            pl.BlockSpec((L, HEAD_DIM), lambda b, h: (0, 0)),
            pl.BlockSpec((L, HEAD_DIM), lambda b, h: (0, 0)),
            pl.BlockSpec((1, HEAD_DIM), lambda b, h: (0, h)),
            col(0), col(H), col(2 * H), col(3 * H),
        ],
        out_specs=pl.BlockSpec((1, L, HEAD_DIM), lambda b, h: (b, 0, h)),
        out_shape=jax.ShapeDtypeStruct((B, L, RET_WIDTH), BF16),
        scratch_shapes=[
            pltpu.VMEM((L, HEAD_DIM), F32),
            pltpu.VMEM((L, HEAD_DIM), F32),
            pltpu.VMEM((L // CHUNK, CHUNK, CHUNK), BF16),
            pltpu.VMEM((5, CHUNK, CHUNK), F32),
        ],
        compiler_params=_cparams(("parallel", "parallel")),
        name="retention",
    )(dec, cos, sin, gn, qkvg, qkvg, qkvg, qkvg)


def _filter_kernel(w1_ref, b1_ref, fr_ref, w2_ref, b2_ref, w3_ref, kk_ref):
    L = kk_ref.shape[1] // 2
    hi = lax.Precision.HIGHEST
    r = lax.broadcasted_iota(jnp.int32, (HY_HIDDEN, L), 0)
    lane = lax.broadcasted_iota(jnp.int32, (HY_HIDDEN, L), 1)
    band_idx = jnp.where(r <= HY_EMB_BANDS, r - 1, r - 1 - HY_EMB_BANDS).astype(F32)
    band = 1e-4 + band_idx * ((HY_EMB_BANDS - 1 - 1e-4) / (HY_EMB_BANDS - 1))
    crow = lax.broadcasted_iota(jnp.int32, (HY_WIDTH, L), 0).astype(F32)
    delta = jnp.abs(HY_MIN_DECAY + crow * ((HY_MAX_DECAY - HY_MIN_DECAY) / (HY_WIDTH - 1)))
    clane = lax.broadcasted_iota(jnp.int32, (HY_WIDTH, L), 1)
    fr = fr_ref[...]

    def taps(s_int, s_row_int, w3):
        s = s_int.astype(F32)
        t = s / (L - 1.0)
        ang = (2.0 * math.pi / L) * s * band
        feats = jnp.where(r == 0, t,
                          jnp.where(r <= HY_EMB_BANDS, jnp.cos(ang),
                                    jnp.where(r <= 2 * HY_EMB_BANDS, -jnp.sin(ang), 0.0)))
        h = jnp.sin(fr * (jnp.dot(w1_ref[...], feats, precision=hi, preferred_element_type=F32) + b1_ref[...]))
        h = jnp.sin(fr * (jnp.dot(w2_ref[...], h, precision=hi, preferred_element_type=F32) + b2_ref[...]))
        h = jnp.dot(w3, h, precision=hi, preferred_element_type=F32)
        return h * jnp.exp(-(s_row_int.astype(F32) / (L - 1.0)) * delta)

    h_f = taps(lane, clane, w3_ref[0, 0])
    h_b = taps(L - lane, L - clane, w3_ref[0, 1])
    h_b = jnp.where(clane == 0, 0.0, h_b)
    norm = jnp.sum(jnp.abs(h_f), axis=-1, keepdims=True) + jnp.sum(jnp.abs(h_b), axis=-1, keepdims=True)
    inv = 1.0 / norm
    kk_ref[:, :L] = h_b * inv
    kk_ref[:, L:] = h_f * inv


def _hyena_filters(w1t, b1, fr, w2t, b2, w3t, L):
    small = lambda shape: pl.BlockSpec(shape, lambda o: (0,) * len(shape))
    return pl.pallas_call(
        _filter_kernel,
        grid=(2,),
        in_specs=[
            small((HY_HIDDEN, HY_HIDDEN)), small((HY_HIDDEN, 1)), small((HY_HIDDEN, 1)),
            small((HY_HIDDEN, HY_HIDDEN)), small((HY_HIDDEN, 1)),
            pl.BlockSpec((1, 2, HY_WIDTH, HY_HIDDEN), lambda o: (o, 0, 0, 0)),
        ],
        out_specs=pl.BlockSpec((HY_WIDTH, 2 * L), lambda o: (o, 0)),
        out_shape=jax.ShapeDtypeStruct((2 * HY_WIDTH, 2 * L), F32),
        compiler_params=_cparams(("parallel",)),
        name="hyena_filters",
    )(w1t, b1, fr, w2t, b2, w3t)


def _hyena_kernel(kk1_ref, kk2_ref, xv_ref, x1_ref, x2_ref, cwv_ref, cw1_ref, cw2_ref, bias_ref,
                  out_ref, cv_s, c1_s, c2_s, o_s, toep_s, u_s, x1u_s, y_s, *, B, BP):
    L = xv_ref.shape[1]
    T = HY_BLOCK
    NB = L // T
    R = SUBLANES * B
    TPB = T // LANES

    @pl.when(pl.program_id(0) == 0)
    def _():
        toep_s[...] = jnp.zeros_like(toep_s)
        u_s[...] = jnp.zeros_like(u_s)
        x1u_s[...] = jnp.zeros_like(x1u_s)

    lane = lax.broadcasted_iota(jnp.int32, (R, L), 1)

    def dwconv(x_ref, w_ref, dst):
        x = x_ref[...]
        xp = jnp.where(lane == 0, 0.0, pltpu.roll(x, 1, axis=1))
        xn = jnp.where(lane == L - 1, 0.0, pltpu.roll(x, L - 1, axis=1))
        w = w_ref[0]
        y = (xp.reshape(B, SUBLANES, L) * w[0][None] + x.reshape(B, SUBLANES, L) * w[1][None]
             + xn.reshape(B, SUBLANES, L) * w[2][None]).reshape(R, L)
        for j in range(L // LANES):
            dst[j] = y[:, j * LANES:(j + 1) * LANES]

    dwconv(xv_ref, cwv_ref, cv_s)
    dwconv(x1_ref, cw1_ref, c1_s)
    dwconv(x2_ref, cw2_ref, c2_s)

    def build_toeplitz(kk_ref, c):
        row = jnp.broadcast_to(kk_ref[pl.ds(c, 1), :], (BF16_ROWS, 2 * L))
        for g in range(LANES // BF16_ROWS):
            rolled = pltpu.roll(row, BF16_ROWS * g, axis=1, stride=1, stride_axis=0).astype(BF16)
            toep_s[BF16_ROWS * g:BF16_ROWS * (g + 1), :] = rolled
            toep_s[LANES + BF16_ROWS * g:LANES + BF16_ROWS * (g + 1), LANES:] = rolled[:, :2 * L - LANES]

    def long_conv():
        y_s[...] = jnp.zeros_like(y_s)
        for d in range(-(NB - 1), NB):
            n = NB - abs(d)
            src = max(0, -d) * BP
            dst = max(0, d) * BP
            m = toep_s[:, L + d * T:L + (d + 1) * T]
            lhs = u_s[src:src + n * BP, :].astype(BF16)
            y_s[dst:dst + n * BP, :] += jnp.dot(lhs, m, preferred_element_type=F32)

    def channel(c, carry):
        rows = pl.ds(c, B, stride=SUBLANES)
        for j in range(L // LANES):
            J, l0 = j // TPB, (j % TPB) * LANES
            u_s[J * BP:J * BP + B, l0:l0 + LANES] = cv_s.at[j][rows, :]
            x1u_s[J * BP:J * BP + B, l0:l0 + LANES] = c1_s.at[j][rows, :]
        b0 = bias_ref[0, 0, pl.ds(c, 1), :]
        b1 = bias_ref[0, 1, pl.ds(c, 1), :]
        build_toeplitz(kk1_ref, c)
        long_conv()
        u_s[...] = x1u_s[...] * (y_s[...] + u_s[...] * b0)
        build_toeplitz(kk2_ref, c)
        long_conv()
        for j in range(L // LANES):
            J, l0 = j // TPB, (j % TPB) * LANES
            yy = y_s[J * BP:J * BP + B, l0:l0 + LANES] + u_s[J * BP:J * BP + B, l0:l0 + LANES] * b1[:, :LANES]
            o_s.at[j][rows, :] = c2_s.at[j][rows, :] * yy
        return carry

    lax.fori_loop(0, SUBLANES, channel, 0)
    for j in range(L // LANES):
        out_ref[:, j * LANES:(j + 1) * LANES] = o_s[j]


def _hyena(kk, hy_t, cw, bias, B):
    n_in, R, L = hy_t.shape
    NB = L // HY_BLOCK
    BP = -(-B // SUBLANES) * SUBLANES
    n_c = HY_WIDTH // SUBLANES

    def sect(s):
        return pl.BlockSpec((None, R, L), lambda c: (s * n_c + c, 0, 0))

    def cws(s):
        return pl.BlockSpec((1, 3, SUBLANES, L), lambda c: (s * n_c + c, 0, 0, 0))

    return pl.pallas_call(
        functools.partial(_hyena_kernel, B=B, BP=BP),
        grid=(n_c,),
        in_specs=[
            pl.BlockSpec((SUBLANES, 2 * L), lambda c: (c, 0)),
            pl.BlockSpec((SUBLANES, 2 * L), lambda c: (n_c + c, 0)),
            sect(0), sect(1), sect(2), cws(0), cws(1), cws(2),
            pl.BlockSpec((1, 2, SUBLANES, HY_BLOCK), lambda c: (c, 0, 0, 0)),
        ],
        out_specs=pl.BlockSpec((None, R, L), lambda c: (c, 0, 0)),
        out_shape=jax.ShapeDtypeStruct((n_c, R, L), F32),
        scratch_shapes=[
            pltpu.VMEM((L // LANES, R, LANES), F32), pltpu.VMEM((L // LANES, R, LANES), F32),
            pltpu.VMEM((L // LANES, R, LANES), F32), pltpu.VMEM((L // LANES, R, LANES), F32),
            pltpu.VMEM((2 * LANES, 2 * L), BF16),
            pltpu.VMEM((NB * BP, HY_BLOCK), F32),
            pltpu.VMEM((NB * BP, HY_BLOCK), F32),
            pltpu.VMEM((NB * BP, HY_BLOCK), F32),
        ],
        compiler_params=_cparams(("arbitrary",)),
        name="hyena",
    )(kk, kk, hy_t, hy_t, hy_t, cw, cw, cw, bias)


def _pool_kernel(u_ref, w_ref, sc_ref, o_ref):
    L = u_ref.shape[1]
    u = u_ref[0].astype(F32)
    t = lax.broadcasted_iota(jnp.int32, (L, POOL_WIDTH), 0)
    lane = lax.broadcasted_iota(jnp.int32, (L, POOL_WIDTH), 1)
    pad = jnp.zeros((POOL_PAD, POOL_WIDTH), F32)
    up = jnp.concatenate([pad, u, pad], axis=0)
    n = L + 2 * POOL_PAD

    def both(x, k):
        return pltpu.roll(x, k, axis=0) + pltpu.roll(x, n - k, axis=0)

    q2 = pltpu.roll(up, 1, axis=0) + up
    q4 = both(q2, 1)
    q8 = both(q4, 2)
    q16 = both(q8, 4)
    p2, p4, p8, p16 = (q[POOL_PAD:POOL_PAD + L] for q in (q2, q4, q8, q16))
    g = lane // POOL_GROUP_DIM
    half = jnp.where(g == 0, 1, jnp.where(g == 1, 2, jnp.where(g == 2, 4, 8)))
    total = jnp.where(g == 0, p2, jnp.where(g == 1, p4, jnp.where(g == 2, p8, p16)))
    lo = jnp.maximum(t - half, 0)
    hi = jnp.minimum(t + half - 1, L - 1)
    cnt = (hi - lo + 1).astype(F32)
    d = (total / cnt - u).astype(BF16)
    o_ref[0] = (jnp.dot(d, w_ref[...], preferred_element_type=F32) * sc_ref[...]).astype(BF16)


def _pool(pool_in, w_bd, scale):
    B, L, _ = pool_in.shape
    return pl.pallas_call(
        _pool_kernel,
        grid=(B,),
        in_specs=[
            pl.BlockSpec((1, L, POOL_WIDTH), lambda b: (b, 0, 0)),
            pl.BlockSpec((POOL_WIDTH, POOL_WIDTH), lambda b: (0, 0)),
            pl.BlockSpec((1, POOL_WIDTH), lambda b: (0, 0)),
        ],
        out_specs=pl.BlockSpec((1, L, POOL_WIDTH), lambda b: (b, 0, 0)),
        out_shape=jax.ShapeDtypeStruct((B, L, POOL_WIDTH), BF16),
        compiler_params=_cparams(("parallel",)),
        name="pool_mixer",
    )(pool_in, w_bd, scale)


def _mix_out_kernel(x_ref, ret_ref, hy_ref, pool_ref, wr_ref, wh_ref, wp_ref, g_ref, x1_ref, h2_ref):
    tm = x_ref.shape[1]
    hy_t = hy_ref[...].reshape(HY_WIDTH, tm).astype(BF16)
    mix = jnp.dot(ret_ref[0], wr_ref[...], preferred_element_type=F32)
    mix += lax.dot_general(hy_t, wh_ref[...], (((0,), (0,)), ((), ())), preferred_element_type=F32)
    mix += jnp.dot(pool_ref[0], wp_ref[...], preferred_element_type=F32)
    x1 = x_ref[0] + mix
    x1_ref[0] = x1
    h2_ref[0] = _rms(x1, g_ref[...]).astype(BF16)


def _mix_out(x, ret, hy_o, pool_o, wr, wh, wp, g, tm):
    B, L, _ = x.shape
    nt = L // tm
    return pl.pallas_call(
        _mix_out_kernel,
        grid=(B, nt),
        in_specs=[
            pl.BlockSpec((1, tm, D_MODEL), lambda b, i: (b, i, 0)),
            pl.BlockSpec((1, tm, RET_WIDTH), lambda b, i: (b, i, 0)),
            pl.BlockSpec((HY_WIDTH // SUBLANES, SUBLANES, tm), lambda b, i: (0, b, i)),
            pl.BlockSpec((1, tm, POOL_WIDTH), lambda b, i: (b, i, 0)),
            pl.BlockSpec((RET_WIDTH, D_MODEL), lambda b, i: (0, 0)),
            pl.BlockSpec((HY_WIDTH, D_MODEL), lambda b, i: (0, 0)),
            pl.BlockSpec((POOL_WIDTH, D_MODEL), lambda b, i: (0, 0)),
            pl.BlockSpec((1, D_MODEL), lambda b, i: (0, 0)),
        ],
        out_specs=[
            pl.BlockSpec((1, tm, D_MODEL), lambda b, i: (b, i, 0)),
            pl.BlockSpec((1, tm, D_MODEL), lambda b, i: (b, i, 0)),
        ],
        out_shape=[
            jax.ShapeDtypeStruct((B, L, D_MODEL), F32),
            jax.ShapeDtypeStruct((B, L, D_MODEL), BF16),
        ],
        compiler_params=_cparams(("parallel", "parallel")),
        name="mix_out",
    )(x, ret, hy_o, pool_o, wr, wh, wp, g)


def _ffn_kernel(h_ref, hp_ref, hn_ref, x1_ref, wg_ref, wu_ref, cw_ref, wd_ref, p_ref, pw_ref, pg_ref,
                pn_ref, fin_ref, o_ref, acc_s, *, final_norm):
    tm = h_ref.shape[1]
    i = pl.program_id(1)
    j = pl.program_id(2)
    nt = pl.num_programs(1)
    nc = pl.num_programs(2)

    h = h_ref[0]
    wg = wg_ref[...]
    gate = jnp.dot(h, wg, preferred_element_type=F32)
    g_prev = jnp.dot(hp_ref[0], wg, preferred_element_type=F32)[BF16_ROWS - 1:BF16_ROWS]
    g_next = jnp.dot(hn_ref[0], wg, preferred_element_type=F32)[0:1]
    g_prev = jnp.where(i == 0, 0.0, g_prev)
    g_next = jnp.where(i == nt - 1, 0.0, g_next)
    row = lax.broadcasted_iota(jnp.int32, gate.shape, 0)
    g_up = jnp.where(row == 0, g_prev, pltpu.roll(gate, 1, axis=0))
    g_dn = jnp.where(row == tm - 1, g_next, pltpu.roll(gate, tm - 1, axis=0))
    cw = cw_ref[...]
    gc = g_up * cw[0:1] + gate * cw[1:2] + g_dn * cw[2:3]
    up = jnp.dot(h, wu_ref[...], preferred_element_type=F32)
    a = (jax.nn.gelu(gc) * up).astype(BF16)
    contrib = jnp.dot(a, wd_ref[...], preferred_element_type=F32)

    @pl.when(j == 0)
    def _():
        acc_s[...] = contrib

    @pl.when(j > 0)
    def _():
        acc_s[...] += contrib

    @pl.when(j == nc - 1)
    def _():
        x2 = x1_ref[0] + acc_s[...]
        e = _rms(jnp.dot(p_ref[0].astype(BF16), pw_ref[...], preferred_element_type=F32), pn_ref[...])
        gt = jax.nn.sigmoid(jnp.dot(x2.astype(BF16), pg_ref[...], preferred_element_type=F32))
        x3 = x2 + gt * e
        if final_norm:
            x3 = _rms(x3, fin_ref[...])
        o_ref[0] = x3


def _ffn(h2, x1, w_up, cw, w_down, p, ple_w, ple_gate, ple_norm, fin, tm, final_norm):
    B, L, _ = x1.shape
    nt = L // tm
    nc = D_FF // FFN_CHUNK
    halo = BF16_ROWS
    nh = L // halo
    per = tm // halo
    full = lambda shape: pl.BlockSpec(shape, lambda b, i, j: (0,) * len(shape))
    return pl.pallas_call(
        functools.partial(_ffn_kernel, final_norm=final_norm),
        grid=(B, nt, nc),
        in_specs=[
            pl.BlockSpec((1, tm, D_MODEL), lambda b, i, j: (b, i, 0)),
            pl.BlockSpec((1, halo, D_MODEL), lambda b, i, j: (b, jnp.maximum(i * per - 1, 0), 0)),
            pl.BlockSpec((1, halo, D_MODEL), lambda b, i, j: (b, jnp.minimum((i + 1) * per, nh - 1), 0)),
            pl.BlockSpec((1, tm, D_MODEL), lambda b, i, j: (b, i, 0)),
            pl.BlockSpec((D_MODEL, FFN_CHUNK), lambda b, i, j: (0, j)),
            pl.BlockSpec((D_MODEL, FFN_CHUNK), lambda b, i, j: (0, nc + j)),
            pl.BlockSpec((3, FFN_CHUNK), lambda b, i, j: (0, j)),
            pl.BlockSpec((FFN_CHUNK, D_MODEL), lambda b, i, j: (j, 0)),
            pl.BlockSpec((1, tm, PLE_DIM), lambda b, i, j: (b, i, 0)),
            full((PLE_DIM, D_MODEL)), full((D_MODEL, D_MODEL)), full((1, D_MODEL)), full((1, D_MODEL)),
        ],
        out_specs=pl.BlockSpec((1, tm, D_MODEL), lambda b, i, j: (b, i, 0)),
        out_shape=jax.ShapeDtypeStruct((B, L, D_MODEL), F32),
        scratch_shapes=[pltpu.VMEM((tm, D_MODEL), F32)],
        compiler_params=_cparams(("parallel", "parallel", "arbitrary")),
        name="ffn_ple",
    )(h2, h2, h2, x1, w_up, w_up, cw, w_down, p, ple_w, ple_gate, ple_norm, fin)


def _rope_tables(L):
    half = HEAD_DIM // 2
    inv = ROPE_THETA ** (-jnp.arange(half, dtype=F32) / half)
    ang = jnp.arange(L, dtype=F32)[:, None] * inv[None, :]
    cos, sin = jnp.cos(ang), jnp.sin(ang)
    return jnp.concatenate([cos, cos], axis=-1), jnp.concatenate([-sin, sin], axis=-1)


def _block_diag(pool_w):
    out = jnp.zeros((POOL_WIDTH, POOL_WIDTH), pool_w.dtype)
    for g in range(len(POOL_WINDOWS)):
        sl = slice(g * POOL_GROUP_DIM, (g + 1) * POOL_GROUP_DIM)
        out = out.at[sl, sl].set(pool_w[g])
    return out


def _layer(x, p, lw, cos, sin, fin, tm, final_norm):
    (norm_mix, w_in, dec_f, dec_b, ret_gn, hy_conv, hy_w1, hy_b1, hy_freq, hy_w2, hy_b2, hy_w3, hy_bias,
     pool_w, pool_scale, w_out, norm_ffn, w_up, ffn_conv, w_down, ple_w, ple_gate, ple_norm) = lw
    B, L, _ = x.shape
    R = RET_WIDTH

    wa = w_in[:, :4 * R].astype(BF16)
    wh_t = w_in[:, 4 * R:4 * R + HY_IN_WIDTH].T.astype(BF16)
    wp = w_in[:, 4 * R + HY_IN_WIDTH:].astype(BF16)
    qkvg, hy_t, pool_in = _in_proj(x, norm_mix[None], wa, wh_t, wp, tm)

    dec = jnp.broadcast_to(jnp.stack([dec_f, dec_b], axis=1)[:, :, None], (RET_HEADS, 2, LANES)).astype(F32)
    ret = _retention(qkvg, dec, cos, sin, ret_gn[None])

    w1t = jnp.zeros((HY_HIDDEN, HY_HIDDEN), F32).at[:, :hy_w1.shape[0]].set(hy_w1.T)
    w3t = hy_w3.T.reshape(2, 2, HY_WIDTH, HY_HIDDEN)
    kk = _hyena_filters(w1t, hy_b1[:, None], hy_freq[:, None], hy_w2.T, hy_b2[:, None], w3t, L)
    n_sec = HY_IN_WIDTH // SUBLANES
    cw = jnp.broadcast_to(hy_conv.reshape(3, n_sec, SUBLANES).transpose(1, 0, 2)[..., None],
                          (n_sec, 3, SUBLANES, L))
    bias = jnp.broadcast_to(hy_bias.reshape(2, HY_WIDTH // SUBLANES, SUBLANES).transpose(1, 0, 2)[..., None],
                            (HY_WIDTH // SUBLANES, 2, SUBLANES, HY_BLOCK))
    hy_o = _hyena(kk, hy_t, cw, bias, B)

    pool_o = _pool(pool_in, _block_diag(pool_w).astype(BF16), pool_scale[None])

    wo = w_out.astype(BF16)
    x1, h2 = _mix_out(x, ret, hy_o, pool_o, wo[:R], wo[R:R + HY_WIDTH], wo[R + HY_WIDTH:], norm_ffn[None], tm)

    return _ffn(h2, x1, w_up.astype(BF16), ffn_conv, w_down.astype(BF16), p, ple_w.astype(BF16),
                ple_gate.astype(BF16), ple_norm[None], fin, tm, final_norm)


def _trunk(x, p, layer_weights, norm_final, tm):
    depth = p.shape[0]
    L = x.shape[1]
    cos, sin = _rope_tables(L)
    fin = norm_final[None]
    for i in range(depth):
        x = _layer(x, p[i], [w[i] for w in layer_weights], cos, sin, fin, tm, i == depth - 1)
    return x


def kernel(x_prompt, x_sample, p_prompt, p_sample, norm_mix, w_in, ret_decay_fwd, ret_decay_bwd, ret_gn,
           hy_short_conv, hy_w1, hy_b1, hy_freq, hy_w2, hy_b2, hy_w3, hy_bias, pool_w, pool_scale, w_out,
           norm_ffn, ffn_w_up, ffn_conv, ffn_w_down, ple_w, ple_gate_w, ple_norm, norm_final):
    layer_weights = (norm_mix, w_in, ret_decay_fwd, ret_decay_bwd, ret_gn, hy_short_conv, hy_w1, hy_b1,
                     hy_freq, hy_w2, hy_b2, hy_w3, hy_bias, pool_w, pool_scale, w_out, norm_ffn, ffn_w_up,
                     ffn_conv, ffn_w_down, ple_w, ple_gate_w, ple_norm)
    nb = x_prompt.shape[0]
    x = jnp.concatenate([x_prompt, x_sample], axis=0)
    p = jnp.concatenate([p_prompt, p_sample], axis=1)
    y = _trunk(x, p, layer_weights, norm_final, 512)
    return (y[:nb], y[nb:])
```

```python
import functools
import math

import jax
import jax.numpy as jnp
from jax import lax
from jax.experimental import pallas as pl
from jax.experimental.pallas import tpu as pltpu

F32 = jnp.float32
BF16 = jnp.bfloat16

D_MODEL = 1024
RET_WIDTH = 512
RET_HEADS = 4
HEAD_DIM = 128
HY_WIDTH = 256
POOL_WIDTH = 256
POOL_GROUP_DIM = 64
POOL_WINDOWS = (2, 4, 8, 16)
POOL_PAD = 16
D_FF = 2816
PLE_DIM = 256
CHUNK = 128
ROPE_THETA = 10000.0
HY_EMB_BANDS = 16
HY_HIDDEN = 64
HY_MIN_DECAY = math.log(1e-2) / 1.5
HY_MAX_DECAY = math.log(1e-2) / 0.3
EPS = 1e-6

LANES = 128
SUBLANES = 8
BF16_ROWS = 16
MXU_DIM = 256
VMEM_LIMIT = 56 * 1024 * 1024

HY_BLOCK = MXU_DIM
QVG_WIDTH = 3 * RET_WIDTH
RET_UNROLL = 4
HY_IN_WIDTH = 3 * HY_WIDTH
ROW_TILE = 512
FFN_SUB = MXU_DIM
FFN_DOWN_GROUP = 4


def _cparams(sem):
    return pltpu.CompilerParams(dimension_semantics=sem, vmem_limit_bytes=VMEM_LIMIT)


def _rms(x, g):
    return x * lax.rsqrt(jnp.mean(x * x, axis=-1, keepdims=True) + EPS) * g


def _pair_specs(tm, width, nt, nbp, off, lead=None):
    pre = () if lead is None else (lead,)
    shape = (1, tm, width) if lead is None else (None, 1, tm, width)

    def first(b, i):
        return pre + (jnp.minimum(b, nbp - 1), jnp.where(b < nbp, i, nt - 1), 0)

    def second(b, i):
        return pre + (jnp.maximum(b - nbp, 0) + off, jnp.where(b < nbp, 0, i), 0)

    return pl.BlockSpec(shape, first), pl.BlockSpec(shape, second)


def _pick(nbp, a_ref, b_ref):
    return jnp.where(pl.program_id(0) < nbp, a_ref[0], b_ref[0])


def _in_proj_kernel(xa_ref, xb_ref, g_ref, wa_ref, wt_ref, qvg_ref, kt_ref, hy_ref, pool_ref, *, nbp):
    tm = xa_ref.shape[1]
    h = _rms(_pick(nbp, xa_ref, xb_ref), g_ref[...]).astype(BF16)
    a = jnp.dot(h, wa_ref[...], preferred_element_type=F32).astype(BF16)
    qvg_ref[0] = a[:, :QVG_WIDTH]
    pool_ref[0] = a[:, QVG_WIDTH:]
    t = lax.dot_general(wt_ref[...], h, (((1,), (1,)), ((), ())), preferred_element_type=F32)
    kt_ref[0] = t[:RET_WIDTH].astype(BF16)
    hy_ref[...] = t[RET_WIDTH:].reshape(HY_IN_WIDTH // SUBLANES, SUBLANES, tm)


def _in_proj(x, B, g, wa, wt, tm):
    xa, xb, nbp, off = x
    L = xa.shape[1]
    nt = L // tm
    return pl.pallas_call(
        functools.partial(_in_proj_kernel, nbp=nbp),
        grid=(B, nt),
        in_specs=[
            *_pair_specs(tm, D_MODEL, nt, nbp, off),
            pl.BlockSpec((1, D_MODEL), lambda b, i: (0, 0)),
            pl.BlockSpec((D_MODEL, QVG_WIDTH + POOL_WIDTH), lambda b, i: (0, 0)),
            pl.BlockSpec((RET_WIDTH + HY_IN_WIDTH, D_MODEL), lambda b, i: (0, 0)),
        ],
        out_specs=[
            pl.BlockSpec((1, tm, QVG_WIDTH), lambda b, i: (b, i, 0)),
            pl.BlockSpec((1, RET_WIDTH, tm), lambda b, i: (b, 0, i)),
            pl.BlockSpec((HY_IN_WIDTH // SUBLANES, SUBLANES, tm), lambda b, i: (0, b, i)),
            pl.BlockSpec((1, tm, POOL_WIDTH), lambda b, i: (b, i, 0)),
        ],
        out_shape=[
            jax.ShapeDtypeStruct((B, L, QVG_WIDTH), BF16),
            jax.ShapeDtypeStruct((B, RET_WIDTH, L), BF16),
            jax.ShapeDtypeStruct((HY_IN_WIDTH // SUBLANES, B * SUBLANES, L), F32),
            jax.ShapeDtypeStruct((B, L, POOL_WIDTH), BF16),
        ],
        compiler_params=_cparams(("arbitrary", "arbitrary")),
        name="in_proj",
    )(xa, xb, g, wa, wt)


def _log_sigmoid(x):
    return jnp.minimum(x, 0.0) - jnp.log(1.0 + jnp.exp(-jnp.abs(x)))


def _retention_kernel(dec_ref, cos_ref, sin_ref, cost_ref, sint_ref, gn_ref, q_ref, kt_ref, v_ref, g_ref, o_ref,
                      qr_s, kr_s, kv_s, st_s, tab_s):
    L = q_ref.shape[1]
    n_chunks = L // CHUNK
    C = CHUNK
    scale = HEAD_DIM ** -0.5

    lg = _log_sigmoid(dec_ref[0])
    lgf, lgb = lg[0:1], lg[1:2]
    ri = lax.broadcasted_iota(jnp.int32, (C, C), 0)
    ci = lax.broadcasted_iota(jnp.int32, (C, C), 1)
    rf = ri.astype(F32)
    cf = ci.astype(F32)
    dist = jnp.abs(ri - ci).astype(F32)
    tab_s[0] = jnp.exp(dist * jnp.where(ri >= ci, lgf, lgb)) * scale
    tab_s[1] = jnp.exp((rf + 1.0) * lgf) * scale
    tab_s[2] = jnp.exp((C - rf) * lgb) * scale
    tab_s[3] = jnp.exp((C - 1.0 - cf) * lgf)
    tab_s[4] = jnp.exp(cf * lgb)
    dec_f = jnp.exp(C * lgf)
    dec_b = jnp.exp(C * lgb)

    qf = q_ref[0].astype(F32)
    qr_s[...] = qf * cos_ref[...] + pltpu.roll(qf, HEAD_DIM // 2, axis=1) * sin_ref[...]
    kf = kt_ref[0].astype(F32)
    kr_s[...] = kf * cost_ref[...] + pltpu.roll(kf, HEAD_DIM // 2, axis=0) * sint_ref[...]

    def summaries(n, carry):
        c0 = pl.multiple_of(n * C, C)
        kt = kr_s[:, pl.ds(c0, C)]
        lhs = jnp.concatenate([kt * tab_s[3], kt * tab_s[4]], axis=0).astype(BF16)
        kv_s[n] = jnp.dot(lhs, v_ref[0, pl.ds(c0, C), :], preferred_element_type=F32)
        return carry

    lax.fori_loop(0, n_chunks, summaries, 0, unroll=RET_UNROLL)

    def scan(i, carry):
        sf, sb = carry
        m = n_chunks - 1 - i
        st_s[i, :C, :] = sf.astype(BF16)
        st_s[m, C:, :] = sb.astype(BF16)
        return sf * dec_f + kv_s[i, :C, :], sb * dec_b + kv_s[m, C:, :]

    zero = jnp.zeros((C, C), F32)
    lax.fori_loop(0, n_chunks, scan, (zero, zero))

    gain = gn_ref[...]

    def outputs(n, carry):
        c0 = pl.multiple_of(n * C, C)
        qn = qr_s[pl.ds(c0, C), :]
        vn = v_ref[0, pl.ds(c0, C), :]
        sc = jnp.dot(qn.astype(BF16), kr_s[:, pl.ds(c0, C)].astype(BF16), preferred_element_type=F32)
        lhs = jnp.concatenate([sc * tab_s[0], qn * tab_s[1], qn * tab_s[2]], axis=1).astype(BF16)
        rhs = jnp.concatenate([vn, st_s[n]], axis=0)
        o = jnp.dot(lhs, rhs, preferred_element_type=F32)
        mu = jnp.mean(o, axis=-1, keepdims=True)
        oc = o - mu
        var = jnp.mean(oc * oc, axis=-1, keepdims=True)
        gate = g_ref[0, pl.ds(c0, C), :].astype(F32)
        y = oc * lax.rsqrt(var + EPS) * gain * (gate * jax.nn.sigmoid(gate))
        o_ref[0, pl.ds(c0, C), :] = y.astype(BF16)
        return carry

    lax.fori_loop(0, n_chunks, outputs, 0, unroll=RET_UNROLL)


def _retention(qvg, kt, dec, cos, sin, cos_t, sin_t, gn):
    B, L, _ = qvg.shape
    H = RET_HEADS

    def col(off):
        return pl.BlockSpec((1, L, HEAD_DIM), lambda b, h: (b, 0, off + h))

    table = pl.BlockSpec((L, HEAD_DIM), lambda b, h: (0, 0))
    table_t = pl.BlockSpec((HEAD_DIM, L), lambda b, h: (0, 0))
    return pl.pallas_call(
        _retention_kernel,
        grid=(B, H),
        in_specs=[
            pl.BlockSpec((1, 2, LANES), lambda b, h: (h, 0, 0)),
            table, table, table_t, table_t,
            pl.BlockSpec((1, HEAD_DIM), lambda b, h: (0, h)),
            col(0),
            pl.BlockSpec((1, HEAD_DIM, L), lambda b, h: (b, h, 0)),
            col(H), col(2 * H),
        ],
        out_specs=pl.BlockSpec((1, L, HEAD_DIM), lambda b, h: (b, 0, h)),
        out_shape=jax.ShapeDtypeStruct((B, L, RET_WIDTH), BF16),
        scratch_shapes=[
            pltpu.VMEM((L, HEAD_DIM), F32),
            pltpu.VMEM((HEAD_DIM, L), F32),
            pltpu.VMEM((L // CHUNK, 2 * CHUNK, CHUNK), F32),
            pltpu.VMEM((L // CHUNK, 2 * CHUNK, CHUNK), BF16),
            pltpu.VMEM((5, CHUNK, CHUNK), F32),
        ],
        compiler_params=_cparams(("parallel", "parallel")),
        name="retention",
    )(dec, cos, sin, cos_t, sin_t, gn, qvg, kt, qvg, qvg)


def _filter_kernel(w1_ref, b1_ref, fr_ref, w2_ref, b2_ref, w3_ref, kk_ref):
    L = kk_ref.shape[1] // 2
    hi = lax.Precision.HIGHEST
    r = lax.broadcasted_iota(jnp.int32, (HY_HIDDEN, L), 0)
    lane = lax.broadcasted_iota(jnp.int32, (HY_HIDDEN, L), 1)
    band_idx = jnp.where(r <= HY_EMB_BANDS, r - 1, r - 1 - HY_EMB_BANDS).astype(F32)
    band = 1e-4 + band_idx * ((HY_EMB_BANDS - 1 - 1e-4) / (HY_EMB_BANDS - 1))
    crow = lax.broadcasted_iota(jnp.int32, (HY_WIDTH, L), 0).astype(F32)
    delta = jnp.abs(HY_MIN_DECAY + crow * ((HY_MAX_DECAY - HY_MIN_DECAY) / (HY_WIDTH - 1)))
    clane = lax.broadcasted_iota(jnp.int32, (HY_WIDTH, L), 1)
    fr = fr_ref[...]

    def taps(s_int, s_row_int, w3):
        s = s_int.astype(F32)
        t = s / (L - 1.0)
        ang = (2.0 * math.pi / L) * s * band
        feats = jnp.where(r == 0, t,
                          jnp.where(r <= HY_EMB_BANDS, jnp.cos(ang),
                                    jnp.where(r <= 2 * HY_EMB_BANDS, -jnp.sin(ang), 0.0)))
        h = jnp.sin(fr * (jnp.dot(w1_ref[...], feats, precision=hi, preferred_element_type=F32) + b1_ref[...]))
        h = jnp.sin(fr * (jnp.dot(w2_ref[...], h, precision=hi, preferred_element_type=F32) + b2_ref[...]))
        h = jnp.dot(w3, h, precision=hi, preferred_element_type=F32)
        return h * jnp.exp(-(s_row_int.astype(F32) / (L - 1.0)) * delta)

    h_f = taps(lane, clane, w3_ref[0, 0])
    h_b = taps(L - lane, L - clane, w3_ref[0, 1])
    h_b = jnp.where(clane == 0, 0.0, h_b)
    norm = jnp.sum(jnp.abs(h_f), axis=-1, keepdims=True) + jnp.sum(jnp.abs(h_b), axis=-1, keepdims=True)
    inv = 1.0 / norm
    kk_ref[:, :L] = h_b * inv
    kk_ref[:, L:] = h_f * inv


def _hyena_filters(w1t, b1, fr, w2t, b2, w3t, L):
    small = lambda shape: pl.BlockSpec(shape, lambda o: (0,) * len(shape))
    return pl.pallas_call(
        _filter_kernel,
        grid=(2,),
        in_specs=[
            small((HY_HIDDEN, HY_HIDDEN)), small((HY_HIDDEN, 1)), small((HY_HIDDEN, 1)),
            small((HY_HIDDEN, HY_HIDDEN)), small((HY_HIDDEN, 1)),
            pl.BlockSpec((1, 2, HY_WIDTH, HY_HIDDEN), lambda o: (o, 0, 0, 0)),
        ],
        out_specs=pl.BlockSpec((HY_WIDTH, 2 * L), lambda o: (o, 0)),
        out_shape=jax.ShapeDtypeStruct((2 * HY_WIDTH, 2 * L), F32),
        compiler_params=_cparams(("parallel",)),
        name="hyena_filters",
    )(w1t, b1, fr, w2t, b2, w3t)


def _hyena_kernel(kk1_ref, kk2_ref, xv_ref, x1_ref, x2_ref, cwv_ref, cw1_ref, cw2_ref, bias_ref,
                  out_ref, cv_s, c1_s, c2_s, o_s, toep_s, u_s, x1u_s, y_s, *, B, BP):
    L = xv_ref.shape[1]
    T = HY_BLOCK
    NB = L // T
    R = SUBLANES * B
    TPB = T // LANES

    @pl.when(pl.program_id(0) == 0)
    def _():
        toep_s[...] = jnp.zeros_like(toep_s)
        u_s[...] = jnp.zeros_like(u_s)
        x1u_s[...] = jnp.zeros_like(x1u_s)

    lane = lax.broadcasted_iota(jnp.int32, (R, L), 1)

    def dwconv(x_ref, w_ref, dst):
        x = x_ref[...]
        xp = jnp.where(lane == 0, 0.0, pltpu.roll(x, 1, axis=1))
        xn = jnp.where(lane == L - 1, 0.0, pltpu.roll(x, L - 1, axis=1))
        w = w_ref[0]
        y = (xp.reshape(B, SUBLANES, L) * w[0][None] + x.reshape(B, SUBLANES, L) * w[1][None]
             + xn.reshape(B, SUBLANES, L) * w[2][None]).reshape(R, L)
        for j in range(L // LANES):
            dst[j] = y[:, j * LANES:(j + 1) * LANES]

    dwconv(xv_ref, cwv_ref, cv_s)
    dwconv(x1_ref, cw1_ref, c1_s)
    dwconv(x2_ref, cw2_ref, c2_s)

    def build_toeplitz(kk_ref, c):
        row = jnp.broadcast_to(kk_ref[pl.ds(c, 1), :], (BF16_ROWS, 2 * L))
        for g in range(LANES // BF16_ROWS):
            rolled = pltpu.roll(row, BF16_ROWS * g, axis=1, stride=1, stride_axis=0).astype(BF16)
            toep_s[BF16_ROWS * g:BF16_ROWS * (g + 1), :] = rolled
            toep_s[LANES + BF16_ROWS * g:LANES + BF16_ROWS * (g + 1), LANES:] = rolled[:, :2 * L - LANES]

    def long_conv():
        y_s[...] = jnp.zeros_like(y_s)
        for d in range(-(NB - 1), NB):
            n = NB - abs(d)
            src = max(0, -d) * BP
            dst = max(0, d) * BP
            m = toep_s[:, L + d * T:L + (d + 1) * T]
            lhs = u_s[src:src + n * BP, :].astype(BF16)
            y_s[dst:dst + n * BP, :] += jnp.dot(lhs, m, preferred_element_type=F32)

    def channel(c, carry):
        rows = pl.ds(c, B, stride=SUBLANES)
        for j in range(L // LANES):
            J, l0 = j // TPB, (j % TPB) * LANES
            u_s[J * BP:J * BP + B, l0:l0 + LANES] = cv_s.at[j][rows, :]
            x1u_s[J * BP:J * BP + B, l0:l0 + LANES] = c1_s.at[j][rows, :]
        b0 = bias_ref[0, 0, pl.ds(c, 1), :]
        b1 = bias_ref[0, 1, pl.ds(c, 1), :]
        build_toeplitz(kk1_ref, c)
        long_conv()
        u_s[...] = x1u_s[...] * (y_s[...] + u_s[...] * b0)
        build_toeplitz(kk2_ref, c)
        long_conv()
        for j in range(L // LANES):
            J, l0 = j // TPB, (j % TPB) * LANES
            yy = y_s[J * BP:J * BP + B, l0:l0 + LANES] + u_s[J * BP:J * BP + B, l0:l0 + LANES] * b1[:, :LANES]
            o_s.at[j][rows, :] = c2_s.at[j][rows, :] * yy
        return carry

    lax.fori_loop(0, SUBLANES, channel, 0)
    for j in range(L // LANES):
        out_ref[:, j * LANES:(j + 1) * LANES] = o_s[j]


def _hyena(kk, hy_t, cw, bias, B):
    n_in, R, L = hy_t.shape
    NB = L // HY_BLOCK
    BP = -(-B // SUBLANES) * SUBLANES
    n_c = HY_WIDTH // SUBLANES

    def sect(s):
        return pl.BlockSpec((None, R, L), lambda c: (s * n_c + c, 0, 0))

    def cws(s):
        return pl.BlockSpec((1, 3, SUBLANES, L), lambda c: (s * n_c + c, 0, 0, 0))

    return pl.pallas_call(
        functools.partial(_hyena_kernel, B=B, BP=BP),
        grid=(n_c,),
        in_specs=[
            pl.BlockSpec((SUBLANES, 2 * L), lambda c: (c, 0)),
            pl.BlockSpec((SUBLANES, 2 * L), lambda c: (n_c + c, 0)),
            sect(0), sect(1), sect(2), cws(0), cws(1), cws(2),
            pl.BlockSpec((1, 2, SUBLANES, HY_BLOCK), lambda c: (c, 0, 0, 0)),
        ],
        out_specs=pl.BlockSpec((None, R, L), lambda c: (c, 0, 0)),
        out_shape=jax.ShapeDtypeStruct((n_c, R, L), F32),
        scratch_shapes=[
            pltpu.VMEM((L // LANES, R, LANES), F32), pltpu.VMEM((L // LANES, R, LANES), F32),
            pltpu.VMEM((L // LANES, R, LANES), F32), pltpu.VMEM((L // LANES, R, LANES), F32),
            pltpu.VMEM((2 * LANES, 2 * L), BF16),
            pltpu.VMEM((NB * BP, HY_BLOCK), F32),
            pltpu.VMEM((NB * BP, HY_BLOCK), F32),
            pltpu.VMEM((NB * BP, HY_BLOCK), F32),
        ],
        compiler_params=_cparams(("arbitrary",)),
        name="hyena",
    )(kk, kk, hy_t, hy_t, hy_t, cw, cw, cw, bias)


def _pool_kernel(u_ref, w_ref, sc_ref, o_ref):
    L = u_ref.shape[1]
    u = u_ref[0].astype(F32)
    t = lax.broadcasted_iota(jnp.int32, (L, POOL_WIDTH), 0)
    lane = lax.broadcasted_iota(jnp.int32, (L, POOL_WIDTH), 1)
    pad = jnp.zeros((POOL_PAD, POOL_WIDTH), F32)
    up = jnp.concatenate([pad, u, pad], axis=0)
    n = L + 2 * POOL_PAD

    def both(x, k):
        return pltpu.roll(x, k, axis=0) + pltpu.roll(x, n - k, axis=0)

    q2 = pltpu.roll(up, 1, axis=0) + up
    q4 = both(q2, 1)
    q8 = both(q4, 2)
    q16 = both(q8, 4)
    p2, p4, p8, p16 = (q[POOL_PAD:POOL_PAD + L] for q in (q2, q4, q8, q16))
    g = lane // POOL_GROUP_DIM
    half = jnp.where(g == 0, 1, jnp.where(g == 1, 2, jnp.where(g == 2, 4, 8)))
    total = jnp.where(g == 0, p2, jnp.where(g == 1, p4, jnp.where(g == 2, p8, p16)))
    lo = jnp.maximum(t - half, 0)
    hi = jnp.minimum(t + half - 1, L - 1)
    cnt = (hi - lo + 1).astype(F32)
    d = (total / cnt - u).astype(BF16)
    o_ref[0] = (jnp.dot(d, w_ref[...], preferred_element_type=F32) * sc_ref[...]).astype(BF16)


def _pool(pool_in, w_bd, scale):
    B, L, _ = pool_in.shape
    return pl.pallas_call(
        _pool_kernel,
        grid=(B,),
        in_specs=[
            pl.BlockSpec((1, L, POOL_WIDTH), lambda b: (b, 0, 0)),
            pl.BlockSpec((POOL_WIDTH, POOL_WIDTH), lambda b: (0, 0)),
            pl.BlockSpec((1, POOL_WIDTH), lambda b: (0, 0)),
        ],
        out_specs=pl.BlockSpec((1, L, POOL_WIDTH), lambda b: (b, 0, 0)),
        out_shape=jax.ShapeDtypeStruct((B, L, POOL_WIDTH), BF16),
        compiler_params=_cparams(("parallel",)),
        name="pool_mixer",
    )(pool_in, w_bd, scale)


def _mix_out_kernel(xa_ref, xb_ref, ret_ref, hy_ref, pool_ref, wr_ref, wh_ref, wp_ref, g_ref, x1_ref, h2_ref, *, nbp):
    tm = xa_ref.shape[1]
    hy_t = hy_ref[...].reshape(HY_WIDTH, tm).astype(BF16)
    mix = jnp.dot(ret_ref[0], wr_ref[...], preferred_element_type=F32)
    mix += lax.dot_general(hy_t, wh_ref[...], (((0,), (0,)), ((), ())), preferred_element_type=F32)
    mix += jnp.dot(pool_ref[0], wp_ref[...], preferred_element_type=F32)
    x1 = _pick(nbp, xa_ref, xb_ref) + mix
    x1_ref[0] = x1
    h2_ref[0] = _rms(x1, g_ref[...]).astype(BF16)


def _mix_out(x, ret, hy_o, pool_o, wr, wh, wp, g, tm):
    xa, xb, nbp, off = x
    B, L, _ = ret.shape
    nt = L // tm
    return pl.pallas_call(
        functools.partial(_mix_out_kernel, nbp=nbp),
        grid=(B, nt),
        in_specs=[
            *_pair_specs(tm, D_MODEL, nt, nbp, off),
            pl.BlockSpec((1, tm, RET_WIDTH), lambda b, i: (b, i, 0)),
            pl.BlockSpec((HY_WIDTH // SUBLANES, SUBLANES, tm), lambda b, i: (0, b, i)),
            pl.BlockSpec((1, tm, POOL_WIDTH), lambda b, i: (b, i, 0)),
            pl.BlockSpec((RET_WIDTH, D_MODEL), lambda b, i: (0, 0)),
            pl.BlockSpec((HY_WIDTH, D_MODEL), lambda b, i: (0, 0)),
            pl.BlockSpec((POOL_WIDTH, D_MODEL), lambda b, i: (0, 0)),
            pl.BlockSpec((1, D_MODEL), lambda b, i: (0, 0)),
        ],
        out_specs=[
            pl.BlockSpec((1, tm, D_MODEL), lambda b, i: (b, i, 0)),
            pl.BlockSpec((1, tm, D_MODEL), lambda b, i: (b, i, 0)),
        ],
        out_shape=[
            jax.ShapeDtypeStruct((B, L, D_MODEL), F32),
            jax.ShapeDtypeStruct((B, L, D_MODEL), BF16),
        ],
        compiler_params=_cparams(("arbitrary", "arbitrary")),
        name="mix_out",
    )(xa, xb, ret, hy_o, pool_o, wr, wh, wp, g)


def _ffn_kernel(h_ref, hp_ref, hn_ref, x1_ref, wup_ref, cw_ref, wd_ref, pa_ref, pb_ref, pw_ref, pg_ref,
                pn_ref, fin_ref, *o_refs, nbp, final):
    tm = h_ref.shape[1]
    i = pl.program_id(1)
    nt = pl.num_programs(1)

    hx = jnp.concatenate([h_ref[0], hp_ref[0], hn_ref[0]], axis=0)
    row = lax.broadcasted_iota(jnp.int32, (tm, FFN_SUB), 0)
    n_sub = D_FF // FFN_SUB

    def up_proj(s):
        c0 = s * FFN_SUB
        gate = jnp.dot(hx, wup_ref[:, c0:c0 + FFN_SUB], preferred_element_type=F32)
        up = jnp.dot(hx[:tm], wup_ref[:, D_FF + c0:D_FF + c0 + FFN_SUB], preferred_element_type=F32)
        return gate, up

    acc = x1_ref[0]
    nxt = up_proj(0)
    group, g0 = [], 0
    for s in range(n_sub):
        c0 = s * FFN_SUB
        gx, up = nxt
        if s + 1 < n_sub:
            nxt = up_proj(s + 1)
        gate = gx[:tm]
        g_prev = jnp.where(i == 0, 0.0, gx[tm + BF16_ROWS - 1:tm + BF16_ROWS])
        g_next = jnp.where(i == nt - 1, 0.0, gx[tm + BF16_ROWS:tm + BF16_ROWS + 1])
        g_up = jnp.where(row == 0, g_prev, pltpu.roll(gate, 1, axis=0))
        g_dn = jnp.where(row == tm - 1, g_next, pltpu.roll(gate, tm - 1, axis=0))
        cw = cw_ref[:, c0:c0 + FFN_SUB]
        gc = g_up * cw[0:1] + gate * cw[1:2] + g_dn * cw[2:3]
        group.append((jax.nn.gelu(gc) * up).astype(BF16))
        if len(group) == FFN_DOWN_GROUP or s == n_sub - 1:
            a = jnp.concatenate(group, axis=1)
            acc = acc + jnp.dot(a, wd_ref[g0:c0 + FFN_SUB, :], preferred_element_type=F32)
            group, g0 = [], c0 + FFN_SUB

    x2 = acc
    p = _pick(nbp, pa_ref, pb_ref).astype(BF16)
    e = _rms(jnp.dot(p, pw_ref[...], preferred_element_type=F32), pn_ref[...])
    gt = jax.nn.sigmoid(jnp.dot(x2.astype(BF16), pg_ref[...], preferred_element_type=F32))
    x3 = x2 + gt * e
    if not final:
        o_refs[0][0] = x3
        return
    y = _rms(x3, fin_ref[...])
    b = pl.program_id(0)

    @pl.when(b < nbp)
    def _():
        o_refs[0][0] = y

    @pl.when(b >= nbp)
    def _():
        o_refs[1][0] = y


def _ffn(h2, x1, w_up, cw, w_down, p, layer, ple_w, ple_gate, ple_norm, fin, tm, final):
    pa, pb, nbp, off = p
    B, L, _ = x1.shape
    nt = L // tm
    halo = BF16_ROWS
    nh = L // halo
    per = tm // halo
    full = lambda shape: pl.BlockSpec(shape, lambda b, i: (0,) * len(shape), pipeline_mode=pl.Buffered(1))
    tile = pl.BlockSpec((1, tm, D_MODEL), lambda b, i: (b, i, 0))
    if final:
        out_specs = list(_pair_specs(tm, D_MODEL, nt, nbp, 0))
        out_shape = [jax.ShapeDtypeStruct((nbp, L, D_MODEL), F32), jax.ShapeDtypeStruct((B - nbp, L, D_MODEL), F32)]
    else:
        out_specs = [tile]
        out_shape = [jax.ShapeDtypeStruct((B, L, D_MODEL), F32)]
    return pl.pallas_call(
        functools.partial(_ffn_kernel, nbp=nbp, final=final),
        grid=(B, nt),
        in_specs=[
            tile,
            pl.BlockSpec((1, halo, D_MODEL), lambda b, i: (b, jnp.maximum(i * per - 1, 0), 0)),
            pl.BlockSpec((1, halo, D_MODEL), lambda b, i: (b, jnp.minimum((i + 1) * per, nh - 1), 0)),
            tile,
            full((D_MODEL, 2 * D_FF)), full((3, D_FF)), full((D_FF, D_MODEL)),
            *_pair_specs(tm, PLE_DIM, nt, nbp, off, lead=layer),
            full((PLE_DIM, D_MODEL)), full((D_MODEL, D_MODEL)), full((1, D_MODEL)), full((1, D_MODEL)),
        ],
        out_specs=out_specs,
        out_shape=out_shape,
        compiler_params=_cparams(("arbitrary", "arbitrary")),
        name="ffn_ple",
    )(h2, h2, h2, x1, w_up, cw, w_down, pa, pb, ple_w, ple_gate, ple_norm, fin)


def _rope_tables(L):
    half = HEAD_DIM // 2
    inv = ROPE_THETA ** (-jnp.arange(half, dtype=F32) / half)
    ang = jnp.arange(L, dtype=F32)[:, None] * inv[None, :]
    cos, sin = jnp.cos(ang), jnp.sin(ang)
    return jnp.concatenate([cos, cos], axis=-1), jnp.concatenate([-sin, sin], axis=-1)


def _block_diag(pool_w):
    out = jnp.zeros((POOL_WIDTH, POOL_WIDTH), pool_w.dtype)
    for g in range(len(POOL_WINDOWS)):
        sl = slice(g * POOL_GROUP_DIM, (g + 1) * POOL_GROUP_DIM)
        out = out.at[sl, sl].set(pool_w[g])
    return out


def _layer(x, B, p, layer, lw, cos, sin, fin, tm, final):
    (norm_mix, w_in, dec_f, dec_b, ret_gn, hy_conv, hy_w1, hy_b1, hy_freq, hy_w2, hy_b2, hy_w3, hy_bias,
     pool_w, pool_scale, w_out, norm_ffn, w_up, ffn_conv, w_down, ple_w, ple_gate, ple_norm) = lw
    L = x[0].shape[1]
    R = RET_WIDTH

    w_q, w_k, w_v, w_g, w_hy, w_pool = jnp.split(w_in, [R, 2 * R, 3 * R, 4 * R, 4 * R + HY_IN_WIDTH], axis=1)
    wa = jnp.concatenate([w_q, w_v, w_g, w_pool], axis=1).astype(BF16)
    wt = jnp.concatenate([w_k, w_hy], axis=1).T.astype(BF16)
    qvg, kt, hy_t, pool_in = _in_proj(x, B, norm_mix[None], wa, wt, tm)

    dec = jnp.broadcast_to(jnp.stack([dec_f, dec_b], axis=1)[:, :, None], (RET_HEADS, 2, LANES)).astype(F32)
    ret = _retention(qvg, kt, dec, cos, sin, cos.T, sin.T, ret_gn[None])

    w1t = jnp.zeros((HY_HIDDEN, HY_HIDDEN), F32).at[:, :hy_w1.shape[0]].set(hy_w1.T)
    w3t = hy_w3.T.reshape(2, 2, HY_WIDTH, HY_HIDDEN)
    kk = _hyena_filters(w1t, hy_b1[:, None], hy_freq[:, None], hy_w2.T, hy_b2[:, None], w3t, L)
    n_sec = HY_IN_WIDTH // SUBLANES
    cw = jnp.broadcast_to(hy_conv.reshape(3, n_sec, SUBLANES).transpose(1, 0, 2)[..., None],
                          (n_sec, 3, SUBLANES, L))
    bias = jnp.broadcast_to(hy_bias.reshape(2, HY_WIDTH // SUBLANES, SUBLANES).transpose(1, 0, 2)[..., None],
                            (HY_WIDTH // SUBLANES, 2, SUBLANES, HY_BLOCK))
    hy_o = _hyena(kk, hy_t, cw, bias, B)

    pool_o = _pool(pool_in, _block_diag(pool_w).astype(BF16), pool_scale[None])

    wo = w_out.astype(BF16)
    x1, h2 = _mix_out(x, ret, hy_o, pool_o, wo[:R], wo[R:R + HY_WIDTH], wo[R + HY_WIDTH:], norm_ffn[None], tm)

    return _ffn(h2, x1, w_up.astype(BF16), ffn_conv, w_down.astype(BF16), p, layer, ple_w.astype(BF16),
                ple_gate.astype(BF16), ple_norm[None], fin, tm, final)


def _trunk(x_a, x_b, p_a, p_b, layer_weights, norm_final, tm):
    depth = p_a.shape[0]
    nbp, L = x_a.shape[0], x_a.shape[1]
    B = nbp + x_b.shape[0]
    cos, sin = _rope_tables(L)
    fin = norm_final[None]
    x = (x_a, x_b, nbp, 0)
    p = (p_a, p_b, nbp, 0)
    for i in range(depth):
        out = _layer(x, B, p, i, [w[i] for w in layer_weights], cos, sin, fin, tm, i == depth - 1)
        x = (out[0], out[0], nbp, nbp)
    return out[0], out[1]


def kernel(x_prompt, x_sample, p_prompt, p_sample, norm_mix, w_in, ret_decay_fwd, ret_decay_bwd, ret_gn,
           hy_short_conv, hy_w1, hy_b1, hy_freq, hy_w2, hy_b2, hy_w3, hy_bias, pool_w, pool_scale, w_out,
           norm_ffn, ffn_w_up, ffn_conv, ffn_w_down, ple_w, ple_gate_w, ple_norm, norm_final):
    layer_weights = (norm_mix, w_in, ret_decay_fwd, ret_decay_bwd, ret_gn, hy_short_conv, hy_w1, hy_b1,
                     hy_freq, hy_w2, hy_b2, hy_w3, hy_bias, pool_w, pool_scale, w_out, norm_ffn, ffn_w_up,
                     ffn_conv, ffn_w_down, ple_w, ple_gate_w, ple_norm)
    return _trunk(x_prompt, x_sample, p_prompt, p_sample, layer_weights, norm_final, ROW_TILE)
```

```python
import functools
import math

import jax
import jax.numpy as jnp
from jax import lax
from jax.experimental import pallas as pl
from jax.experimental.pallas import tpu as pltpu

F32 = jnp.float32
BF16 = jnp.bfloat16

D_MODEL = 1024
RET_WIDTH = 512
RET_HEADS = 4
HEAD_DIM = 128
HY_WIDTH = 256
POOL_WIDTH = 256
POOL_GROUP_DIM = 64
POOL_WINDOWS = (2, 4, 8, 16)
POOL_PAD = 16
D_FF = 2816
PLE_DIM = 256
CHUNK = 128
ROPE_THETA = 10000.0
HY_EMB_BANDS = 16
HY_HIDDEN = 64
HY_MIN_DECAY = math.log(1e-2) / 1.5
HY_MAX_DECAY = math.log(1e-2) / 0.3
EPS = 1e-6

LANES = 128
SUBLANES = 8
BF16_ROWS = 16
MXU_DIM = 256
VMEM_LIMIT = 56 * 1024 * 1024

HY_BLOCK = MXU_DIM
HY_PAIR_UNROLL = 1
QVG_WIDTH = 3 * RET_WIDTH
RET_UNROLL = 4
HY_IN_WIDTH = 3 * HY_WIDTH
ROW_TILE = 512
FFN_SUB = MXU_DIM
FFN_DOWN_GROUP = 4


def _cparams(sem):
    return pltpu.CompilerParams(dimension_semantics=sem, vmem_limit_bytes=VMEM_LIMIT)


def _rms(x, g):
    return x * lax.rsqrt(jnp.mean(x * x, axis=-1, keepdims=True) + EPS) * g


def _pair_specs(tm, width, nt, nbp, off, lead=None):
    pre = () if lead is None else (lead,)
    shape = (1, tm, width) if lead is None else (None, 1, tm, width)

    def first(b, i):
        return pre + (jnp.minimum(b, nbp - 1), jnp.where(b < nbp, i, nt - 1), 0)

    def second(b, i):
        return pre + (jnp.maximum(b - nbp, 0) + off, jnp.where(b < nbp, 0, i), 0)

    return pl.BlockSpec(shape, first), pl.BlockSpec(shape, second)


def _pick(nbp, a_ref, b_ref):
    return jnp.where(pl.program_id(0) < nbp, a_ref[0], b_ref[0])


def _pair_halo_specs(rows, width, tm, L, nbp, off):
    per, last = tm // rows, L // rows - 1

    def make(first, after):
        def index(b, i):
            r = jnp.minimum((i + 1) * per, last) if after else jnp.maximum(i * per - 1, 0)
            if first:
                return (jnp.minimum(b, nbp - 1), jnp.where(b < nbp, r, 0), 0)
            return (jnp.maximum(b - nbp, 0) + off, jnp.where(b < nbp, 0, r), 0)
        return pl.BlockSpec((1, rows, width), index)

    return make(True, False), make(False, False), make(True, True), make(False, True)


def _dwconv3_rows(u, tm, cw, first_tile, last_tile):
    main = u[:tm]
    prev = jnp.where(first_tile, 0.0, u[tm + BF16_ROWS - 1:tm + BF16_ROWS])
    nxt = jnp.where(last_tile, 0.0, u[tm + BF16_ROWS:tm + BF16_ROWS + 1])
    row = lax.broadcasted_iota(jnp.int32, main.shape, 0)
    up = jnp.where(row == 0, prev, pltpu.roll(main, 1, axis=0))
    dn = jnp.where(row == tm - 1, nxt, pltpu.roll(main, tm - 1, axis=0))
    return up * cw[0:1] + main * cw[1:2] + dn * cw[2:3]


def _in_proj_kernel(xa_ref, xb_ref, xpa_ref, xpb_ref, xna_ref, xnb_ref, g_ref, wa_ref, wk_ref, wh_ref, cw_ref,
                    qvg_ref, kt_ref, hy_ref, pool_ref, *, nbp):
    tm = xa_ref.shape[1]
    i = pl.program_id(1)
    g = g_ref[...]
    h = _rms(_pick(nbp, xa_ref, xb_ref), g).astype(BF16)
    hx = jnp.concatenate([h, _rms(_pick(nbp, xpa_ref, xpb_ref), g).astype(BF16),
                          _rms(_pick(nbp, xna_ref, xnb_ref), g).astype(BF16)], axis=0)
    u = jnp.dot(hx, wh_ref[...], preferred_element_type=F32)
    y = _dwconv3_rows(u, tm, cw_ref[...], i == 0, i == pl.num_programs(1) - 1)
    yt = y.T
    for j in range(tm // LANES):
        hy_ref[:, j, :, :] = yt[:, j * LANES:(j + 1) * LANES].reshape(HY_IN_WIDTH // SUBLANES, SUBLANES, LANES)
    kt = lax.dot_general(wk_ref[...], h, (((1,), (1,)), ((), ())), preferred_element_type=F32)
    kt_ref[0] = kt.astype(BF16)
    a = jnp.dot(h, wa_ref[...], preferred_element_type=F32).astype(BF16)
    qvg_ref[0] = a[:, :QVG_WIDTH]
    pool_ref[0] = a[:, QVG_WIDTH:]


def _in_proj(x, B, g, wa, wk_t, wh, cw, tm):
    xa, xb, nbp, off = x
    L = xa.shape[1]
    nt = L // tm
    const = lambda shape: pl.BlockSpec(shape, lambda b, i: (0,) * len(shape))
    return pl.pallas_call(
        functools.partial(_in_proj_kernel, nbp=nbp),
        grid=(B, nt),
        in_specs=[
            *_pair_specs(tm, D_MODEL, nt, nbp, off),
            *_pair_halo_specs(BF16_ROWS, D_MODEL, tm, L, nbp, off),
            const((1, D_MODEL)),
            const((D_MODEL, QVG_WIDTH + POOL_WIDTH)),
            const((RET_WIDTH, D_MODEL)),
            const((D_MODEL, HY_IN_WIDTH)),
            const((3, HY_IN_WIDTH)),
        ],
        out_specs=[
            pl.BlockSpec((1, tm, QVG_WIDTH), lambda b, i: (b, i, 0)),
            pl.BlockSpec((1, RET_WIDTH, tm), lambda b, i: (b, 0, i)),
            pl.BlockSpec((HY_IN_WIDTH // SUBLANES, tm // LANES, SUBLANES, LANES), lambda b, i: (0, i, b, 0)),
            pl.BlockSpec((1, tm, POOL_WIDTH), lambda b, i: (b, i, 0)),
        ],
        out_shape=[
            jax.ShapeDtypeStruct((B, L, QVG_WIDTH), BF16),
            jax.ShapeDtypeStruct((B, RET_WIDTH, L), BF16),
            jax.ShapeDtypeStruct((HY_IN_WIDTH // SUBLANES, L // LANES, B * SUBLANES, LANES), F32),
            jax.ShapeDtypeStruct((B, L, POOL_WIDTH), BF16),
        ],
        compiler_params=_cparams(("arbitrary", "arbitrary")),
        name="in_proj",
    )(xa, xb, xa, xb, xa, xb, g, wa, wk_t, wh, cw)


def _log_sigmoid(x):
    return jnp.minimum(x, 0.0) - jnp.log(1.0 + jnp.exp(-jnp.abs(x)))


def _retention_kernel(dec_ref, cos_ref, sin_ref, cost_ref, sint_ref, gn_ref, q_ref, kt_ref, v_ref, g_ref, o_ref,
                      qr_s, kr_s, kv_s, st_s, tab_s):
    L = q_ref.shape[1]
    n_chunks = L // CHUNK
    C = CHUNK
    scale = HEAD_DIM ** -0.5

    lg = _log_sigmoid(dec_ref[0])
    lgf, lgb = lg[0:1], lg[1:2]
    ri = lax.broadcasted_iota(jnp.int32, (C, C), 0)
    ci = lax.broadcasted_iota(jnp.int32, (C, C), 1)
    rf = ri.astype(F32)
    cf = ci.astype(F32)
    dist = jnp.abs(ri - ci).astype(F32)
    tab_s[0] = jnp.exp(dist * jnp.where(ri >= ci, lgf, lgb)) * scale
    tab_s[1] = jnp.exp((rf + 1.0) * lgf) * scale
    tab_s[2] = jnp.exp((C - rf) * lgb) * scale
    tab_s[3] = jnp.exp((C - 1.0 - cf) * lgf)
    tab_s[4] = jnp.exp(cf * lgb)
    dec_f = jnp.exp(C * lgf)
    dec_b = jnp.exp(C * lgb)

    qf = q_ref[0].astype(F32)
    qr_s[...] = qf * cos_ref[...] + pltpu.roll(qf, HEAD_DIM // 2, axis=1) * sin_ref[...]
    kf = kt_ref[0].astype(F32)
    kr_s[...] = kf * cost_ref[...] + pltpu.roll(kf, HEAD_DIM // 2, axis=0) * sint_ref[...]

    def summaries(n, carry):
        c0 = pl.multiple_of(n * C, C)
        kt = kr_s[:, pl.ds(c0, C)]
        lhs = jnp.concatenate([kt * tab_s[3], kt * tab_s[4]], axis=0).astype(BF16)
        kv_s[n] = jnp.dot(lhs, v_ref[0, pl.ds(c0, C), :], preferred_element_type=F32)
        return carry

    lax.fori_loop(0, n_chunks, summaries, 0, unroll=RET_UNROLL)

    def scan(i, carry):
        sf, sb = carry
        m = n_chunks - 1 - i
        st_s[i, :C, :] = sf.astype(BF16)
        st_s[m, C:, :] = sb.astype(BF16)
        return sf * dec_f + kv_s[i, :C, :], sb * dec_b + kv_s[m, C:, :]

    zero = jnp.zeros((C, C), F32)
    lax.fori_loop(0, n_chunks, scan, (zero, zero))

    gain = gn_ref[...]

    def outputs(n, carry):
        c0 = pl.multiple_of(n * C, C)
        qn = qr_s[pl.ds(c0, C), :]
        vn = v_ref[0, pl.ds(c0, C), :]
        sc = jnp.dot(qn.astype(BF16), kr_s[:, pl.ds(c0, C)].astype(BF16), preferred_element_type=F32)
        lhs = jnp.concatenate([sc * tab_s[0], qn * tab_s[1], qn * tab_s[2]], axis=1).astype(BF16)
        rhs = jnp.concatenate([vn, st_s[n]], axis=0)
        o = jnp.dot(lhs, rhs, preferred_element_type=F32)
        mu = jnp.mean(o, axis=-1, keepdims=True)
        oc = o - mu
        var = jnp.mean(oc * oc, axis=-1, keepdims=True)
        gate = g_ref[0, pl.ds(c0, C), :].astype(F32)
        y = oc * lax.rsqrt(var + EPS) * gain * (gate * jax.nn.sigmoid(gate))
        o_ref[0, pl.ds(c0, C), :] = y.astype(BF16)
        return carry

    lax.fori_loop(0, n_chunks, outputs, 0, unroll=RET_UNROLL)


def _retention(qvg, kt, dec, cos, sin, cos_t, sin_t, gn):
    B, L, _ = qvg.shape
    H = RET_HEADS

    def col(off):
        return pl.BlockSpec((1, L, HEAD_DIM), lambda b, h: (b, 0, off + h))

    table = pl.BlockSpec((L, HEAD_DIM), lambda b, h: (0, 0))
    table_t = pl.BlockSpec((HEAD_DIM, L), lambda b, h: (0, 0))
    return pl.pallas_call(
        _retention_kernel,
        grid=(B, H),
        in_specs=[
            pl.BlockSpec((1, 2, LANES), lambda b, h: (h, 0, 0)),
            table, table, table_t, table_t,
            pl.BlockSpec((1, HEAD_DIM), lambda b, h: (0, h)),
            col(0),
            pl.BlockSpec((1, HEAD_DIM, L), lambda b, h: (b, h, 0)),
            col(H), col(2 * H),
        ],
        out_specs=pl.BlockSpec((1, L, HEAD_DIM), lambda b, h: (b, 0, h)),
        out_shape=jax.ShapeDtypeStruct((B, L, RET_WIDTH), BF16),
        scratch_shapes=[
            pltpu.VMEM((L, HEAD_DIM), F32),
            pltpu.VMEM((HEAD_DIM, L), F32),
            pltpu.VMEM((L // CHUNK, 2 * CHUNK, CHUNK), F32),
            pltpu.VMEM((L // CHUNK, 2 * CHUNK, CHUNK), BF16),
            pltpu.VMEM((5, CHUNK, CHUNK), F32),
        ],
        compiler_params=_cparams(("parallel", "parallel")),
        name="retention",
    )(dec, cos, sin, cos_t, sin_t, gn, qvg, kt, qvg, qvg)


def _filter_kernel(w1_ref, b1_ref, fr_ref, w2_ref, b2_ref, w3_ref, kk_ref):
    L = kk_ref.shape[1] // 2
    hi = lax.Precision.HIGHEST
    r = lax.broadcasted_iota(jnp.int32, (HY_HIDDEN, L), 0)
    lane = lax.broadcasted_iota(jnp.int32, (HY_HIDDEN, L), 1)
    band_idx = jnp.where(r <= HY_EMB_BANDS, r - 1, r - 1 - HY_EMB_BANDS).astype(F32)
    band = 1e-4 + band_idx * ((HY_EMB_BANDS - 1 - 1e-4) / (HY_EMB_BANDS - 1))
    crow = lax.broadcasted_iota(jnp.int32, (HY_WIDTH, L), 0).astype(F32)
    delta = jnp.abs(HY_MIN_DECAY + crow * ((HY_MAX_DECAY - HY_MIN_DECAY) / (HY_WIDTH - 1)))
    clane = lax.broadcasted_iota(jnp.int32, (HY_WIDTH, L), 1)
    fr = fr_ref[...]

    def taps(s_int, s_row_int, w3):
        s = s_int.astype(F32)
        t = s / (L - 1.0)
        ang = (2.0 * math.pi / L) * s * band
        feats = jnp.where(r == 0, t,
                          jnp.where(r <= HY_EMB_BANDS, jnp.cos(ang),
                                    jnp.where(r <= 2 * HY_EMB_BANDS, -jnp.sin(ang), 0.0)))
        h = jnp.sin(fr * (jnp.dot(w1_ref[...], feats, precision=hi, preferred_element_type=F32) + b1_ref[...]))
        h = jnp.sin(fr * (jnp.dot(w2_ref[...], h, precision=hi, preferred_element_type=F32) + b2_ref[...]))
        h = jnp.dot(w3, h, precision=hi, preferred_element_type=F32)
        return h * jnp.exp(-(s_row_int.astype(F32) / (L - 1.0)) * delta)

    h_f = taps(lane, clane, w3_ref[0, 0])
    h_b = taps(L - lane, L - clane, w3_ref[0, 1])
    h_b = jnp.where(clane == 0, 0.0, h_b)
    norm = jnp.sum(jnp.abs(h_f), axis=-1, keepdims=True) + jnp.sum(jnp.abs(h_b), axis=-1, keepdims=True)
    inv = 1.0 / norm
    kk = jnp.concatenate([h_b * inv, h_f * inv], axis=1)
    bits = pltpu.bitcast(kk, jnp.uint32)
    bf = (bits + jnp.uint32(0x7FFF) + ((bits >> 16) & jnp.uint32(1))) >> 16
    lane2 = lax.broadcasted_iota(jnp.int32, bf.shape, 1)
    prev = jnp.where(lane2 == 0, jnp.uint32(0), pltpu.roll(bf, 1, axis=1))
    kk_ref[...] = bf | (prev << 16)


def _hyena_filters(w1t, b1, fr, w2t, b2, w3t, L):
    small = lambda shape: pl.BlockSpec(shape, lambda o: (0,) * len(shape))
    return pl.pallas_call(
        _filter_kernel,
        grid=(2,),
        in_specs=[
            small((HY_HIDDEN, HY_HIDDEN)), small((HY_HIDDEN, 1)), small((HY_HIDDEN, 1)),
            small((HY_HIDDEN, HY_HIDDEN)), small((HY_HIDDEN, 1)),
            pl.BlockSpec((1, 2, HY_WIDTH, HY_HIDDEN), lambda o: (o, 0, 0, 0)),
        ],
        out_specs=pl.BlockSpec((HY_WIDTH, 2 * L), lambda o: (o, 0)),
        out_shape=jax.ShapeDtypeStruct((2 * HY_WIDTH, 2 * L), jnp.uint32),
        compiler_params=_cparams(("parallel",)),
        name="hyena_filters",
    )(w1t, b1, fr, w2t, b2, w3t)


def _hyena_kernel(kk1_ref, kk2_ref, xv_ref, x1_ref, x2_ref, bias_ref, out_ref,
                  toep1_s, toep2_s, u2_s, x1u2_s, y2_s, *, B, BP):
    n_tiles = xv_ref.shape[0]
    L = n_tiles * LANES
    T = HY_BLOCK
    NB = L // T
    TPB = T // LANES

    @pl.when(pl.program_id(0) == 0)
    def _():
        toep1_s[...] = jnp.zeros_like(toep1_s)
        toep2_s[...] = jnp.zeros_like(toep2_s)
        u2_s[...] = jnp.zeros_like(u2_s)
        x1u2_s[...] = jnp.zeros_like(x1u2_s)

    def build_toeplitz(kk_ref, c, toep_s):
        row = jnp.broadcast_to(kk_ref[pl.ds(c, 1), :], (SUBLANES, 2 * L))
        for g in range(LANES // BF16_ROWS):
            rolled = pltpu.roll(row, BF16_ROWS * g, axis=1, stride=2, stride_axis=0)
            tile = pltpu.bitcast(rolled, BF16)
            toep_s[BF16_ROWS * g:BF16_ROWS * (g + 1), :] = tile
            toep_s[LANES + BF16_ROWS * g:LANES + BF16_ROWS * (g + 1), LANES:] = tile[:, :2 * L - LANES]

    def long_conv(toep_s, u_s, y_s):
        y_s[...] = jnp.zeros_like(y_s)
        for d in range(-(NB - 1), NB):
            n = NB - abs(d)
            src = max(0, -d) * BP
            dst = max(0, d) * BP
            m = toep_s[:, L + d * T:L + (d + 1) * T]
            lhs = u_s[src:src + n * BP, :].astype(BF16)
            y_s[dst:dst + n * BP, :] += jnp.dot(lhs, m, preferred_element_type=F32)

    def channel(c, k):
        u_s, x1u_s, y_s = u2_s.at[k], x1u2_s.at[k], y2_s.at[k]
        rows = pl.ds(c, B, stride=SUBLANES)
        for j in range(n_tiles):
            J, l0 = j // TPB, (j % TPB) * LANES
            u_s[J * BP:J * BP + B, l0:l0 + LANES] = xv_ref.at[j][rows, :]
            x1u_s[J * BP:J * BP + B, l0:l0 + LANES] = x1_ref.at[j][rows, :]
        b0 = bias_ref[0, 0, pl.ds(c, 1), :]
        b1 = bias_ref[0, 1, pl.ds(c, 1), :]
        build_toeplitz(kk2_ref, c, toep2_s)
        long_conv(toep1_s, u_s, y_s)
        u_s[...] = x1u_s[...] * (y_s[...] + u_s[...] * b0)
        build_toeplitz(kk1_ref, jnp.minimum(c + 1, SUBLANES - 1), toep1_s)
        long_conv(toep2_s, u_s, y_s)
        for j in range(n_tiles):
            J, l0 = j // TPB, (j % TPB) * LANES
            yy = y_s[J * BP:J * BP + B, l0:l0 + LANES] + u_s[J * BP:J * BP + B, l0:l0 + LANES] * b1[:, :LANES]
            out_ref.at[j][rows, :] = x2_ref.at[j][rows, :] * yy

    def channel_pair(i, carry):
        channel(2 * i, 0)
        channel(2 * i + 1, 1)
        return carry

    build_toeplitz(kk1_ref, 0, toep1_s)
    lax.fori_loop(0, SUBLANES // 2, channel_pair, 0, unroll=HY_PAIR_UNROLL)


def _hyena(kk, hy_t, bias, B):
    n_in, n_tiles, R, _ = hy_t.shape
    L = n_tiles * LANES
    NB = L // HY_BLOCK
    BP = -(-B // SUBLANES) * SUBLANES
    n_c = HY_WIDTH // SUBLANES

    def sect(s):
        return pl.BlockSpec((None, n_tiles, R, LANES), lambda c: (s * n_c + c, 0, 0, 0))

    return pl.pallas_call(
        functools.partial(_hyena_kernel, B=B, BP=BP),
        grid=(n_c,),
        in_specs=[
            pl.BlockSpec((SUBLANES, 2 * L), lambda c: (c, 0)),
            pl.BlockSpec((SUBLANES, 2 * L), lambda c: (n_c + c, 0)),
            sect(0), sect(1), sect(2),
            pl.BlockSpec((1, 2, SUBLANES, HY_BLOCK), lambda c: (c, 0, 0, 0)),
        ],
        out_specs=pl.BlockSpec((None, n_tiles, R, LANES), lambda c: (c, 0, 0, 0)),
        out_shape=jax.ShapeDtypeStruct((n_c, n_tiles, R, LANES), F32),
        scratch_shapes=[
            pltpu.VMEM((2 * LANES, 2 * L), BF16), pltpu.VMEM((2 * LANES, 2 * L), BF16),
            pltpu.VMEM((2, NB * BP, HY_BLOCK), F32),
            pltpu.VMEM((2, NB * BP, HY_BLOCK), F32),
            pltpu.VMEM((2, NB * BP, HY_BLOCK), F32),
        ],
        compiler_params=_cparams(("arbitrary",)),
        name="hyena",
    )(kk, kk, hy_t, hy_t, hy_t, bias)


def _pool_kernel(u_ref, w_ref, sc_ref, o_ref):
    L = u_ref.shape[1]
    u = u_ref[0].astype(F32)
    t = lax.broadcasted_iota(jnp.int32, (L, POOL_WIDTH), 0)
    lane = lax.broadcasted_iota(jnp.int32, (L, POOL_WIDTH), 1)
    pad = jnp.zeros((POOL_PAD, POOL_WIDTH), F32)
    up = jnp.concatenate([pad, u, pad], axis=0)
    n = L + 2 * POOL_PAD

    def both(x, k):
        return pltpu.roll(x, k, axis=0) + pltpu.roll(x, n - k, axis=0)

    q2 = pltpu.roll(up, 1, axis=0) + up
    q4 = both(q2, 1)
    q8 = both(q4, 2)
    q16 = both(q8, 4)
    p2, p4, p8, p16 = (q[POOL_PAD:POOL_PAD + L] for q in (q2, q4, q8, q16))
    g = lane // POOL_GROUP_DIM
    half = jnp.where(g == 0, 1, jnp.where(g == 1, 2, jnp.where(g == 2, 4, 8)))
    total = jnp.where(g == 0, p2, jnp.where(g == 1, p4, jnp.where(g == 2, p8, p16)))
    lo = jnp.maximum(t - half, 0)
    hi = jnp.minimum(t + half - 1, L - 1)
    cnt = (hi - lo + 1).astype(F32)
    d = (total / cnt - u).astype(BF16)
    o_ref[0] = (jnp.dot(d, w_ref[...], preferred_element_type=F32) * sc_ref[...]).astype(BF16)


def _pool(pool_in, w_bd, scale):
    B, L, _ = pool_in.shape
    return pl.pallas_call(
        _pool_kernel,
        grid=(B,),
        in_specs=[
            pl.BlockSpec((1, L, POOL_WIDTH), lambda b: (b, 0, 0)),
            pl.BlockSpec((POOL_WIDTH, POOL_WIDTH), lambda b: (0, 0)),
            pl.BlockSpec((1, POOL_WIDTH), lambda b: (0, 0)),
        ],
        out_specs=pl.BlockSpec((1, L, POOL_WIDTH), lambda b: (b, 0, 0)),
        out_shape=jax.ShapeDtypeStruct((B, L, POOL_WIDTH), BF16),
        compiler_params=_cparams(("parallel",)),
        name="pool_mixer",
    )(pool_in, w_bd, scale)


def _mix_out_kernel(xa_ref, xb_ref, ret_ref, hy_ref, pool_ref, wr_ref, wh_ref, wp_ref, g_ref, x1_ref, h2_ref, *, nbp):
    tm = xa_ref.shape[1]
    hy_t = jnp.concatenate([hy_ref[:, j].reshape(HY_WIDTH, LANES) for j in range(tm // LANES)],
                           axis=1).astype(BF16)
    mix = jnp.dot(ret_ref[0], wr_ref[...], preferred_element_type=F32)
    mix += lax.dot_general(hy_t, wh_ref[...], (((0,), (0,)), ((), ())), preferred_element_type=F32)
    mix += jnp.dot(pool_ref[0], wp_ref[...], preferred_element_type=F32)
    x1 = _pick(nbp, xa_ref, xb_ref) + mix
    x1_ref[0] = x1
    h2_ref[0] = _rms(x1, g_ref[...]).astype(BF16)


def _mix_out(x, ret, hy_o, pool_o, wr, wh, wp, g, tm):
    xa, xb, nbp, off = x
    B, L, _ = ret.shape
    nt = L // tm
    return pl.pallas_call(
        functools.partial(_mix_out_kernel, nbp=nbp),
        grid=(B, nt),
        in_specs=[
            *_pair_specs(tm, D_MODEL, nt, nbp, off),
            pl.BlockSpec((1, tm, RET_WIDTH), lambda b, i: (b, i, 0)),
            pl.BlockSpec((HY_WIDTH // SUBLANES, tm // LANES, SUBLANES, LANES), lambda b, i: (0, i, b, 0)),
            pl.BlockSpec((1, tm, POOL_WIDTH), lambda b, i: (b, i, 0)),
            pl.BlockSpec((RET_WIDTH, D_MODEL), lambda b, i: (0, 0)),
            pl.BlockSpec((HY_WIDTH, D_MODEL), lambda b, i: (0, 0)),
            pl.BlockSpec((POOL_WIDTH, D_MODEL), lambda b, i: (0, 0)),
            pl.BlockSpec((1, D_MODEL), lambda b, i: (0, 0)),
        ],
        out_specs=[
            pl.BlockSpec((1, tm, D_MODEL), lambda b, i: (b, i, 0)),
            pl.BlockSpec((1, tm, D_MODEL), lambda b, i: (b, i, 0)),
        ],
        out_shape=[
            jax.ShapeDtypeStruct((B, L, D_MODEL), F32),
            jax.ShapeDtypeStruct((B, L, D_MODEL), BF16),
        ],
        compiler_params=_cparams(("arbitrary", "arbitrary")),
        name="mix_out",
    )(xa, xb, ret, hy_o, pool_o, wr, wh, wp, g)


def _ffn_kernel(h_ref, hp_ref, hn_ref, x1_ref, wup_ref, cw_ref, wd_ref, pa_ref, pb_ref, pw_ref, pg_ref,
                pn_ref, fin_ref, *o_refs, nbp, final):
    tm = h_ref.shape[1]
    i = pl.program_id(1)
    nt = pl.num_programs(1)

    hx = jnp.concatenate([h_ref[0], hp_ref[0], hn_ref[0]], axis=0)
    row = lax.broadcasted_iota(jnp.int32, (tm, FFN_SUB), 0)
    n_sub = D_FF // FFN_SUB

    def up_proj(s):
        c0 = s * FFN_SUB
        gate = jnp.dot(hx, wup_ref[:, c0:c0 + FFN_SUB], preferred_element_type=F32)
        up = jnp.dot(hx[:tm], wup_ref[:, D_FF + c0:D_FF + c0 + FFN_SUB], preferred_element_type=F32)
        return gate, up

    acc = x1_ref[0]
    nxt = up_proj(0)
    group, g0 = [], 0
    for s in range(n_sub):
        c0 = s * FFN_SUB
        gx, up = nxt
        if s + 1 < n_sub:
            nxt = up_proj(s + 1)
        gate = gx[:tm]
        g_prev = jnp.where(i == 0, 0.0, gx[tm + BF16_ROWS - 1:tm + BF16_ROWS])
        g_next = jnp.where(i == nt - 1, 0.0, gx[tm + BF16_ROWS:tm + BF16_ROWS + 1])
        g_up = jnp.where(row == 0, g_prev, pltpu.roll(gate, 1, axis=0))
        g_dn = jnp.where(row == tm - 1, g_next, pltpu.roll(gate, tm - 1, axis=0))
        cw = cw_ref[:, c0:c0 + FFN_SUB]
        gc = g_up * cw[0:1] + gate * cw[1:2] + g_dn * cw[2:3]
        group.append((jax.nn.gelu(gc) * up).astype(BF16))
        if len(group) == FFN_DOWN_GROUP or s == n_sub - 1:
            a = jnp.concatenate(group, axis=1)
            acc = acc + jnp.dot(a, wd_ref[g0:c0 + FFN_SUB, :], preferred_element_type=F32)
            group, g0 = [], c0 + FFN_SUB

    x2 = acc
    p = _pick(nbp, pa_ref, pb_ref).astype(BF16)
    e = _rms(jnp.dot(p, pw_ref[...], preferred_element_type=F32), pn_ref[...])
    gt = jax.nn.sigmoid(jnp.dot(x2.astype(BF16), pg_ref[...], preferred_element_type=F32))
    x3 = x2 + gt * e
    if not final:
        o_refs[0][0] = x3
        return
    y = _rms(x3, fin_ref[...])
    b = pl.program_id(0)

    @pl.when(b < nbp)
    def _():
        o_refs[0][0] = y

    @pl.when(b >= nbp)
    def _():
        o_refs[1][0] = y


def _ffn(h2, x1, w_up, cw, w_down, p, layer, ple_w, ple_gate, ple_norm, fin, tm, final):
    pa, pb, nbp, off = p
    B, L, _ = x1.shape
    nt = L // tm
    halo = BF16_ROWS
    nh = L // halo
    per = tm // halo
    full = lambda shape: pl.BlockSpec(shape, lambda b, i: (0,) * len(shape), pipeline_mode=pl.Buffered(1))
    tile = pl.BlockSpec((1, tm, D_MODEL), lambda b, i: (b, i, 0))
    if final:
        out_specs = list(_pair_specs(tm, D_MODEL, nt, nbp, 0))
        out_shape = [jax.ShapeDtypeStruct((nbp, L, D_MODEL), F32), jax.ShapeDtypeStruct((B - nbp, L, D_MODEL), F32)]
    else:
        out_specs = [tile]
        out_shape = [jax.ShapeDtypeStruct((B, L, D_MODEL), F32)]
    return pl.pallas_call(
        functools.partial(_ffn_kernel, nbp=nbp, final=final),
        grid=(B, nt),
        in_specs=[
            tile,
            pl.BlockSpec((1, halo, D_MODEL), lambda b, i: (b, jnp.maximum(i * per - 1, 0), 0)),
            pl.BlockSpec((1, halo, D_MODEL), lambda b, i: (b, jnp.minimum((i + 1) * per, nh - 1), 0)),
            tile,
            full((D_MODEL, 2 * D_FF)), full((3, D_FF)), full((D_FF, D_MODEL)),
            *_pair_specs(tm, PLE_DIM, nt, nbp, off, lead=layer),
            full((PLE_DIM, D_MODEL)), full((D_MODEL, D_MODEL)), full((1, D_MODEL)), full((1, D_MODEL)),
        ],
        out_specs=out_specs,
        out_shape=out_shape,
        compiler_params=_cparams(("arbitrary", "arbitrary")),
        name="ffn_ple",
    )(h2, h2, h2, x1, w_up, cw, w_down, pa, pb, ple_w, ple_gate, ple_norm, fin)


def _rope_tables(L):
    half = HEAD_DIM // 2
    inv = ROPE_THETA ** (-jnp.arange(half, dtype=F32) / half)
    ang = jnp.arange(L, dtype=F32)[:, None] * inv[None, :]
    cos, sin = jnp.cos(ang), jnp.sin(ang)
    return jnp.concatenate([cos, cos], axis=-1), jnp.concatenate([-sin, sin], axis=-1)


def _block_diag(pool_w):
    out = jnp.zeros((POOL_WIDTH, POOL_WIDTH), pool_w.dtype)
    for g in range(len(POOL_WINDOWS)):
        sl = slice(g * POOL_GROUP_DIM, (g + 1) * POOL_GROUP_DIM)
        out = out.at[sl, sl].set(pool_w[g])
    return out


def _layer(x, B, p, layer, lw, cos, sin, fin, tm, final):
    (norm_mix, w_in, dec_f, dec_b, ret_gn, hy_conv, hy_w1, hy_b1, hy_freq, hy_w2, hy_b2, hy_w3, hy_bias,
     pool_w, pool_scale, w_out, norm_ffn, w_up, ffn_conv, w_down, ple_w, ple_gate, ple_norm) = lw
    L = x[0].shape[1]
    R = RET_WIDTH

    w_q, w_k, w_v, w_g, w_hy, w_pool = jnp.split(w_in, [R, 2 * R, 3 * R, 4 * R, 4 * R + HY_IN_WIDTH], axis=1)
    wa = jnp.concatenate([w_q, w_v, w_g, w_pool], axis=1).astype(BF16)
    qvg, kt, hy_t, pool_in = _in_proj(x, B, norm_mix[None], wa, w_k.T.astype(BF16), w_hy.astype(BF16), hy_conv, tm)

    dec = jnp.broadcast_to(jnp.stack([dec_f, dec_b], axis=1)[:, :, None], (RET_HEADS, 2, LANES)).astype(F32)
    ret = _retention(qvg, kt, dec, cos, sin, cos.T, sin.T, ret_gn[None])

    w1t = jnp.zeros((HY_HIDDEN, HY_HIDDEN), F32).at[:, :hy_w1.shape[0]].set(hy_w1.T)
    w3t = hy_w3.T.reshape(2, 2, HY_WIDTH, HY_HIDDEN)
    kk = _hyena_filters(w1t, hy_b1[:, None], hy_freq[:, None], hy_w2.T, hy_b2[:, None], w3t, L)
    bias = jnp.broadcast_to(hy_bias.reshape(2, HY_WIDTH // SUBLANES, SUBLANES).transpose(1, 0, 2)[..., None],
                            (HY_WIDTH // SUBLANES, 2, SUBLANES, HY_BLOCK))
    hy_o = _hyena(kk, hy_t, bias, B)

    pool_o = _pool(pool_in, _block_diag(pool_w).astype(BF16), pool_scale[None])

    wo = w_out.astype(BF16)
    x1, h2 = _mix_out(x, ret, hy_o, pool_o, wo[:R], wo[R:R + HY_WIDTH], wo[R + HY_WIDTH:], norm_ffn[None], tm)

    return _ffn(h2, x1, w_up.astype(BF16), ffn_conv, w_down.astype(BF16), p, layer, ple_w.astype(BF16),
                ple_gate.astype(BF16), ple_norm[None], fin, tm, final)


def _trunk(x_a, x_b, p_a, p_b, layer_weights, norm_final, tm):
    depth = p_a.shape[0]
    nbp, L = x_a.shape[0], x_a.shape[1]
    B = nbp + x_b.shape[0]
    cos, sin = _rope_tables(L)
    fin = norm_final[None]
    x = (x_a, x_b, nbp, 0)
    p = (p_a, p_b, nbp, 0)
    for i in range(depth):
        out = _layer(x, B, p, i, [w[i] for w in layer_weights], cos, sin, fin, tm, i == depth - 1)
        x = (out[0], out[0], nbp, nbp)
    return out[0], out[1]


def kernel(x_prompt, x_sample, p_prompt, p_sample, norm_mix, w_in, ret_decay_fwd, ret_decay_bwd, ret_gn,
           hy_short_conv, hy_w1, hy_b1, hy_freq, hy_w2, hy_b2, hy_w3, hy_bias, pool_w, pool_scale, w_out,
           norm_ffn, ffn_w_up, ffn_conv, ffn_w_down, ple_w, ple_gate_w, ple_norm, norm_final):
    layer_weights = (norm_mix, w_in, ret_decay_fwd, ret_decay_bwd, ret_gn, hy_short_conv, hy_w1, hy_b1,
                     hy_freq, hy_w2, hy_b2, hy_w3, hy_bias, pool_w, pool_scale, w_out, norm_ffn, ffn_w_up,
                     ffn_conv, ffn_w_down, ple_w, ple_gate_w, ple_norm)
    return _trunk(x_prompt, x_sample, p_prompt, p_sample, layer_weights, norm_final, ROW_TILE)
```

```python
import functools
import math

import jax
import jax.numpy as jnp
from jax import lax
from jax.experimental import pallas as pl
from jax.experimental.pallas import tpu as pltpu

F32 = jnp.float32
BF16 = jnp.bfloat16

D_MODEL = 1024
RET_WIDTH = 512
RET_HEADS = 4
HEAD_DIM = 128
HY_WIDTH = 256
POOL_WIDTH = 256
POOL_GROUP_DIM = 64
POOL_WINDOWS = (2, 4, 8, 16)
POOL_PAD = 16
D_FF = 2816
PLE_DIM = 256
CHUNK = 128
ROPE_THETA = 10000.0
HY_EMB_BANDS = 16
HY_HIDDEN = 64
HY_MIN_DECAY = math.log(1e-2) / 1.5
HY_MAX_DECAY = math.log(1e-2) / 0.3
EPS = 1e-6

LANES = 128
SUBLANES = 8
BF16_ROWS = 16
MXU_DIM = 256
VMEM_LIMIT = 56 * 1024 * 1024

HY_BLOCK = MXU_DIM
HY_PAIR_UNROLL = 1
QVG_WIDTH = 3 * RET_WIDTH
RET_UNROLL = True
HY_IN_WIDTH = 3 * HY_WIDTH
ROW_TILE = 512
FFN_SUB = MXU_DIM
FFN_DOWN_GROUP = 4


def _cparams(sem):
    return pltpu.CompilerParams(dimension_semantics=sem, vmem_limit_bytes=VMEM_LIMIT)


def _rms(x, g):
    return x * lax.rsqrt(jnp.mean(x * x, axis=-1, keepdims=True) + EPS) * g


def _pair_specs(tm, width, nt, nbp, off, lead=None):
    pre = () if lead is None else (lead,)
    shape = (1, tm, width) if lead is None else (None, 1, tm, width)

    def first(b, i):
        return pre + (jnp.minimum(b, nbp - 1), jnp.where(b < nbp, i, nt - 1), 0)

    def second(b, i):
        return pre + (jnp.maximum(b - nbp, 0) + off, jnp.where(b < nbp, 0, i), 0)

    return pl.BlockSpec(shape, first), pl.BlockSpec(shape, second)


def _pick(nbp, a_ref, b_ref):
    return jnp.where(pl.program_id(0) < nbp, a_ref[0], b_ref[0])


def _pair_halo_specs(rows, width, tm, L, nbp, off):
    per, last = tm // rows, L // rows - 1

    def make(first, after):
        def index(b, i):
            r = jnp.minimum((i + 1) * per, last) if after else jnp.maximum(i * per - 1, 0)
            if first:
                return (jnp.minimum(b, nbp - 1), jnp.where(b < nbp, r, 0), 0)
            return (jnp.maximum(b - nbp, 0) + off, jnp.where(b < nbp, 0, r), 0)
        return pl.BlockSpec((1, rows, width), index)

    return make(True, False), make(False, False), make(True, True), make(False, True)


def _dwconv3_rows(u, tm, cw, first_tile, last_tile):
    main = u[:tm]
    prev = jnp.where(first_tile, 0.0, u[tm + BF16_ROWS - 1:tm + BF16_ROWS])
    nxt = jnp.where(last_tile, 0.0, u[tm + BF16_ROWS:tm + BF16_ROWS + 1])
    row = lax.broadcasted_iota(jnp.int32, main.shape, 0)
    up = jnp.where(row == 0, prev, pltpu.roll(main, 1, axis=0))
    dn = jnp.where(row == tm - 1, nxt, pltpu.roll(main, tm - 1, axis=0))
    return up * cw[0:1] + main * cw[1:2] + dn * cw[2:3]


def _in_proj_kernel(xa_ref, xb_ref, xpa_ref, xpb_ref, xna_ref, xnb_ref, g_ref, wa_ref, wk_ref, wh_ref, cw_ref,
                    qvg_ref, kt_ref, hy_ref, pool_ref, *, nbp):
    tm = xa_ref.shape[1]
    i = pl.program_id(1)
    g = g_ref[...]
    h = _rms(_pick(nbp, xa_ref, xb_ref), g).astype(BF16)
    hx = jnp.concatenate([h, _rms(_pick(nbp, xpa_ref, xpb_ref), g).astype(BF16),
                          _rms(_pick(nbp, xna_ref, xnb_ref), g).astype(BF16)], axis=0)
    u = jnp.dot(hx, wh_ref[...], preferred_element_type=F32)
    y = _dwconv3_rows(u, tm, cw_ref[...], i == 0, i == pl.num_programs(1) - 1)
    yt = y.T
    for j in range(tm // LANES):
        hy_ref[:, j, :, :] = yt[:, j * LANES:(j + 1) * LANES].reshape(HY_IN_WIDTH // SUBLANES, SUBLANES, LANES)
    kt = lax.dot_general(wk_ref[...], h, (((1,), (1,)), ((), ())), preferred_element_type=F32)
    kt_ref[0] = kt.astype(BF16)
    a = jnp.dot(h, wa_ref[...], preferred_element_type=F32).astype(BF16)
    qvg_ref[0] = a[:, :QVG_WIDTH]
    pool_ref[0] = a[:, QVG_WIDTH:]


def _in_proj(x, B, g, wa, wk_t, wh, cw, tm):
    xa, xb, nbp, off = x
    L = xa.shape[1]
    nt = L // tm
    const = lambda shape: pl.BlockSpec(shape, lambda b, i: (0,) * len(shape))
    return pl.pallas_call(
        functools.partial(_in_proj_kernel, nbp=nbp),
        grid=(B, nt),
        in_specs=[
            *_pair_specs(tm, D_MODEL, nt, nbp, off),
            *_pair_halo_specs(BF16_ROWS, D_MODEL, tm, L, nbp, off),
            const((1, D_MODEL)),
            const((D_MODEL, QVG_WIDTH + POOL_WIDTH)),
            const((RET_WIDTH, D_MODEL)),
            const((D_MODEL, HY_IN_WIDTH)),
            const((3, HY_IN_WIDTH)),
        ],
        out_specs=[
            pl.BlockSpec((1, tm, QVG_WIDTH), lambda b, i: (b, i, 0)),
            pl.BlockSpec((1, RET_WIDTH, tm), lambda b, i: (b, 0, i)),
            pl.BlockSpec((HY_IN_WIDTH // SUBLANES, tm // LANES, SUBLANES, LANES), lambda b, i: (0, i, b, 0)),
            pl.BlockSpec((1, tm, POOL_WIDTH), lambda b, i: (b, i, 0)),
        ],
        out_shape=[
            jax.ShapeDtypeStruct((B, L, QVG_WIDTH), BF16),
            jax.ShapeDtypeStruct((B, RET_WIDTH, L), BF16),
            jax.ShapeDtypeStruct((HY_IN_WIDTH // SUBLANES, L // LANES, B * SUBLANES, LANES), F32),
            jax.ShapeDtypeStruct((B, L, POOL_WIDTH), BF16),
        ],
        compiler_params=_cparams(("arbitrary", "arbitrary")),
        name="in_proj",
    )(xa, xb, xa, xb, xa, xb, g, wa, wk_t, wh, cw)


def _log_sigmoid(x):
    return jnp.minimum(x, 0.0) - jnp.log(1.0 + jnp.exp(-jnp.abs(x)))


def _retention_kernel(dec_ref, cos_ref, sin_ref, cost_ref, sint_ref, gn_ref, q_ref, kt_ref, v_ref, g_ref, o_ref,
                      qr_s, kr_s, kv_s, st_s, tab_s):
    L = q_ref.shape[1]
    n_chunks = L // CHUNK
    C = CHUNK
    scale = HEAD_DIM ** -0.5

    lg = _log_sigmoid(dec_ref[0])
    lgf, lgb = lg[0:1], lg[1:2]
    ri = lax.broadcasted_iota(jnp.int32, (C, C), 0)
    ci = lax.broadcasted_iota(jnp.int32, (C, C), 1)
    rf = ri.astype(F32)
    cf = ci.astype(F32)
    dist = jnp.abs(ri - ci).astype(F32)
    tab_s[0] = jnp.exp(dist * jnp.where(ri >= ci, lgf, lgb)) * scale
    tab_s[1] = jnp.exp((rf + 1.0) * lgf) * scale
    tab_s[2] = jnp.exp((C - rf) * lgb) * scale
    tab_s[3] = jnp.exp((C - 1.0 - cf) * lgf)
    tab_s[4] = jnp.exp(cf * lgb)
    dec_f = jnp.exp(C * lgf)
    dec_b = jnp.exp(C * lgb)

    qf = q_ref[0].astype(F32)
    qr_s[...] = qf * cos_ref[...] + pltpu.roll(qf, HEAD_DIM // 2, axis=1) * sin_ref[...]
    kf = kt_ref[0].astype(F32)
    kr_s[...] = kf * cost_ref[...] + pltpu.roll(kf, HEAD_DIM // 2, axis=0) * sint_ref[...]

    def summaries(n, carry):
        c0 = pl.multiple_of(n * C, C)
        kt = kr_s[:, pl.ds(c0, C)]
        lhs = jnp.concatenate([kt * tab_s[3], kt * tab_s[4]], axis=0).astype(BF16)
        kv_s[n] = jnp.dot(lhs, v_ref[0, pl.ds(c0, C), :], preferred_element_type=F32)
        return carry

    lax.fori_loop(0, n_chunks, summaries, 0, unroll=RET_UNROLL)

    def scan(i, carry):
        sf, sb = carry
        m = n_chunks - 1 - i
        st_s[i, :C, :] = sf.astype(BF16)
        st_s[m, C:, :] = sb.astype(BF16)
        return sf * dec_f + kv_s[i, :C, :], sb * dec_b + kv_s[m, C:, :]

    zero = jnp.zeros((C, C), F32)
    lax.fori_loop(0, n_chunks, scan, (zero, zero), unroll=RET_UNROLL)

    gain = gn_ref[...]

    def outputs(n, carry):
        c0 = pl.multiple_of(n * C, C)
        qn = qr_s[pl.ds(c0, C), :]
        vn = v_ref[0, pl.ds(c0, C), :]
        sc = jnp.dot(qn.astype(BF16), kr_s[:, pl.ds(c0, C)].astype(BF16), preferred_element_type=F32)
        lhs = jnp.concatenate([sc * tab_s[0], qn * tab_s[1], qn * tab_s[2]], axis=1).astype(BF16)
        rhs = jnp.concatenate([vn, st_s[n]], axis=0)
        o = jnp.dot(lhs, rhs, preferred_element_type=F32)
        mu = jnp.mean(o, axis=-1, keepdims=True)
        oc = o - mu
        var = jnp.mean(oc * oc, axis=-1, keepdims=True)
        gate = g_ref[0, pl.ds(c0, C), :].astype(F32)
        y = oc * lax.rsqrt(var + EPS) * gain * (gate * jax.nn.sigmoid(gate))
        o_ref[0, pl.ds(c0, C), :] = y.astype(BF16)
        return carry

    lax.fori_loop(0, n_chunks, outputs, 0, unroll=RET_UNROLL)


def _retention(qvg, kt, dec, cos, sin, cos_t, sin_t, gn):
    B, L, _ = qvg.shape
    H = RET_HEADS

    def col(off):
        return pl.BlockSpec((1, L, HEAD_DIM), lambda b, h: (b, 0, off + h))

    table = pl.BlockSpec((L, HEAD_DIM), lambda b, h: (0, 0))
    table_t = pl.BlockSpec((HEAD_DIM, L), lambda b, h: (0, 0))
    return pl.pallas_call(
        _retention_kernel,
        grid=(B, H),
        in_specs=[
            pl.BlockSpec((1, 2, LANES), lambda b, h: (h, 0, 0)),
            table, table, table_t, table_t,
            pl.BlockSpec((1, HEAD_DIM), lambda b, h: (0, h)),
            col(0),
            pl.BlockSpec((1, HEAD_DIM, L), lambda b, h: (b, h, 0)),
            col(H), col(2 * H),
        ],
        out_specs=pl.BlockSpec((1, L, HEAD_DIM), lambda b, h: (b, 0, h)),
        out_shape=jax.ShapeDtypeStruct((B, L, RET_WIDTH), BF16),
        scratch_shapes=[
            pltpu.VMEM((L, HEAD_DIM), F32),
            pltpu.VMEM((HEAD_DIM, L), F32),
            pltpu.VMEM((L // CHUNK, 2 * CHUNK, CHUNK), F32),
            pltpu.VMEM((L // CHUNK, 2 * CHUNK, CHUNK), BF16),
            pltpu.VMEM((5, CHUNK, CHUNK), F32),
        ],
        compiler_params=_cparams(("parallel", "parallel")),
        name="retention",
    )(dec, cos, sin, cos_t, sin_t, gn, qvg, kt, qvg, qvg)


def _filter_kernel(w1_ref, b1_ref, fr_ref, w2_ref, b2_ref, w3_ref, kk_ref):
    L = kk_ref.shape[1] // 2
    hi = lax.Precision.HIGHEST
    r = lax.broadcasted_iota(jnp.int32, (HY_HIDDEN, L), 0)
    lane = lax.broadcasted_iota(jnp.int32, (HY_HIDDEN, L), 1)
    band_idx = jnp.where(r <= HY_EMB_BANDS, r - 1, r - 1 - HY_EMB_BANDS).astype(F32)
    band = 1e-4 + band_idx * ((HY_EMB_BANDS - 1 - 1e-4) / (HY_EMB_BANDS - 1))
    crow = lax.broadcasted_iota(jnp.int32, (HY_WIDTH, L), 0).astype(F32)
    delta = jnp.abs(HY_MIN_DECAY + crow * ((HY_MAX_DECAY - HY_MIN_DECAY) / (HY_WIDTH - 1)))
    clane = lax.broadcasted_iota(jnp.int32, (HY_WIDTH, L), 1)
    fr = fr_ref[...]

    def taps(s_int, s_row_int, w3):
        s = s_int.astype(F32)
        t = s / (L - 1.0)
        ang = (2.0 * math.pi / L) * s * band
        feats = jnp.where(r == 0, t,
                          jnp.where(r <= HY_EMB_BANDS, jnp.cos(ang),
                                    jnp.where(r <= 2 * HY_EMB_BANDS, -jnp.sin(ang), 0.0)))
        h = jnp.sin(fr * (jnp.dot(w1_ref[...], feats, precision=hi, preferred_element_type=F32) + b1_ref[...]))
        h = jnp.sin(fr * (jnp.dot(w2_ref[...], h, precision=hi, preferred_element_type=F32) + b2_ref[...]))
        h = jnp.dot(w3, h, precision=hi, preferred_element_type=F32)
        return h * jnp.exp(-(s_row_int.astype(F32) / (L - 1.0)) * delta)

    h_f = taps(lane, clane, w3_ref[0, 0])
    h_b = taps(L - lane, L - clane, w3_ref[0, 1])
    h_b = jnp.where(clane == 0, 0.0, h_b)
    norm = jnp.sum(jnp.abs(h_f), axis=-1, keepdims=True) + jnp.sum(jnp.abs(h_b), axis=-1, keepdims=True)
    inv = 1.0 / norm
    kk = jnp.concatenate([h_b * inv, h_f * inv], axis=1)
    bits = pltpu.bitcast(kk, jnp.uint32)
    bf = (bits + jnp.uint32(0x7FFF) + ((bits >> 16) & jnp.uint32(1))) >> 16
    lane2 = lax.broadcasted_iota(jnp.int32, bf.shape, 1)
    prev = jnp.where(lane2 == 0, jnp.uint32(0), pltpu.roll(bf, 1, axis=1))
    kk_ref[...] = bf | (prev << 16)


def _hyena_filters(w1t, b1, fr, w2t, b2, w3t, L):
    small = lambda shape: pl.BlockSpec(shape, lambda o: (0,) * len(shape))
    return pl.pallas_call(
        _filter_kernel,
        grid=(2,),
        in_specs=[
            small((HY_HIDDEN, HY_HIDDEN)), small((HY_HIDDEN, 1)), small((HY_HIDDEN, 1)),
            small((HY_HIDDEN, HY_HIDDEN)), small((HY_HIDDEN, 1)),
            pl.BlockSpec((1, 2, HY_WIDTH, HY_HIDDEN), lambda o: (o, 0, 0, 0)),
        ],
        out_specs=pl.BlockSpec((HY_WIDTH, 2 * L), lambda o: (o, 0)),
        out_shape=jax.ShapeDtypeStruct((2 * HY_WIDTH, 2 * L), jnp.uint32),
        compiler_params=_cparams(("parallel",)),
        name="hyena_filters",
    )(w1t, b1, fr, w2t, b2, w3t)


def _hyena_kernel(kk1_ref, kk2_ref, xv_ref, x1_ref, x2_ref, bias_ref, out_ref,
                  toep1_s, toep2_s, u2_s, x1u2_s, y2_s, *, B, BP):
    n_tiles = xv_ref.shape[0]
    L = n_tiles * LANES
    T = HY_BLOCK
    NB = L // T
    TPB = T // LANES

    @pl.when(pl.program_id(0) == 0)
    def _():
        toep1_s[...] = jnp.zeros_like(toep1_s)
        toep2_s[...] = jnp.zeros_like(toep2_s)
        u2_s[...] = jnp.zeros_like(u2_s)
        x1u2_s[...] = jnp.zeros_like(x1u2_s)

    def build_toeplitz(kk_ref, c, toep_s):
        row = jnp.broadcast_to(kk_ref[pl.ds(c, 1), :], (SUBLANES, 2 * L))
        for g in range(LANES // BF16_ROWS):
            rolled = pltpu.roll(row, BF16_ROWS * g, axis=1, stride=2, stride_axis=0)
            tile = pltpu.bitcast(rolled, BF16)
            toep_s[BF16_ROWS * g:BF16_ROWS * (g + 1), :] = tile
            toep_s[LANES + BF16_ROWS * g:LANES + BF16_ROWS * (g + 1), LANES:] = tile[:, :2 * L - LANES]

    def long_conv(toep_s, u_s, y_s):
        y_s[...] = jnp.zeros_like(y_s)
        for d in range(-(NB - 1), NB):
            n = NB - abs(d)
            src = max(0, -d) * BP
            dst = max(0, d) * BP
            m = toep_s[:, L + d * T:L + (d + 1) * T]
            lhs = u_s[src:src + n * BP, :].astype(BF16)
            y_s[dst:dst + n * BP, :] += jnp.dot(lhs, m, preferred_element_type=F32)

    def channel(c, k):
        u_s, x1u_s, y_s = u2_s.at[k], x1u2_s.at[k], y2_s.at[k]
        rows = pl.ds(c, B, stride=SUBLANES)
        for j in range(n_tiles):
            J, l0 = j // TPB, (j % TPB) * LANES
            u_s[J * BP:J * BP + B, l0:l0 + LANES] = xv_ref.at[j][rows, :]
            x1u_s[J * BP:J * BP + B, l0:l0 + LANES] = x1_ref.at[j][rows, :]
        b0 = bias_ref[0, 0, pl.ds(c, 1), :]
        b1 = bias_ref[0, 1, pl.ds(c, 1), :]
        build_toeplitz(kk2_ref, c, toep2_s)
        long_conv(toep1_s, u_s, y_s)
        u_s[...] = x1u_s[...] * (y_s[...] + u_s[...] * b0)
        build_toeplitz(kk1_ref, jnp.minimum(c + 1, SUBLANES - 1), toep1_s)
        long_conv(toep2_s, u_s, y_s)
        for j in range(n_tiles):
            J, l0 = j // TPB, (j % TPB) * LANES
            yy = y_s[J * BP:J * BP + B, l0:l0 + LANES] + u_s[J * BP:J * BP + B, l0:l0 + LANES] * b1[:, :LANES]
            out_ref.at[j][rows, :] = x2_ref.at[j][rows, :] * yy

    def channel_pair(i, carry):
        channel(2 * i, 0)
        channel(2 * i + 1, 1)
        return carry

    build_toeplitz(kk1_ref, 0, toep1_s)
    lax.fori_loop(0, SUBLANES // 2, channel_pair, 0, unroll=HY_PAIR_UNROLL)


def _hyena(kk, hy_t, bias, B):
    n_in, n_tiles, R, _ = hy_t.shape
    L = n_tiles * LANES
    NB = L // HY_BLOCK
    BP = -(-B // SUBLANES) * SUBLANES
    n_c = HY_WIDTH // SUBLANES

    def sect(s):
        return pl.BlockSpec((None, n_tiles, R, LANES), lambda c: (s * n_c + c, 0, 0, 0))

    return pl.pallas_call(
        functools.partial(_hyena_kernel, B=B, BP=BP),
        grid=(n_c,),
        in_specs=[
            pl.BlockSpec((SUBLANES, 2 * L), lambda c: (c, 0)),
            pl.BlockSpec((SUBLANES, 2 * L), lambda c: (n_c + c, 0)),
            sect(0), sect(1), sect(2),
            pl.BlockSpec((1, 2, SUBLANES, HY_BLOCK), lambda c: (c, 0, 0, 0)),
        ],
        out_specs=pl.BlockSpec((None, n_tiles, R, LANES), lambda c: (c, 0, 0, 0)),
        out_shape=jax.ShapeDtypeStruct((n_c, n_tiles, R, LANES), F32),
        scratch_shapes=[
            pltpu.VMEM((2 * LANES, 2 * L), BF16), pltpu.VMEM((2 * LANES, 2 * L), BF16),
            pltpu.VMEM((2, NB * BP, HY_BLOCK), F32),
            pltpu.VMEM((2, NB * BP, HY_BLOCK), F32),
            pltpu.VMEM((2, NB * BP, HY_BLOCK), F32),
        ],
        compiler_params=_cparams(("arbitrary",)),
        name="hyena",
    )(kk, kk, hy_t, hy_t, hy_t, bias)


def _pool_kernel(u_ref, w_ref, sc_ref, o_ref):
    L = u_ref.shape[1]
    u = u_ref[0].astype(F32)
    t = lax.broadcasted_iota(jnp.int32, (L, POOL_WIDTH), 0)
    lane = lax.broadcasted_iota(jnp.int32, (L, POOL_WIDTH), 1)
    pad = jnp.zeros((POOL_PAD, POOL_WIDTH), F32)
    up = jnp.concatenate([pad, u, pad], axis=0)
    n = L + 2 * POOL_PAD

    def both(x, k):
        return pltpu.roll(x, k, axis=0) + pltpu.roll(x, n - k, axis=0)

    q2 = pltpu.roll(up, 1, axis=0) + up
    q4 = both(q2, 1)
    q8 = both(q4, 2)
    q16 = both(q8, 4)
    p2, p4, p8, p16 = (q[POOL_PAD:POOL_PAD + L] for q in (q2, q4, q8, q16))
    g = lane // POOL_GROUP_DIM
    half = jnp.where(g == 0, 1, jnp.where(g == 1, 2, jnp.where(g == 2, 4, 8)))
    total = jnp.where(g == 0, p2, jnp.where(g == 1, p4, jnp.where(g == 2, p8, p16)))
    lo = jnp.maximum(t - half, 0)
    hi = jnp.minimum(t + half - 1, L - 1)
    cnt = (hi - lo + 1).astype(F32)
    d = (total / cnt - u).astype(BF16)
    o_ref[0] = (jnp.dot(d, w_ref[...], preferred_element_type=F32) * sc_ref[...]).astype(BF16)


def _pool(pool_in, w_bd, scale):
    B, L, _ = pool_in.shape
    return pl.pallas_call(
        _pool_kernel,
        grid=(B,),
        in_specs=[
            pl.BlockSpec((1, L, POOL_WIDTH), lambda b: (b, 0, 0)),
            pl.BlockSpec((POOL_WIDTH, POOL_WIDTH), lambda b: (0, 0)),
            pl.BlockSpec((1, POOL_WIDTH), lambda b: (0, 0)),
        ],
        out_specs=pl.BlockSpec((1, L, POOL_WIDTH), lambda b: (b, 0, 0)),
        out_shape=jax.ShapeDtypeStruct((B, L, POOL_WIDTH), BF16),
        compiler_params=_cparams(("parallel",)),
        name="pool_mixer",
    )(pool_in, w_bd, scale)


def _mix_out_kernel(xa_ref, xb_ref, ret_ref, hy_ref, pool_ref, wr_ref, wh_ref, wp_ref, g_ref, x1_ref, h2_ref, *, nbp):
    tm = xa_ref.shape[1]
    hy_t = jnp.concatenate([hy_ref[:, j].reshape(HY_WIDTH, LANES) for j in range(tm // LANES)],
                           axis=1).astype(BF16)
    mix = jnp.dot(ret_ref[0], wr_ref[...], preferred_element_type=F32)
    mix += lax.dot_general(hy_t, wh_ref[...], (((0,), (0,)), ((), ())), preferred_element_type=F32)
    mix += jnp.dot(pool_ref[0], wp_ref[...], preferred_element_type=F32)
    x1 = _pick(nbp, xa_ref, xb_ref) + mix
    x1_ref[0] = x1
    h2_ref[0] = _rms(x1, g_ref[...]).astype(BF16)


def _mix_out(x, ret, hy_o, pool_o, wr, wh, wp, g, tm):
    xa, xb, nbp, off = x
    B, L, _ = ret.shape
    nt = L // tm
    return pl.pallas_call(
        functools.partial(_mix_out_kernel, nbp=nbp),
        grid=(B, nt),
        in_specs=[
            *_pair_specs(tm, D_MODEL, nt, nbp, off),
            pl.BlockSpec((1, tm, RET_WIDTH), lambda b, i: (b, i, 0)),
            pl.BlockSpec((HY_WIDTH // SUBLANES, tm // LANES, SUBLANES, LANES), lambda b, i: (0, i, b, 0)),
            pl.BlockSpec((1, tm, POOL_WIDTH), lambda b, i: (b, i, 0)),
            pl.BlockSpec((RET_WIDTH, D_MODEL), lambda b, i: (0, 0)),
            pl.BlockSpec((HY_WIDTH, D_MODEL), lambda b, i: (0, 0)),
            pl.BlockSpec((POOL_WIDTH, D_MODEL), lambda b, i: (0, 0)),
            pl.BlockSpec((1, D_MODEL), lambda b, i: (0, 0)),
        ],
        out_specs=[
            pl.BlockSpec((1, tm, D_MODEL), lambda b, i: (b, i, 0)),
            pl.BlockSpec((1, tm, D_MODEL), lambda b, i: (b, i, 0)),
        ],
        out_shape=[
            jax.ShapeDtypeStruct((B, L, D_MODEL), F32),
            jax.ShapeDtypeStruct((B, L, D_MODEL), BF16),
        ],
        compiler_params=_cparams(("arbitrary", "arbitrary")),
        name="mix_out",
    )(xa, xb, ret, hy_o, pool_o, wr, wh, wp, g)


def _ffn_kernel(h_ref, hp_ref, hn_ref, x1_ref, wup_ref, cw_ref, wd_ref, pa_ref, pb_ref, pw_ref, pg_ref,
                pn_ref, fin_ref, *o_refs, nbp, final):
    tm = h_ref.shape[1]
    i = pl.program_id(1)
    nt = pl.num_programs(1)

    hx = jnp.concatenate([h_ref[0], hp_ref[0], hn_ref[0]], axis=0)
    row = lax.broadcasted_iota(jnp.int32, (tm, FFN_SUB), 0)
    n_sub = D_FF // FFN_SUB

    def up_proj(s):
        c0 = s * FFN_SUB
        gate = jnp.dot(hx, wup_ref[:, c0:c0 + FFN_SUB], preferred_element_type=F32)
        up = jnp.dot(hx[:tm], wup_ref[:, D_FF + c0:D_FF + c0 + FFN_SUB], preferred_element_type=F32)
        return gate, up

    acc = x1_ref[0]
    nxt = up_proj(0)
    group, g0 = [], 0
    for s in range(n_sub):
        c0 = s * FFN_SUB
        gx, up = nxt
        if s + 1 < n_sub:
            nxt = up_proj(s + 1)
        gate = gx[:tm]
        g_prev = jnp.where(i == 0, 0.0, gx[tm + BF16_ROWS - 1:tm + BF16_ROWS])
        g_next = jnp.where(i == nt - 1, 0.0, gx[tm + BF16_ROWS:tm + BF16_ROWS + 1])
        g_up = jnp.where(row == 0, g_prev, pltpu.roll(gate, 1, axis=0))
        g_dn = jnp.where(row == tm - 1, g_next, pltpu.roll(gate, tm - 1, axis=0))
        cw = cw_ref[:, c0:c0 + FFN_SUB]
        gc = g_up * cw[0:1] + gate * cw[1:2] + g_dn * cw[2:3]
        group.append((jax.nn.gelu(gc) * up).astype(BF16))
        if len(group) == FFN_DOWN_GROUP or s == n_sub - 1:
            a = jnp.concatenate(group, axis=1)
            acc = acc + jnp.dot(a, wd_ref[g0:c0 + FFN_SUB, :], preferred_element_type=F32)
            group, g0 = [], c0 + FFN_SUB

    x2 = acc
    p = _pick(nbp, pa_ref, pb_ref).astype(BF16)
    e = _rms(jnp.dot(p, pw_ref[...], preferred_element_type=F32), pn_ref[...])
    gt = jax.nn.sigmoid(jnp.dot(x2.astype(BF16), pg_ref[...], preferred_element_type=F32))
    x3 = x2 + gt * e
    if not final:
        o_refs[0][0] = x3
        return
    y = _rms(x3, fin_ref[...])
    b = pl.program_id(0)

    @pl.when(b < nbp)
    def _():
        o_refs[0][0] = y

    @pl.when(b >= nbp)
    def _():
        o_refs[1][0] = y


def _ffn(h2, x1, w_up, cw, w_down, p, layer, ple_w, ple_gate, ple_norm, fin, tm, final):
    pa, pb, nbp, off = p
    B, L, _ = x1.shape
    nt = L // tm
    halo = BF16_ROWS
    nh = L // halo
    per = tm // halo
    full = lambda shape: pl.BlockSpec(shape, lambda b, i: (0,) * len(shape), pipeline_mode=pl.Buffered(1))
    tile = pl.BlockSpec((1, tm, D_MODEL), lambda b, i: (b, i, 0))
    if final:
        out_specs = list(_pair_specs(tm, D_MODEL, nt, nbp, 0))
        out_shape = [jax.ShapeDtypeStruct((nbp, L, D_MODEL), F32), jax.ShapeDtypeStruct((B - nbp, L, D_MODEL), F32)]
    else:
        out_specs = [tile]
        out_shape = [jax.ShapeDtypeStruct((B, L, D_MODEL), F32)]
    return pl.pallas_call(
        functools.partial(_ffn_kernel, nbp=nbp, final=final),
        grid=(B, nt),
        in_specs=[
            tile,
            pl.BlockSpec((1, halo, D_MODEL), lambda b, i: (b, jnp.maximum(i * per - 1, 0), 0)),
            pl.BlockSpec((1, halo, D_MODEL), lambda b, i: (b, jnp.minimum((i + 1) * per, nh - 1), 0)),
            tile,
            full((D_MODEL, 2 * D_FF)), full((3, D_FF)), full((D_FF, D_MODEL)),
            *_pair_specs(tm, PLE_DIM, nt, nbp, off, lead=layer),
            full((PLE_DIM, D_MODEL)), full((D_MODEL, D_MODEL)), full((1, D_MODEL)), full((1, D_MODEL)),
        ],
        out_specs=out_specs,
        out_shape=out_shape,
        compiler_params=_cparams(("arbitrary", "arbitrary")),
        name="ffn_ple",
    )(h2, h2, h2, x1, w_up, cw, w_down, pa, pb, ple_w, ple_gate, ple_norm, fin)


def _rope_tables(L):
    half = HEAD_DIM // 2
    inv = ROPE_THETA ** (-jnp.arange(half, dtype=F32) / half)
    ang = jnp.arange(L, dtype=F32)[:, None] * inv[None, :]
    cos, sin = jnp.cos(ang), jnp.sin(ang)
    return jnp.concatenate([cos, cos], axis=-1), jnp.concatenate([-sin, sin], axis=-1)


def _block_diag(pool_w):
    out = jnp.zeros((POOL_WIDTH, POOL_WIDTH), pool_w.dtype)
    for g in range(len(POOL_WINDOWS)):
        sl = slice(g * POOL_GROUP_DIM, (g + 1) * POOL_GROUP_DIM)
        out = out.at[sl, sl].set(pool_w[g])
    return out


def _layer(x, B, p, layer, lw, cos, sin, fin, tm, final):
    (norm_mix, w_in, dec_f, dec_b, ret_gn, hy_conv, hy_w1, hy_b1, hy_freq, hy_w2, hy_b2, hy_w3, hy_bias,
     pool_w, pool_scale, w_out, norm_ffn, w_up, ffn_conv, w_down, ple_w, ple_gate, ple_norm) = lw
    L = x[0].shape[1]
    R = RET_WIDTH

    w_q, w_k, w_v, w_g, w_hy, w_pool = jnp.split(w_in, [R, 2 * R, 3 * R, 4 * R, 4 * R + HY_IN_WIDTH], axis=1)
    wa = jnp.concatenate([w_q, w_v, w_g, w_pool], axis=1).astype(BF16)
    qvg, kt, hy_t, pool_in = _in_proj(x, B, norm_mix[None], wa, w_k.T.astype(BF16), w_hy.astype(BF16), hy_conv, tm)

    dec = jnp.broadcast_to(jnp.stack([dec_f, dec_b], axis=1)[:, :, None], (RET_HEADS, 2, LANES)).astype(F32)
    ret = _retention(qvg, kt, dec, cos, sin, cos.T, sin.T, ret_gn[None])

    w1t = jnp.zeros((HY_HIDDEN, HY_HIDDEN), F32).at[:, :hy_w1.shape[0]].set(hy_w1.T)
    w3t = hy_w3.T.reshape(2, 2, HY_WIDTH, HY_HIDDEN)
    kk = _hyena_filters(w1t, hy_b1[:, None], hy_freq[:, None], hy_w2.T, hy_b2[:, None], w3t, L)
    bias = jnp.broadcast_to(hy_bias.reshape(2, HY_WIDTH // SUBLANES, SUBLANES).transpose(1, 0, 2)[..., None],
                            (HY_WIDTH // SUBLANES, 2, SUBLANES, HY_BLOCK))
    hy_o = _hyena(kk, hy_t, bias, B)

    pool_o = _pool(pool_in, _block_diag(pool_w).astype(BF16), pool_scale[None])

    wo = w_out.astype(BF16)
    x1, h2 = _mix_out(x, ret, hy_o, pool_o, wo[:R], wo[R:R + HY_WIDTH], wo[R + HY_WIDTH:], norm_ffn[None], tm)

    return _ffn(h2, x1, w_up.astype(BF16), ffn_conv, w_down.astype(BF16), p, layer, ple_w.astype(BF16),
                ple_gate.astype(BF16), ple_norm[None], fin, tm, final)


def _trunk(x_a, x_b, p_a, p_b, layer_weights, norm_final, tm):
    depth = p_a.shape[0]
    nbp, L = x_a.shape[0], x_a.shape[1]
    B = nbp + x_b.shape[0]
    cos, sin = _rope_tables(L)
    fin = norm_final[None]
    x = (x_a, x_b, nbp, 0)
    p = (p_a, p_b, nbp, 0)
    for i in range(depth):
        out = _layer(x, B, p, i, [w[i] for w in layer_weights], cos, sin, fin, tm, i == depth - 1)
        x = (out[0], out[0], nbp, nbp)
    return out[0], out[1]


def kernel(x_prompt, x_sample, p_prompt, p_sample, norm_mix, w_in, ret_decay_fwd, ret_decay_bwd, ret_gn,
           hy_short_conv, hy_w1, hy_b1, hy_freq, hy_w2, hy_b2, hy_w3, hy_bias, pool_w, pool_scale, w_out,
           norm_ffn, ffn_w_up, ffn_conv, ffn_w_down, ple_w, ple_gate_w, ple_norm, norm_final):
    layer_weights = (norm_mix, w_in, ret_decay_fwd, ret_decay_bwd, ret_gn, hy_short_conv, hy_w1, hy_b1,
                     hy_freq, hy_w2, hy_b2, hy_w3, hy_bias, pool_w, pool_scale, w_out, norm_ffn, ffn_w_up,
                     ffn_conv, ffn_w_down, ple_w, ple_gate_w, ple_norm)
    return _trunk(x_prompt, x_sample, p_prompt, p_sample, layer_weights, norm_final, ROW_TILE)
```

```python
import functools
import math

import jax
import jax.numpy as jnp
from jax import lax
from jax.experimental import pallas as pl
from jax.experimental.pallas import tpu as pltpu

F32 = jnp.float32
BF16 = jnp.bfloat16

D_MODEL = 1024
RET_WIDTH = 512
RET_HEADS = 4
HEAD_DIM = 128
HY_WIDTH = 256
POOL_WIDTH = 256
POOL_GROUP_DIM = 64
POOL_WINDOWS = (2, 4, 8, 16)
POOL_PAD = 16
D_FF = 2816
PLE_DIM = 256
CHUNK = 128
ROPE_THETA = 10000.0
HY_EMB_BANDS = 16
HY_HIDDEN = 64
HY_MIN_DECAY = math.log(1e-2) / 1.5
HY_MAX_DECAY = math.log(1e-2) / 0.3
EPS = 1e-6

LANES = 128
SUBLANES = 8
BF16_ROWS = 16
MXU_DIM = 256
VMEM_LIMIT = 56 * 1024 * 1024

HY_BLOCK = MXU_DIM
HY_PAIR_UNROLL = 1
QVG_WIDTH = 3 * RET_WIDTH
RET_UNROLL = True
HY_IN_WIDTH = 3 * HY_WIDTH
ROW_TILE = 512
WIDE_TILE = 2
FFN_SUB = MXU_DIM
FFN_DOWN_GROUP = 4


def _cparams(sem):
    return pltpu.CompilerParams(dimension_semantics=sem, vmem_limit_bytes=VMEM_LIMIT)


def _rms(x, g):
    return x * lax.rsqrt(jnp.mean(x * x, axis=-1, keepdims=True) + EPS) * g


def _pair_specs(tm, width, nt, nbp, off, lead=None):
    pre = () if lead is None else (lead,)
    shape = (1, tm, width) if lead is None else (None, 1, tm, width)

    def first(b, i):
        return pre + (jnp.minimum(b, nbp - 1), jnp.where(b < nbp, i, nt - 1), 0)

    def second(b, i):
        return pre + (jnp.maximum(b - nbp, 0) + off, jnp.where(b < nbp, 0, i), 0)

    return pl.BlockSpec(shape, first), pl.BlockSpec(shape, second)


def _pick(nbp, a_ref, b_ref):
    return jnp.where(pl.program_id(0) < nbp, a_ref[0], b_ref[0])


def _pair_halo_specs(rows, width, tm, L, nbp, off):
    per, last = tm // rows, L // rows - 1

    def make(first, after):
        def index(b, i):
            r = jnp.minimum((i + 1) * per, last) if after else jnp.maximum(i * per - 1, 0)
            if first:
                return (jnp.minimum(b, nbp - 1), jnp.where(b < nbp, r, 0), 0)
            return (jnp.maximum(b - nbp, 0) + off, jnp.where(b < nbp, 0, r), 0)
        return pl.BlockSpec((1, rows, width), index)

    return make(True, False), make(False, False), make(True, True), make(False, True)


def _dwconv3_rows(u, tm, cw, first_tile, last_tile):
    main = u[:tm]
    prev = jnp.where(first_tile, 0.0, u[tm + BF16_ROWS - 1:tm + BF16_ROWS])
    nxt = jnp.where(last_tile, 0.0, u[tm + BF16_ROWS:tm + BF16_ROWS + 1])
    row = lax.broadcasted_iota(jnp.int32, main.shape, 0)
    up = jnp.where(row == 0, prev, pltpu.roll(main, 1, axis=0))
    dn = jnp.where(row == tm - 1, nxt, pltpu.roll(main, tm - 1, axis=0))
    return up * cw[0:1] + main * cw[1:2] + dn * cw[2:3]


def _in_proj_kernel(xa_ref, xb_ref, xpa_ref, xpb_ref, xna_ref, xnb_ref, g_ref, wa_ref, wk_ref, wh_ref, cw_ref,
                    qvg_ref, kt_ref, hy_ref, pool_ref, *, nbp):
    tm = xa_ref.shape[1]
    i = pl.program_id(1)
    g = g_ref[...]
    h = _rms(_pick(nbp, xa_ref, xb_ref), g).astype(BF16)
    hx = jnp.concatenate([h, _rms(_pick(nbp, xpa_ref, xpb_ref), g).astype(BF16),
                          _rms(_pick(nbp, xna_ref, xnb_ref), g).astype(BF16)], axis=0)
    u = jnp.dot(hx, wh_ref[...], preferred_element_type=F32)
    y = _dwconv3_rows(u, tm, cw_ref[...], i == 0, i == pl.num_programs(1) - 1)
    yt = y.T
    for j in range(tm // LANES):
        hy_ref[:, j, :, :] = yt[:, j * LANES:(j + 1) * LANES].reshape(HY_IN_WIDTH // SUBLANES, SUBLANES, LANES)
    kt = lax.dot_general(wk_ref[...], h, (((1,), (1,)), ((), ())), preferred_element_type=F32)
    kt_ref[0] = kt.astype(BF16)
    a = jnp.dot(h, wa_ref[...], preferred_element_type=F32).astype(BF16)
    qvg_ref[0] = a[:, :QVG_WIDTH]
    pool_ref[0] = a[:, QVG_WIDTH:]


def _in_proj(x, B, g, wa, wk_t, wh, cw, tm):
    xa, xb, nbp, off = x
    L = xa.shape[1]
    nt = L // tm
    const = lambda shape: pl.BlockSpec(shape, lambda b, i: (0,) * len(shape))
    return pl.pallas_call(
        functools.partial(_in_proj_kernel, nbp=nbp),
        grid=(B, nt),
        in_specs=[
            *_pair_specs(tm, D_MODEL, nt, nbp, off),
            *_pair_halo_specs(BF16_ROWS, D_MODEL, tm, L, nbp, off),
            const((1, D_MODEL)),
            const((D_MODEL, QVG_WIDTH + POOL_WIDTH)),
            const((RET_WIDTH, D_MODEL)),
            const((D_MODEL, HY_IN_WIDTH)),
            const((3, HY_IN_WIDTH)),
        ],
        out_specs=[
            pl.BlockSpec((1, tm, QVG_WIDTH), lambda b, i: (b, i, 0)),
            pl.BlockSpec((1, RET_WIDTH, tm), lambda b, i: (b, 0, i)),
            pl.BlockSpec((HY_IN_WIDTH // SUBLANES, tm // LANES, SUBLANES, LANES), lambda b, i: (0, i, b, 0)),
            pl.BlockSpec((1, tm, POOL_WIDTH), lambda b, i: (b, i, 0)),
        ],
        out_shape=[
            jax.ShapeDtypeStruct((B, L, QVG_WIDTH), BF16),
            jax.ShapeDtypeStruct((B, RET_WIDTH, L), BF16),
            jax.ShapeDtypeStruct((HY_IN_WIDTH // SUBLANES, L // LANES, B * SUBLANES, LANES), F32),
            jax.ShapeDtypeStruct((B, L, POOL_WIDTH), BF16),
        ],
        compiler_params=_cparams(("arbitrary", "arbitrary")),
        name="in_proj",
    )(xa, xb, xa, xb, xa, xb, g, wa, wk_t, wh, cw)


def _log_sigmoid(x):
    return jnp.minimum(x, 0.0) - jnp.log(1.0 + jnp.exp(-jnp.abs(x)))


def _retention_kernel(dec_ref, cos_ref, sin_ref, cost_ref, sint_ref, gn_ref, q_ref, kt_ref, v_ref, g_ref, o_ref,
                      qr_s, kr_s, kv_s, st_s, tab_s):
    L = q_ref.shape[1]
    n_chunks = L // CHUNK
    C = CHUNK
    scale = HEAD_DIM ** -0.5

    lg = _log_sigmoid(dec_ref[0])
    lgf, lgb = lg[0:1], lg[1:2]
    ri = lax.broadcasted_iota(jnp.int32, (C, C), 0)
    ci = lax.broadcasted_iota(jnp.int32, (C, C), 1)
    rf = ri.astype(F32)
    cf = ci.astype(F32)
    dist = jnp.abs(ri - ci).astype(F32)
    tab_s[0] = jnp.exp(dist * jnp.where(ri >= ci, lgf, lgb)) * scale
    tab_s[1] = jnp.exp((rf + 1.0) * lgf) * scale
    tab_s[2] = jnp.exp((C - rf) * lgb) * scale
    tab_s[3] = jnp.exp((C - 1.0 - cf) * lgf)
    tab_s[4] = jnp.exp(cf * lgb)
    dec_f = jnp.exp(C * lgf)
    dec_b = jnp.exp(C * lgb)

    qf = q_ref[0].astype(F32)
    qr_s[...] = qf * cos_ref[...] + pltpu.roll(qf, HEAD_DIM // 2, axis=1) * sin_ref[...]
    kf = kt_ref[0].astype(F32)
    kr_s[...] = kf * cost_ref[...] + pltpu.roll(kf, HEAD_DIM // 2, axis=0) * sint_ref[...]

    def summaries(n, carry):
        c0 = pl.multiple_of(n * C, C)
        kt = kr_s[:, pl.ds(c0, C)]
        lhs = jnp.concatenate([kt * tab_s[3], kt * tab_s[4]], axis=0).astype(BF16)
        kv_s[n] = jnp.dot(lhs, v_ref[0, pl.ds(c0, C), :], preferred_element_type=F32)
        return carry

    lax.fori_loop(0, n_chunks, summaries, 0, unroll=RET_UNROLL)

    def scan(i, carry):
        sf, sb = carry
        m = n_chunks - 1 - i
        st_s[i, :C, :] = sf.astype(BF16)
        st_s[m, C:, :] = sb.astype(BF16)
        return sf * dec_f + kv_s[i, :C, :], sb * dec_b + kv_s[m, C:, :]

    zero = jnp.zeros((C, C), F32)
    lax.fori_loop(0, n_chunks, scan, (zero, zero), unroll=RET_UNROLL)

    gain = gn_ref[...]

    def outputs(n, carry):
        c0 = pl.multiple_of(n * C, C)
        qn = qr_s[pl.ds(c0, C), :]
        vn = v_ref[0, pl.ds(c0, C), :]
        sc = jnp.dot(qn.astype(BF16), kr_s[:, pl.ds(c0, C)].astype(BF16), preferred_element_type=F32)
        lhs = jnp.concatenate([sc * tab_s[0], qn * tab_s[1], qn * tab_s[2]], axis=1).astype(BF16)
        rhs = jnp.concatenate([vn, st_s[n]], axis=0)
        o = jnp.dot(lhs, rhs, preferred_element_type=F32)
        mu = jnp.mean(o, axis=-1, keepdims=True)
        oc = o - mu
        var = jnp.mean(oc * oc, axis=-1, keepdims=True)
        gate = g_ref[0, pl.ds(c0, C), :].astype(F32)
        y = oc * lax.rsqrt(var + EPS) * gain * (gate * jax.nn.sigmoid(gate))
        o_ref[0, pl.ds(c0, C), :] = y.astype(BF16)
        return carry

    lax.fori_loop(0, n_chunks, outputs, 0, unroll=RET_UNROLL)


def _retention(qvg, kt, dec, cos, sin, cos_t, sin_t, gn):
    B, L, _ = qvg.shape
    H = RET_HEADS

    def col(off):
        return pl.BlockSpec((1, L, HEAD_DIM), lambda b, h: (b, 0, off + h))

    table = pl.BlockSpec((L, HEAD_DIM), lambda b, h: (0, 0))
    table_t = pl.BlockSpec((HEAD_DIM, L), lambda b, h: (0, 0))
    return pl.pallas_call(
        _retention_kernel,
        grid=(B, H),
        in_specs=[
            pl.BlockSpec((1, 2, LANES), lambda b, h: (h, 0, 0)),
            table, table, table_t, table_t,
            pl.BlockSpec((1, HEAD_DIM), lambda b, h: (0, h)),
            col(0),
            pl.BlockSpec((1, HEAD_DIM, L), lambda b, h: (b, h, 0)),
            col(H), col(2 * H),
        ],
        out_specs=pl.BlockSpec((1, L, HEAD_DIM), lambda b, h: (b, 0, h)),
        out_shape=jax.ShapeDtypeStruct((B, L, RET_WIDTH), BF16),
        scratch_shapes=[
            pltpu.VMEM((L, HEAD_DIM), F32),
            pltpu.VMEM((HEAD_DIM, L), F32),
            pltpu.VMEM((L // CHUNK, 2 * CHUNK, CHUNK), F32),
            pltpu.VMEM((L // CHUNK, 2 * CHUNK, CHUNK), BF16),
            pltpu.VMEM((5, CHUNK, CHUNK), F32),
        ],
        compiler_params=_cparams(("parallel", "parallel")),
        name="retention",
    )(dec, cos, sin, cos_t, sin_t, gn, qvg, kt, qvg, qvg)


def _filter_kernel(w1_ref, b1_ref, fr_ref, w2_ref, b2_ref, w3_ref, kk_ref, hid_s):
    L = kk_ref.shape[1] // 2
    hi = lax.Precision.HIGHEST

    @pl.when(pl.program_id(0) == 0)
    def _():
        r = lax.broadcasted_iota(jnp.int32, (HY_HIDDEN, L), 0)
        lane = lax.broadcasted_iota(jnp.int32, (HY_HIDDEN, L), 1)
        band_idx = jnp.where(r <= HY_EMB_BANDS, r - 1, r - 1 - HY_EMB_BANDS).astype(F32)
        band = 1e-4 + band_idx * ((HY_EMB_BANDS - 1 - 1e-4) / (HY_EMB_BANDS - 1))
        fr = fr_ref[...]

        def hidden(s_int):
            s = s_int.astype(F32)
            t = s / (L - 1.0)
            ang = (2.0 * math.pi / L) * s * band
            feats = jnp.where(r == 0, t,
                              jnp.where(r <= HY_EMB_BANDS, jnp.cos(ang),
                                        jnp.where(r <= 2 * HY_EMB_BANDS, -jnp.sin(ang), 0.0)))
            h = jnp.sin(fr * (jnp.dot(w1_ref[...], feats, precision=hi, preferred_element_type=F32) + b1_ref[...]))
            return jnp.sin(fr * (jnp.dot(w2_ref[...], h, precision=hi, preferred_element_type=F32) + b2_ref[...]))

        hid_s[0] = hidden(lane)
        hid_s[1] = hidden(L - lane)

    crow = lax.broadcasted_iota(jnp.int32, (HY_WIDTH, L), 0).astype(F32)
    delta = jnp.abs(HY_MIN_DECAY + crow * ((HY_MAX_DECAY - HY_MIN_DECAY) / (HY_WIDTH - 1)))
    clane = lax.broadcasted_iota(jnp.int32, (HY_WIDTH, L), 1)

    def taps(hid, s_row_int, w3):
        h = jnp.dot(w3, hid, precision=hi, preferred_element_type=F32)
        return h * jnp.exp(-(s_row_int.astype(F32) / (L - 1.0)) * delta)

    h_f = taps(hid_s[0], clane, w3_ref[0, 0])
    h_b = taps(hid_s[1], L - clane, w3_ref[0, 1])
    h_b = jnp.where(clane == 0, 0.0, h_b)
    norm = jnp.sum(jnp.abs(h_f), axis=-1, keepdims=True) + jnp.sum(jnp.abs(h_b), axis=-1, keepdims=True)
    inv = 1.0 / norm
    kk = jnp.concatenate([h_b * inv, h_f * inv], axis=1)
    bits = pltpu.bitcast(kk, jnp.uint32)
    bf = (bits + jnp.uint32(0x7FFF) + ((bits >> 16) & jnp.uint32(1))) >> 16
    lane2 = lax.broadcasted_iota(jnp.int32, bf.shape, 1)
    prev = jnp.where(lane2 == 0, jnp.uint32(0), pltpu.roll(bf, 1, axis=1))
    kk_ref[...] = bf | (prev << 16)


def _hyena_filters(w1t, b1, fr, w2t, b2, w3t, L):
    small = lambda shape: pl.BlockSpec(shape, lambda o: (0,) * len(shape))
    return pl.pallas_call(
        _filter_kernel,
        grid=(2,),
        in_specs=[
            small((HY_HIDDEN, HY_HIDDEN)), small((HY_HIDDEN, 1)), small((HY_HIDDEN, 1)),
            small((HY_HIDDEN, HY_HIDDEN)), small((HY_HIDDEN, 1)),
            pl.BlockSpec((1, 2, HY_WIDTH, HY_HIDDEN), lambda o: (o, 0, 0, 0)),
        ],
        out_specs=pl.BlockSpec((HY_WIDTH, 2 * L), lambda o: (o, 0)),
        out_shape=jax.ShapeDtypeStruct((2 * HY_WIDTH, 2 * L), jnp.uint32),
        scratch_shapes=[pltpu.VMEM((2, HY_HIDDEN, L), F32)],
        compiler_params=_cparams(("arbitrary",)),
        name="hyena_filters",
    )(w1t, b1, fr, w2t, b2, w3t)


def _hyena_kernel(kk1_ref, kk2_ref, xv_ref, x1_ref, x2_ref, bias_ref, out_ref,
                  toep1_s, toep2_s, u2_s, x1u2_s, y2_s, *, B, BP):
    n_tiles = xv_ref.shape[0]
    L = n_tiles * LANES
    T = HY_BLOCK
    NB = L // T
    TPB = T // LANES

    @pl.when(pl.program_id(0) == 0)
    def _():
        toep1_s[...] = jnp.zeros_like(toep1_s)
        toep2_s[...] = jnp.zeros_like(toep2_s)
        u2_s[...] = jnp.zeros_like(u2_s)
        x1u2_s[...] = jnp.zeros_like(x1u2_s)

    def build_toeplitz(kk_ref, c, toep_s):
        row = jnp.broadcast_to(kk_ref[pl.ds(c, 1), :], (SUBLANES, 2 * L))
        for g in range(LANES // BF16_ROWS):
            rolled = pltpu.roll(row, BF16_ROWS * g, axis=1, stride=2, stride_axis=0)
            tile = pltpu.bitcast(rolled, BF16)
            toep_s[BF16_ROWS * g:BF16_ROWS * (g + 1), :] = tile
            toep_s[LANES + BF16_ROWS * g:LANES + BF16_ROWS * (g + 1), LANES:] = tile[:, :2 * L - LANES]

    def long_conv(toep_s, u_s, y_s):
        y_s[...] = jnp.zeros_like(y_s)
        for d in range(-(NB - 1), NB):
            n = NB - abs(d)
            src = max(0, -d) * BP
            dst = max(0, d) * BP
            m = toep_s[:, L + d * T:L + (d + 1) * T]
            lhs = u_s[src:src + n * BP, :].astype(BF16)
            y_s[dst:dst + n * BP, :] += jnp.dot(lhs, m, preferred_element_type=F32)

    def channel(c, k):
        u_s, x1u_s, y_s = u2_s.at[k], x1u2_s.at[k], y2_s.at[k]
        rows = pl.ds(c, B, stride=SUBLANES)
        for j in range(n_tiles):
            J, l0 = j // TPB, (j % TPB) * LANES
            u_s[J * BP:J * BP + B, l0:l0 + LANES] = xv_ref.at[j][rows, :]
            x1u_s[J * BP:J * BP + B, l0:l0 + LANES] = x1_ref.at[j][rows, :]
        b0 = bias_ref[0, 0, pl.ds(c, 1), :]
        b1 = bias_ref[0, 1, pl.ds(c, 1), :]
        build_toeplitz(kk2_ref, c, toep2_s)
        long_conv(toep1_s, u_s, y_s)
        u_s[...] = x1u_s[...] * (y_s[...] + u_s[...] * b0)
        build_toeplitz(kk1_ref, jnp.minimum(c + 1, SUBLANES - 1), toep1_s)
        long_conv(toep2_s, u_s, y_s)
        for j in range(n_tiles):
            J, l0 = j // TPB, (j % TPB) * LANES
            yy = y_s[J * BP:J * BP + B, l0:l0 + LANES] + u_s[J * BP:J * BP + B, l0:l0 + LANES] * b1[:, :LANES]
            out_ref.at[j][rows, :] = x2_ref.at[j][rows, :] * yy

    def channel_pair(i, carry):
        channel(2 * i, 0)
        channel(2 * i + 1, 1)
        return carry

    build_toeplitz(kk1_ref, 0, toep1_s)
    lax.fori_loop(0, SUBLANES // 2, channel_pair, 0, unroll=HY_PAIR_UNROLL)


def _hyena(kk, hy_t, bias, B):
    n_in, n_tiles, R, _ = hy_t.shape
    L = n_tiles * LANES
    NB = L // HY_BLOCK
    BP = -(-B // SUBLANES) * SUBLANES
    n_c = HY_WIDTH // SUBLANES

    def sect(s):
        return pl.BlockSpec((None, n_tiles, R, LANES), lambda c: (s * n_c + c, 0, 0, 0))

    return pl.pallas_call(
        functools.partial(_hyena_kernel, B=B, BP=BP),
        grid=(n_c,),
        in_specs=[
            pl.BlockSpec((SUBLANES, 2 * L), lambda c: (c, 0)),
            pl.BlockSpec((SUBLANES, 2 * L), lambda c: (n_c + c, 0)),
            sect(0), sect(1), sect(2),
            pl.BlockSpec((1, 2, SUBLANES, HY_BLOCK), lambda c: (c, 0, 0, 0)),
        ],
        out_specs=pl.BlockSpec((None, n_tiles, R, LANES), lambda c: (c, 0, 0, 0)),
        out_shape=jax.ShapeDtypeStruct((n_c, n_tiles, R, LANES), F32),
        scratch_shapes=[
            pltpu.VMEM((2 * LANES, 2 * L), BF16), pltpu.VMEM((2 * LANES, 2 * L), BF16),
            pltpu.VMEM((2, NB * BP, HY_BLOCK), F32),
            pltpu.VMEM((2, NB * BP, HY_BLOCK), F32),
            pltpu.VMEM((2, NB * BP, HY_BLOCK), F32),
        ],
        compiler_params=_cparams(("arbitrary",)),
        name="hyena",
    )(kk, kk, hy_t, hy_t, hy_t, bias)


def _pool_kernel(u_ref, w_ref, sc_ref, o_ref):
    L = u_ref.shape[1]
    u = u_ref[0].astype(F32)
    t = lax.broadcasted_iota(jnp.int32, (L, POOL_WIDTH), 0)
    lane = lax.broadcasted_iota(jnp.int32, (L, POOL_WIDTH), 1)
    pad = jnp.zeros((POOL_PAD, POOL_WIDTH), F32)
    up = jnp.concatenate([pad, u, pad], axis=0)
    n = L + 2 * POOL_PAD

    def both(x, k):
        return pltpu.roll(x, k, axis=0) + pltpu.roll(x, n - k, axis=0)

    q2 = pltpu.roll(up, 1, axis=0) + up
    q4 = both(q2, 1)
    q8 = both(q4, 2)
    q16 = both(q8, 4)
    p2, p4, p8, p16 = (q[POOL_PAD:POOL_PAD + L] for q in (q2, q4, q8, q16))
    g = lane // POOL_GROUP_DIM
    half = jnp.where(g == 0, 1, jnp.where(g == 1, 2, jnp.where(g == 2, 4, 8)))
    total = jnp.where(g == 0, p2, jnp.where(g == 1, p4, jnp.where(g == 2, p8, p16)))
    lo = jnp.maximum(t - half, 0)
    hi = jnp.minimum(t + half - 1, L - 1)
    cnt = (hi - lo + 1).astype(F32)
    d = (total / cnt - u).astype(BF16)
    o_ref[0] = (jnp.dot(d, w_ref[...], preferred_element_type=F32) * sc_ref[...]).astype(BF16)


def _pool(pool_in, w_bd, scale):
    B, L, _ = pool_in.shape
    return pl.pallas_call(
        _pool_kernel,
        grid=(B,),
        in_specs=[
            pl.BlockSpec((1, L, POOL_WIDTH), lambda b: (b, 0, 0)),
            pl.BlockSpec((POOL_WIDTH, POOL_WIDTH), lambda b: (0, 0)),
            pl.BlockSpec((1, POOL_WIDTH), lambda b: (0, 0)),
        ],
        out_specs=pl.BlockSpec((1, L, POOL_WIDTH), lambda b: (b, 0, 0)),
        out_shape=jax.ShapeDtypeStruct((B, L, POOL_WIDTH), BF16),
        compiler_params=_cparams(("parallel",)),
        name="pool_mixer",
    )(pool_in, w_bd, scale)


def _mix_out_kernel(xa_ref, xb_ref, ret_ref, hy_ref, pool_ref, wr_ref, wh_ref, wp_ref, g_ref, x1_ref, h2_ref, *, nbp):
    tm = xa_ref.shape[1]
    hy_t = jnp.concatenate([hy_ref[:, j].reshape(HY_WIDTH, LANES) for j in range(tm // LANES)],
                           axis=1).astype(BF16)
    mix = jnp.dot(ret_ref[0], wr_ref[...], preferred_element_type=F32)
    mix += lax.dot_general(hy_t, wh_ref[...], (((0,), (0,)), ((), ())), preferred_element_type=F32)
    mix += jnp.dot(pool_ref[0], wp_ref[...], preferred_element_type=F32)
    x1 = _pick(nbp, xa_ref, xb_ref) + mix
    x1_ref[0] = x1
    h2_ref[0] = _rms(x1, g_ref[...]).astype(BF16)


def _mix_out(x, ret, hy_o, pool_o, wr, wh, wp, g, tm):
    xa, xb, nbp, off = x
    B, L, _ = ret.shape
    nt = L // tm
    return pl.pallas_call(
        functools.partial(_mix_out_kernel, nbp=nbp),
        grid=(B, nt),
        in_specs=[
            *_pair_specs(tm, D_MODEL, nt, nbp, off),
            pl.BlockSpec((1, tm, RET_WIDTH), lambda b, i: (b, i, 0)),
            pl.BlockSpec((HY_WIDTH // SUBLANES, tm // LANES, SUBLANES, LANES), lambda b, i: (0, i, b, 0)),
            pl.BlockSpec((1, tm, POOL_WIDTH), lambda b, i: (b, i, 0)),
            pl.BlockSpec((RET_WIDTH, D_MODEL), lambda b, i: (0, 0)),
            pl.BlockSpec((HY_WIDTH, D_MODEL), lambda b, i: (0, 0)),
            pl.BlockSpec((POOL_WIDTH, D_MODEL), lambda b, i: (0, 0)),
            pl.BlockSpec((1, D_MODEL), lambda b, i: (0, 0)),
        ],
        out_specs=[
            pl.BlockSpec((1, tm, D_MODEL), lambda b, i: (b, i, 0)),
            pl.BlockSpec((1, tm, D_MODEL), lambda b, i: (b, i, 0)),
        ],
        out_shape=[
            jax.ShapeDtypeStruct((B, L, D_MODEL), F32),
            jax.ShapeDtypeStruct((B, L, D_MODEL), BF16),
        ],
        compiler_params=_cparams(("arbitrary", "arbitrary")),
        name="mix_out",
    )(xa, xb, ret, hy_o, pool_o, wr, wh, wp, g)


def _ffn_kernel(h_ref, hp_ref, hn_ref, x1_ref, wup_ref, cw_ref, wd_ref, pa_ref, pb_ref, pw_ref, pg_ref,
                pn_ref, fin_ref, *o_refs, nbp, final):
    tm = h_ref.shape[1]
    i = pl.program_id(1)
    nt = pl.num_programs(1)

    hx = jnp.concatenate([h_ref[0], hp_ref[0], hn_ref[0]], axis=0)
    row = lax.broadcasted_iota(jnp.int32, (tm, FFN_SUB), 0)
    n_sub = D_FF // FFN_SUB

    def up_proj(s):
        c0 = s * FFN_SUB
        gate = jnp.dot(hx, wup_ref[:, c0:c0 + FFN_SUB], preferred_element_type=F32)
        up = jnp.dot(hx[:tm], wup_ref[:, D_FF + c0:D_FF + c0 + FFN_SUB], preferred_element_type=F32)
        return gate, up

    acc = x1_ref[0]
    nxt = up_proj(0)
    group, g0 = [], 0
    for s in range(n_sub):
        c0 = s * FFN_SUB
        gx, up = nxt
        if s + 1 < n_sub:
            nxt = up_proj(s + 1)
        gate = gx[:tm]
        g_prev = jnp.where(i == 0, 0.0, gx[tm + BF16_ROWS - 1:tm + BF16_ROWS])
        g_next = jnp.where(i == nt - 1, 0.0, gx[tm + BF16_ROWS:tm + BF16_ROWS + 1])
        g_up = jnp.where(row == 0, g_prev, pltpu.roll(gate, 1, axis=0))
        g_dn = jnp.where(row == tm - 1, g_next, pltpu.roll(gate, tm - 1, axis=0))
        cw = cw_ref[:, c0:c0 + FFN_SUB]
        gc = g_up * cw[0:1] + gate * cw[1:2] + g_dn * cw[2:3]
        group.append((jax.nn.gelu(gc) * up).astype(BF16))
        if len(group) == FFN_DOWN_GROUP or s == n_sub - 1:
            a = jnp.concatenate(group, axis=1)
            acc = acc + jnp.dot(a, wd_ref[g0:c0 + FFN_SUB, :], preferred_element_type=F32)
            group, g0 = [], c0 + FFN_SUB

    x2 = acc
    p = _pick(nbp, pa_ref, pb_ref).astype(BF16)
    e = _rms(jnp.dot(p, pw_ref[...], preferred_element_type=F32), pn_ref[...])
    gt = jax.nn.sigmoid(jnp.dot(x2.astype(BF16), pg_ref[...], preferred_element_type=F32))
    x3 = x2 + gt * e
    if not final:
        o_refs[0][0] = x3
        return
    y = _rms(x3, fin_ref[...])
    b = pl.program_id(0)

    @pl.when(b < nbp)
    def _():
        o_refs[0][0] = y

    @pl.when(b >= nbp)
    def _():
        o_refs[1][0] = y


def _ffn(h2, x1, w_up, cw, w_down, p, layer, ple_w, ple_gate, ple_norm, fin, tm, final):
    pa, pb, nbp, off = p
    B, L, _ = x1.shape
    nt = L // tm
    halo = BF16_ROWS
    nh = L // halo
    per = tm // halo
    full = lambda shape: pl.BlockSpec(shape, lambda b, i: (0,) * len(shape), pipeline_mode=pl.Buffered(1))
    tile = pl.BlockSpec((1, tm, D_MODEL), lambda b, i: (b, i, 0))
    if final:
        out_specs = list(_pair_specs(tm, D_MODEL, nt, nbp, 0))
        out_shape = [jax.ShapeDtypeStruct((nbp, L, D_MODEL), F32), jax.ShapeDtypeStruct((B - nbp, L, D_MODEL), F32)]
    else:
        out_specs = [tile]
        out_shape = [jax.ShapeDtypeStruct((B, L, D_MODEL), F32)]
    return pl.pallas_call(
        functools.partial(_ffn_kernel, nbp=nbp, final=final),
        grid=(B, nt),
        in_specs=[
            tile,
            pl.BlockSpec((1, halo, D_MODEL), lambda b, i: (b, jnp.maximum(i * per - 1, 0), 0)),
            pl.BlockSpec((1, halo, D_MODEL), lambda b, i: (b, jnp.minimum((i + 1) * per, nh - 1), 0)),
            tile,
            full((D_MODEL, 2 * D_FF)), full((3, D_FF)), full((D_FF, D_MODEL)),
            *_pair_specs(tm, PLE_DIM, nt, nbp, off, lead=layer),
            full((PLE_DIM, D_MODEL)), full((D_MODEL, D_MODEL)), full((1, D_MODEL)), full((1, D_MODEL)),
        ],
        out_specs=out_specs,
        out_shape=out_shape,
        compiler_params=_cparams(("arbitrary", "arbitrary")),
        name="ffn_ple",
    )(h2, h2, h2, x1, w_up, cw, w_down, pa, pb, ple_w, ple_gate, ple_norm, fin)


def _rope_tables(L):
    half = HEAD_DIM // 2
    inv = ROPE_THETA ** (-jnp.arange(half, dtype=F32) / half)
    ang = jnp.arange(L, dtype=F32)[:, None] * inv[None, :]
    cos, sin = jnp.cos(ang), jnp.sin(ang)
    return jnp.concatenate([cos, cos], axis=-1), jnp.concatenate([-sin, sin], axis=-1)


def _block_diag(pool_w):
    out = jnp.zeros((POOL_WIDTH, POOL_WIDTH), pool_w.dtype)
    for g in range(len(POOL_WINDOWS)):
        sl = slice(g * POOL_GROUP_DIM, (g + 1) * POOL_GROUP_DIM)
        out = out.at[sl, sl].set(pool_w[g])
    return out


def _layer(x, B, p, layer, lw, cos, sin, fin, tm, final):
    (norm_mix, w_in, dec_f, dec_b, ret_gn, hy_conv, hy_w1, hy_b1, hy_freq, hy_w2, hy_b2, hy_w3, hy_bias,
     pool_w, pool_scale, w_out, norm_ffn, w_up, ffn_conv, w_down, ple_w, ple_gate, ple_norm) = lw
    L = x[0].shape[1]
    R = RET_WIDTH

    w_q, w_k, w_v, w_g, w_hy, w_pool = jnp.split(w_in, [R, 2 * R, 3 * R, 4 * R, 4 * R + HY_IN_WIDTH], axis=1)
    wa = jnp.concatenate([w_q, w_v, w_g, w_pool], axis=1).astype(BF16)
    qvg, kt, hy_t, pool_in = _in_proj(x, B, norm_mix[None], wa, w_k.T.astype(BF16), w_hy.astype(BF16), hy_conv,
                                      WIDE_TILE * tm)

    dec = jnp.broadcast_to(jnp.stack([dec_f, dec_b], axis=1)[:, :, None], (RET_HEADS, 2, LANES)).astype(F32)
    ret = _retention(qvg, kt, dec, cos, sin, cos.T, sin.T, ret_gn[None])

    w1t = jnp.zeros((HY_HIDDEN, HY_HIDDEN), F32).at[:, :hy_w1.shape[0]].set(hy_w1.T)
    w3t = hy_w3.T.reshape(2, 2, HY_WIDTH, HY_HIDDEN)
    kk = _hyena_filters(w1t, hy_b1[:, None], hy_freq[:, None], hy_w2.T, hy_b2[:, None], w3t, L)
    bias = jnp.broadcast_to(hy_bias.reshape(2, HY_WIDTH // SUBLANES, SUBLANES).transpose(1, 0, 2)[..., None],
                            (HY_WIDTH // SUBLANES, 2, SUBLANES, HY_BLOCK))
    hy_o = _hyena(kk, hy_t, bias, B)

    pool_o = _pool(pool_in, _block_diag(pool_w).astype(BF16), pool_scale[None])

    wo = w_out.astype(BF16)
    x1, h2 = _mix_out(x, ret, hy_o, pool_o, wo[:R], wo[R:R + HY_WIDTH], wo[R + HY_WIDTH:], norm_ffn[None],
                      WIDE_TILE * tm)

    return _ffn(h2, x1, w_up.astype(BF16), ffn_conv, w_down.astype(BF16), p, layer, ple_w.astype(BF16),
                ple_gate.astype(BF16), ple_norm[None], fin, tm, final)


def _trunk(x_a, x_b, p_a, p_b, layer_weights, norm_final, tm):
    depth = p_a.shape[0]
    nbp, L = x_a.shape[0], x_a.shape[1]
    B = nbp + x_b.shape[0]
    cos, sin = _rope_tables(L)
    fin = norm_final[None]
    x = (x_a, x_b, nbp, 0)
    p = (p_a, p_b, nbp, 0)
    for i in range(depth):
        out = _layer(x, B, p, i, [w[i] for w in layer_weights], cos, sin, fin, tm, i == depth - 1)
        x = (out[0], out[0], nbp, nbp)
    return out[0], out[1]


def kernel(x_prompt, x_sample, p_prompt, p_sample, norm_mix, w_in, ret_decay_fwd, ret_decay_bwd, ret_gn,
           hy_short_conv, hy_w1, hy_b1, hy_freq, hy_w2, hy_b2, hy_w3, hy_bias, pool_w, pool_scale, w_out,
           norm_ffn, ffn_w_up, ffn_conv, ffn_w_down, ple_w, ple_gate_w, ple_norm, norm_final):
    layer_weights = (norm_mix, w_in, ret_decay_fwd, ret_decay_bwd, ret_gn, hy_short_conv, hy_w1, hy_b1,
                     hy_freq, hy_w2, hy_b2, hy_w3, hy_bias, pool_w, pool_scale, w_out, norm_ffn, ffn_w_up,
                     ffn_conv, ffn_w_down, ple_w, ple_gate_w, ple_norm)
    return _trunk(x_prompt, x_sample, p_prompt, p_sample, layer_weights, norm_final, ROW_TILE)
```

```python
import functools
import math

import jax
import jax.numpy as jnp
from jax import lax
from jax.experimental import pallas as pl
from jax.experimental.pallas import tpu as pltpu

F32 = jnp.float32
BF16 = jnp.bfloat16

D_MODEL = 1024
RET_WIDTH = 512
RET_HEADS = 4
HEAD_DIM = 128
HY_WIDTH = 256
POOL_WIDTH = 256
POOL_GROUP_DIM = 64
POOL_WINDOWS = (2, 4, 8, 16)
POOL_PAD = 16
D_FF = 2816
PLE_DIM = 256
CHUNK = 128
ROPE_THETA = 10000.0
HY_EMB_BANDS = 16
HY_HIDDEN = 64
HY_MIN_DECAY = math.log(1e-2) / 1.5
HY_MAX_DECAY = math.log(1e-2) / 0.3
EPS = 1e-6

LANES = 128
SUBLANES = 8
BF16_ROWS = 16
MXU_DIM = 256
VMEM_LIMIT = 56 * 1024 * 1024

HY_BLOCK = MXU_DIM
HY_MAX_BLOCKS = 8
HY_PAIR_UNROLL = 1
QVG_WIDTH = 3 * RET_WIDTH
RET_UNROLL = True
HY_IN_WIDTH = 3 * HY_WIDTH
ROW_TILE = 512
WIDE_TILE = 2
FFN_SUB = MXU_DIM
FFN_DOWN_GROUP = 4


def _cparams(sem):
    return pltpu.CompilerParams(dimension_semantics=sem, vmem_limit_bytes=VMEM_LIMIT)


def _rms(x, g):
    return x * lax.rsqrt(jnp.mean(x * x, axis=-1, keepdims=True) + EPS) * g


def _pair_specs(tm, width, nt, nbp, off, lead=None):
    pre = () if lead is None else (lead,)
    shape = (1, tm, width) if lead is None else (None, 1, tm, width)

    def first(b, i):
        return pre + (jnp.minimum(b, nbp - 1), jnp.where(b < nbp, i, nt - 1), 0)

    def second(b, i):
        return pre + (jnp.maximum(b - nbp, 0) + off, jnp.where(b < nbp, 0, i), 0)

    return pl.BlockSpec(shape, first), pl.BlockSpec(shape, second)


def _pick(nbp, a_ref, b_ref):
    return jnp.where(pl.program_id(0) < nbp, a_ref[0], b_ref[0])


def _pair_halo_specs(rows, width, tm, L, nbp, off):
    per, last = tm // rows, L // rows - 1

    def make(first, after):
        def index(b, i):
            r = jnp.minimum((i + 1) * per, last) if after else jnp.maximum(i * per - 1, 0)
            if first:
                return (jnp.minimum(b, nbp - 1), jnp.where(b < nbp, r, 0), 0)
            return (jnp.maximum(b - nbp, 0) + off, jnp.where(b < nbp, 0, r), 0)
        return pl.BlockSpec((1, rows, width), index)

    return make(True, False), make(False, False), make(True, True), make(False, True)


def _dwconv3_rows(u, tm, cw, first_tile, last_tile):
    main = u[:tm]
    prev = jnp.where(first_tile, 0.0, u[tm + BF16_ROWS - 1:tm + BF16_ROWS])
    nxt = jnp.where(last_tile, 0.0, u[tm + BF16_ROWS:tm + BF16_ROWS + 1])
    row = lax.broadcasted_iota(jnp.int32, main.shape, 0)
    up = jnp.where(row == 0, prev, pltpu.roll(main, 1, axis=0))
    dn = jnp.where(row == tm - 1, nxt, pltpu.roll(main, tm - 1, axis=0))
    return up * cw[0:1] + main * cw[1:2] + dn * cw[2:3]


def _in_proj_kernel(xa_ref, xb_ref, xpa_ref, xpb_ref, xna_ref, xnb_ref, g_ref, wa_ref, wk_ref, wh_ref, cw_ref,
                    cos_ref, sin_ref, cost_ref, sint_ref, qvg_ref, kt_ref, hy_ref, pool_ref, *, nbp):
    tm = xa_ref.shape[1]
    i = pl.program_id(1)
    g = g_ref[...]
    h = _rms(_pick(nbp, xa_ref, xb_ref), g).astype(BF16)
    hx = jnp.concatenate([h, _rms(_pick(nbp, xpa_ref, xpb_ref), g).astype(BF16),
                          _rms(_pick(nbp, xna_ref, xnb_ref), g).astype(BF16)], axis=0)
    u = jnp.dot(hx, wh_ref[...], preferred_element_type=F32)
    y = _dwconv3_rows(u, tm, cw_ref[...], i == 0, i == pl.num_programs(1) - 1)
    yt = y.T
    for j in range(tm // LANES):
        hy_ref[:, j, :, :] = yt[:, j * LANES:(j + 1) * LANES].reshape(HY_IN_WIDTH // SUBLANES, SUBLANES, LANES)
    kt = lax.dot_general(wk_ref[...], h, (((1,), (1,)), ((), ())), preferred_element_type=F32)
    cos_t, sin_t = cost_ref[...], sint_ref[...]
    for hd in range(RET_HEADS):
        kh = kt[hd * HEAD_DIM:(hd + 1) * HEAD_DIM]
        kt_ref[0, hd * HEAD_DIM:(hd + 1) * HEAD_DIM, :] = (
            kh * cos_t + pltpu.roll(kh, HEAD_DIM // 2, axis=0) * sin_t).astype(BF16)
    a = jnp.dot(h, wa_ref[...], preferred_element_type=F32)
    cos, sin = cos_ref[...], sin_ref[...]
    for hd in range(RET_HEADS):
        qh = a[:, hd * HEAD_DIM:(hd + 1) * HEAD_DIM]
        qvg_ref[0, :, hd * HEAD_DIM:(hd + 1) * HEAD_DIM] = (
            qh * cos + pltpu.roll(qh, HEAD_DIM // 2, axis=1) * sin).astype(BF16)
    qvg_ref[0, :, RET_WIDTH:] = a[:, RET_WIDTH:QVG_WIDTH].astype(BF16)
    pool_ref[0] = a[:, QVG_WIDTH:].astype(BF16)


def _in_proj(x, B, g, wa, wk_t, wh, cw, cos, sin, cos_t, sin_t, tm):
    xa, xb, nbp, off = x
    L = xa.shape[1]
    nt = L // tm
    const = lambda shape: pl.BlockSpec(shape, lambda b, i: (0,) * len(shape))
    return pl.pallas_call(
        functools.partial(_in_proj_kernel, nbp=nbp),
        grid=(B, nt),
        in_specs=[
            *_pair_specs(tm, D_MODEL, nt, nbp, off),
            *_pair_halo_specs(BF16_ROWS, D_MODEL, tm, L, nbp, off),
            const((1, D_MODEL)),
            const((D_MODEL, QVG_WIDTH + POOL_WIDTH)),
            const((RET_WIDTH, D_MODEL)),
            const((D_MODEL, HY_IN_WIDTH)),
            const((3, HY_IN_WIDTH)),
            pl.BlockSpec((tm, HEAD_DIM), lambda b, i: (i, 0)),
            pl.BlockSpec((tm, HEAD_DIM), lambda b, i: (i, 0)),
            pl.BlockSpec((HEAD_DIM, tm), lambda b, i: (0, i)),
            pl.BlockSpec((HEAD_DIM, tm), lambda b, i: (0, i)),
        ],
        out_specs=[
            pl.BlockSpec((1, tm, QVG_WIDTH), lambda b, i: (b, i, 0)),
            pl.BlockSpec((1, RET_WIDTH, tm), lambda b, i: (b, 0, i)),
            pl.BlockSpec((HY_IN_WIDTH // SUBLANES, tm // LANES, SUBLANES, LANES), lambda b, i: (0, i, b, 0)),
            pl.BlockSpec((1, tm, POOL_WIDTH), lambda b, i: (b, i, 0)),
        ],
        out_shape=[
            jax.ShapeDtypeStruct((B, L, QVG_WIDTH), BF16),
            jax.ShapeDtypeStruct((B, RET_WIDTH, L), BF16),
            jax.ShapeDtypeStruct((HY_IN_WIDTH // SUBLANES, L // LANES, B * SUBLANES, LANES), F32),
            jax.ShapeDtypeStruct((B, L, POOL_WIDTH), BF16),
        ],
        compiler_params=_cparams(("arbitrary", "arbitrary")),
        name="in_proj",
    )(xa, xb, xa, xb, xa, xb, g, wa, wk_t, wh, cw, cos, sin, cos_t, sin_t)


def _log_sigmoid(x):
    return jnp.minimum(x, 0.0) - jnp.log(1.0 + jnp.exp(-jnp.abs(x)))


def _retention_kernel(dec_ref, gn_ref, q_ref, kt_ref, v_ref, g_ref, o_ref, kv_s, st_s, tab_s):
    L = q_ref.shape[1]
    n_chunks = L // CHUNK
    C = CHUNK
    scale = HEAD_DIM ** -0.5

    lg = _log_sigmoid(dec_ref[0])
    lgf, lgb = lg[0:1], lg[1:2]
    ri = lax.broadcasted_iota(jnp.int32, (C, C), 0)
    ci = lax.broadcasted_iota(jnp.int32, (C, C), 1)
    rf = ri.astype(F32)
    cf = ci.astype(F32)
    dist = jnp.abs(ri - ci).astype(F32)
    tab_s[0] = jnp.exp(dist * jnp.where(ri >= ci, lgf, lgb)) * scale
    tab_s[1] = jnp.exp((rf + 1.0) * lgf) * scale
    tab_s[2] = jnp.exp((C - rf) * lgb) * scale
    tab_s[3] = jnp.exp((C - 1.0 - cf) * lgf)
    tab_s[4] = jnp.exp(cf * lgb)
    dec_f = jnp.exp(C * lgf)
    dec_b = jnp.exp(C * lgb)

    def summaries(n, carry):
        c0 = pl.multiple_of(n * C, C)
        kt = kt_ref[0, :, pl.ds(c0, C)].astype(F32)
        lhs = jnp.concatenate([kt * tab_s[3], kt * tab_s[4]], axis=0).astype(BF16)
        kv_s[n] = jnp.dot(lhs, v_ref[0, pl.ds(c0, C), :], preferred_element_type=F32)
        return carry

    lax.fori_loop(0, n_chunks, summaries, 0, unroll=RET_UNROLL)

    def scan(i, carry):
        sf, sb = carry
        m = n_chunks - 1 - i
        st_s[i, :C, :] = sf.astype(BF16)
        st_s[m, C:, :] = sb.astype(BF16)
        return sf * dec_f + kv_s[i, :C, :], sb * dec_b + kv_s[m, C:, :]

    zero = jnp.zeros((C, C), F32)
    lax.fori_loop(0, n_chunks, scan, (zero, zero), unroll=RET_UNROLL)

    gain = gn_ref[...]

    def outputs(n, carry):
        c0 = pl.multiple_of(n * C, C)
        qb = q_ref[0, pl.ds(c0, C), :]
        qn = qb.astype(F32)
        vn = v_ref[0, pl.ds(c0, C), :]
        sc = jnp.dot(qb, kt_ref[0, :, pl.ds(c0, C)], preferred_element_type=F32)
        lhs = jnp.concatenate([sc * tab_s[0], qn * tab_s[1], qn * tab_s[2]], axis=1).astype(BF16)
        rhs = jnp.concatenate([vn, st_s[n]], axis=0)
        o = jnp.dot(lhs, rhs, preferred_element_type=F32)
        mu = jnp.mean(o, axis=-1, keepdims=True)
        oc = o - mu
        var = jnp.mean(oc * oc, axis=-1, keepdims=True)
        gate = g_ref[0, pl.ds(c0, C), :].astype(F32)
        y = oc * lax.rsqrt(var + EPS) * gain * (gate * jax.nn.sigmoid(gate))
        o_ref[0, pl.ds(c0, C), :] = y.astype(BF16)
        return carry

    lax.fori_loop(0, n_chunks, outputs, 0, unroll=RET_UNROLL)


def _retention(qvg, kt, dec, gn):
    B, L, _ = qvg.shape
    H = RET_HEADS

    def col(off):
        return pl.BlockSpec((1, L, HEAD_DIM), lambda b, h: (b, 0, off + h))

    return pl.pallas_call(
        _retention_kernel,
        grid=(B, H),
        in_specs=[
            pl.BlockSpec((1, 2, LANES), lambda b, h: (h, 0, 0)),
            pl.BlockSpec((1, HEAD_DIM), lambda b, h: (0, h)),
            col(0),
            pl.BlockSpec((1, HEAD_DIM, L), lambda b, h: (b, h, 0)),
            col(H), col(2 * H),
        ],
        out_specs=pl.BlockSpec((1, L, HEAD_DIM), lambda b, h: (b, 0, h)),
        out_shape=jax.ShapeDtypeStruct((B, L, RET_WIDTH), BF16),
        scratch_shapes=[
            pltpu.VMEM((L // CHUNK, 2 * CHUNK, CHUNK), F32),
            pltpu.VMEM((L // CHUNK, 2 * CHUNK, CHUNK), BF16),
            pltpu.VMEM((5, CHUNK, CHUNK), F32),
        ],
        compiler_params=_cparams(("parallel", "parallel")),
        name="retention",
    )(dec, gn, qvg, kt, qvg, qvg)


def _filter_kernel(w1_ref, b1_ref, fr_ref, w2_ref, b2_ref, w3_ref, kk_ref, hid_s):
    L = kk_ref.shape[1] // 2
    hi = lax.Precision.HIGHEST

    @pl.when(pl.program_id(0) == 0)
    def _():
        r = lax.broadcasted_iota(jnp.int32, (HY_HIDDEN, L), 0)
        lane = lax.broadcasted_iota(jnp.int32, (HY_HIDDEN, L), 1)
        band_idx = jnp.where(r <= HY_EMB_BANDS, r - 1, r - 1 - HY_EMB_BANDS).astype(F32)
        band = 1e-4 + band_idx * ((HY_EMB_BANDS - 1 - 1e-4) / (HY_EMB_BANDS - 1))
        fr = fr_ref[...]

        def hidden(s_int):
            s = s_int.astype(F32)
            t = s / (L - 1.0)
            ang = (2.0 * math.pi / L) * s * band
            feats = jnp.where(r == 0, t,
                              jnp.where(r <= HY_EMB_BANDS, jnp.cos(ang),
                                        jnp.where(r <= 2 * HY_EMB_BANDS, -jnp.sin(ang), 0.0)))
            h = jnp.sin(fr * (jnp.dot(w1_ref[...], feats, precision=hi, preferred_element_type=F32) + b1_ref[...]))
            return jnp.sin(fr * (jnp.dot(w2_ref[...], h, precision=hi, preferred_element_type=F32) + b2_ref[...]))

        hid_s[0] = hidden(lane)
        hid_s[1] = hidden(L - lane)

    crow = lax.broadcasted_iota(jnp.int32, (HY_WIDTH, L), 0).astype(F32)
    delta = jnp.abs(HY_MIN_DECAY + crow * ((HY_MAX_DECAY - HY_MIN_DECAY) / (HY_WIDTH - 1)))
    clane = lax.broadcasted_iota(jnp.int32, (HY_WIDTH, L), 1)

    def taps(hid, s_row_int, w3):
        h = jnp.dot(w3, hid, precision=hi, preferred_element_type=F32)
        return h * jnp.exp(-(s_row_int.astype(F32) / (L - 1.0)) * delta)

    h_f = taps(hid_s[0], clane, w3_ref[0, 0])
    h_b = taps(hid_s[1], L - clane, w3_ref[0, 1])
    h_b = jnp.where(clane == 0, 0.0, h_b)
    norm = jnp.sum(jnp.abs(h_f), axis=-1, keepdims=True) + jnp.sum(jnp.abs(h_b), axis=-1, keepdims=True)
    inv = 1.0 / norm
    kk = jnp.concatenate([h_b * inv, h_f * inv], axis=1)
    bits = pltpu.bitcast(kk, jnp.uint32)
    bf = (bits + jnp.uint32(0x7FFF) + ((bits >> 16) & jnp.uint32(1))) >> 16
    lane2 = lax.broadcasted_iota(jnp.int32, bf.shape, 1)
    prev = jnp.where(lane2 == 0, jnp.uint32(0), pltpu.roll(bf, 1, axis=1))
    kk_ref[...] = bf | (prev << 16)


def _hyena_filters(w1t, b1, fr, w2t, b2, w3t, L):
    small = lambda shape: pl.BlockSpec(shape, lambda o: (0,) * len(shape))
    return pl.pallas_call(
        _filter_kernel,
        grid=(2,),
        in_specs=[
            small((HY_HIDDEN, HY_HIDDEN)), small((HY_HIDDEN, 1)), small((HY_HIDDEN, 1)),
            small((HY_HIDDEN, HY_HIDDEN)), small((HY_HIDDEN, 1)),
            pl.BlockSpec((1, 2, HY_WIDTH, HY_HIDDEN), lambda o: (o, 0, 0, 0)),
        ],
        out_specs=pl.BlockSpec((HY_WIDTH, 2 * L), lambda o: (o, 0)),
        out_shape=jax.ShapeDtypeStruct((2 * HY_WIDTH, 2 * L), jnp.uint32),
        scratch_shapes=[pltpu.VMEM((2, HY_HIDDEN, L), F32)],
        compiler_params=_cparams(("arbitrary",)),
        name="hyena_filters",
    )(w1t, b1, fr, w2t, b2, w3t)


def _hyena_kernel(kk1_ref, kk2_ref, xv_ref, x1_ref, x2_ref, bias_ref, out_ref,
                  toep1_s, toep2_s, u2_s, x1u2_s, y2_s, *, B, BP):
    n_tiles = xv_ref.shape[0]
    L = n_tiles * LANES
    T = HY_BLOCK
    NB = L // T
    TPB = T // LANES

    @pl.when(pl.program_id(0) == 0)
    def _():
        toep1_s[...] = jnp.zeros_like(toep1_s)
        toep2_s[...] = jnp.zeros_like(toep2_s)
        u2_s[...] = jnp.zeros_like(u2_s)
        x1u2_s[...] = jnp.zeros_like(x1u2_s)

    def build_toeplitz(kk_ref, c, toep_s):
        row = jnp.broadcast_to(kk_ref[pl.ds(c, 1), :], (SUBLANES, 2 * L))
        for g in range(LANES // BF16_ROWS):
            rolled = pltpu.roll(row, BF16_ROWS * g, axis=1, stride=2, stride_axis=0)
            tile = pltpu.bitcast(rolled, BF16)
            toep_s[BF16_ROWS * g:BF16_ROWS * (g + 1), :] = tile
            toep_s[LANES + BF16_ROWS * g:LANES + BF16_ROWS * (g + 1), LANES:] = tile[:, :2 * L - LANES]

    def long_conv(toep_s, u_s, y_s):
        y_s[...] = jnp.zeros_like(y_s)
        for d in range(-(NB - 1), NB):
            n = NB - abs(d)
            src = max(0, -d) * BP
            dst = max(0, d) * BP
            m = toep_s[:, L + d * T:L + (d + 1) * T]
            for c0 in range(0, n, HY_MAX_BLOCKS):
                nn = min(HY_MAX_BLOCKS, n - c0)
                lhs = u_s[src + c0 * BP:src + (c0 + nn) * BP, :].astype(BF16)
                y_s[dst + c0 * BP:dst + (c0 + nn) * BP, :] += jnp.dot(lhs, m, preferred_element_type=F32)

    def channel(c, k):
        u_s, x1u_s, y_s = u2_s.at[k], x1u2_s.at[k], y2_s.at[k]
        rows = pl.ds(c, B, stride=SUBLANES)
        for j in range(n_tiles):
            J, l0 = j // TPB, (j % TPB) * LANES
            u_s[J * BP:J * BP + B, l0:l0 + LANES] = xv_ref.at[j][rows, :]
            x1u_s[J * BP:J * BP + B, l0:l0 + LANES] = x1_ref.at[j][rows, :]
        b0 = bias_ref[0, 0, pl.ds(c, 1), :]
        b1 = bias_ref[0, 1, pl.ds(c, 1), :]
        build_toeplitz(kk2_ref, c, toep2_s)
        long_conv(toep1_s, u_s, y_s)
        u_s[...] = x1u_s[...] * (y_s[...] + u_s[...] * b0)
        build_toeplitz(kk1_ref, jnp.minimum(c + 1, SUBLANES - 1), toep1_s)
        long_conv(toep2_s, u_s, y_s)
        for j in range(n_tiles):
            J, l0 = j // TPB, (j % TPB) * LANES
            yy = y_s[J * BP:J * BP + B, l0:l0 + LANES] + u_s[J * BP:J * BP + B, l0:l0 + LANES] * b1[:, :LANES]
            out_ref.at[j][rows, :] = x2_ref.at[j][rows, :] * yy

    def channel_pair(i, carry):
        channel(2 * i, 0)
        channel(2 * i + 1, 1)
        return carry

    build_toeplitz(kk1_ref, 0, toep1_s)
    lax.fori_loop(0, SUBLANES // 2, channel_pair, 0, unroll=HY_PAIR_UNROLL)


def _hyena(kk, hy_t, bias, B):
    n_in, n_tiles, R, _ = hy_t.shape
    L = n_tiles * LANES
    NB = L // HY_BLOCK
    BP = -(-B // SUBLANES) * SUBLANES
    n_c = HY_WIDTH // SUBLANES

    def sect(s):
        return pl.BlockSpec((None, n_tiles, R, LANES), lambda c: (s * n_c + c, 0, 0, 0))

    return pl.pallas_call(
        functools.partial(_hyena_kernel, B=B, BP=BP),
        grid=(n_c,),
        in_specs=[
            pl.BlockSpec((SUBLANES, 2 * L), lambda c: (c, 0)),
            pl.BlockSpec((SUBLANES, 2 * L), lambda c: (n_c + c, 0)),
            sect(0), sect(1), sect(2),
            pl.BlockSpec((1, 2, SUBLANES, HY_BLOCK), lambda c: (c, 0, 0, 0)),
        ],
        out_specs=pl.BlockSpec((None, n_tiles, R, LANES), lambda c: (c, 0, 0, 0)),
        out_shape=jax.ShapeDtypeStruct((n_c, n_tiles, R, LANES), F32),
        scratch_shapes=[
            pltpu.VMEM((2 * LANES, 2 * L), BF16), pltpu.VMEM((2 * LANES, 2 * L), BF16),
            pltpu.VMEM((2, NB * BP, HY_BLOCK), F32),
            pltpu.VMEM((2, NB * BP, HY_BLOCK), F32),
            pltpu.VMEM((2, NB * BP, HY_BLOCK), F32),
        ],
        compiler_params=_cparams(("arbitrary",)),
        name="hyena",
    )(kk, kk, hy_t, hy_t, hy_t, bias)


def _pool_kernel(u_ref, w_ref, sc_ref, o_ref):
    L = u_ref.shape[1]
    u = u_ref[0].astype(F32)
    t = lax.broadcasted_iota(jnp.int32, (L, POOL_WIDTH), 0)
    lane = lax.broadcasted_iota(jnp.int32, (L, POOL_WIDTH), 1)
    pad = jnp.zeros((POOL_PAD, POOL_WIDTH), F32)
    up = jnp.concatenate([pad, u, pad], axis=0)
    n = L + 2 * POOL_PAD

    def both(x, k):
        return pltpu.roll(x, k, axis=0) + pltpu.roll(x, n - k, axis=0)

    q2 = pltpu.roll(up, 1, axis=0) + up
    q4 = both(q2, 1)
    q8 = both(q4, 2)
    q16 = both(q8, 4)
    p2, p4, p8, p16 = (q[POOL_PAD:POOL_PAD + L] for q in (q2, q4, q8, q16))
    g = lane // POOL_GROUP_DIM
    half = jnp.where(g == 0, 1, jnp.where(g == 1, 2, jnp.where(g == 2, 4, 8)))
    total = jnp.where(g == 0, p2, jnp.where(g == 1, p4, jnp.where(g == 2, p8, p16)))
    lo = jnp.maximum(t - half, 0)
    hi = jnp.minimum(t + half - 1, L - 1)
    cnt = (hi - lo + 1).astype(F32)
    d = (total / cnt - u).astype(BF16)
    o_ref[0] = (jnp.dot(d, w_ref[...], preferred_element_type=F32) * sc_ref[...]).astype(BF16)


def _pool(pool_in, w_bd, scale):
    B, L, _ = pool_in.shape
    return pl.pallas_call(
        _pool_kernel,
        grid=(B,),
        in_specs=[
            pl.BlockSpec((1, L, POOL_WIDTH), lambda b: (b, 0, 0)),
            pl.BlockSpec((POOL_WIDTH, POOL_WIDTH), lambda b: (0, 0)),
            pl.BlockSpec((1, POOL_WIDTH), lambda b: (0, 0)),
        ],
        out_specs=pl.BlockSpec((1, L, POOL_WIDTH), lambda b: (b, 0, 0)),
        out_shape=jax.ShapeDtypeStruct((B, L, POOL_WIDTH), BF16),
        compiler_params=_cparams(("parallel",)),
        name="pool_mixer",
    )(pool_in, w_bd, scale)


def _mix_out_kernel(xa_ref, xb_ref, ret_ref, hy_ref, pool_ref, wr_ref, wh_ref, wp_ref, g_ref, x1_ref, h2_ref, *, nbp):
    tm = xa_ref.shape[1]
    hy_t = jnp.concatenate([hy_ref[:, j].reshape(HY_WIDTH, LANES) for j in range(tm // LANES)],
                           axis=1).astype(BF16)
    mix = jnp.dot(ret_ref[0], wr_ref[...], preferred_element_type=F32)
    mix += lax.dot_general(hy_t, wh_ref[...], (((0,), (0,)), ((), ())), preferred_element_type=F32)
    mix += jnp.dot(pool_ref[0], wp_ref[...], preferred_element_type=F32)
    x1 = _pick(nbp, xa_ref, xb_ref) + mix
    x1_ref[0] = x1
    h2_ref[0] = _rms(x1, g_ref[...]).astype(BF16)


def _mix_out(x, ret, hy_o, pool_o, wr, wh, wp, g, tm):
    xa, xb, nbp, off = x
    B, L, _ = ret.shape
    nt = L // tm
    return pl.pallas_call(
        functools.partial(_mix_out_kernel, nbp=nbp),
        grid=(B, nt),
        in_specs=[
            *_pair_specs(tm, D_MODEL, nt, nbp, off),
            pl.BlockSpec((1, tm, RET_WIDTH), lambda b, i: (b, i, 0)),
            pl.BlockSpec((HY_WIDTH // SUBLANES, tm // LANES, SUBLANES, LANES), lambda b, i: (0, i, b, 0)),
            pl.BlockSpec((1, tm, POOL_WIDTH), lambda b, i: (b, i, 0)),
            pl.BlockSpec((RET_WIDTH, D_MODEL), lambda b, i: (0, 0)),
            pl.BlockSpec((HY_WIDTH, D_MODEL), lambda b, i: (0, 0)),
            pl.BlockSpec((POOL_WIDTH, D_MODEL), lambda b, i: (0, 0)),
            pl.BlockSpec((1, D_MODEL), lambda b, i: (0, 0)),
        ],
        out_specs=[
            pl.BlockSpec((1, tm, D_MODEL), lambda b, i: (b, i, 0)),
            pl.BlockSpec((1, tm, D_MODEL), lambda b, i: (b, i, 0)),
        ],
        out_shape=[
            jax.ShapeDtypeStruct((B, L, D_MODEL), F32),
            jax.ShapeDtypeStruct((B, L, D_MODEL), BF16),
        ],
        compiler_params=_cparams(("arbitrary", "arbitrary")),
        name="mix_out",
    )(xa, xb, ret, hy_o, pool_o, wr, wh, wp, g)


def _ffn_kernel(h_ref, hp_ref, hn_ref, x1_ref, wup_ref, cw_ref, wd_ref, pa_ref, pb_ref, pw_ref, pg_ref,
                pn_ref, fin_ref, *o_refs, nbp, final):
    tm = h_ref.shape[1]
    i = pl.program_id(1)
    nt = pl.num_programs(1)

    hx = jnp.concatenate([h_ref[0], hp_ref[0], hn_ref[0]], axis=0)
    row = lax.broadcasted_iota(jnp.int32, (tm, FFN_SUB), 0)
    n_sub = D_FF // FFN_SUB

    def up_proj(s):
        c0 = s * FFN_SUB
        gate = jnp.dot(hx, wup_ref[:, c0:c0 + FFN_SUB], preferred_element_type=F32)
        up = jnp.dot(hx[:tm], wup_ref[:, D_FF + c0:D_FF + c0 + FFN_SUB], preferred_element_type=F32)
        return gate, up

    acc = x1_ref[0]
    nxt = up_proj(0)
    group, g0 = [], 0
    for s in range(n_sub):
        c0 = s * FFN_SUB
        gx, up = nxt
        if s + 1 < n_sub:
            nxt = up_proj(s + 1)
        gate = gx[:tm]
        g_prev = jnp.where(i == 0, 0.0, gx[tm + BF16_ROWS - 1:tm + BF16_ROWS])
        g_next = jnp.where(i == nt - 1, 0.0, gx[tm + BF16_ROWS:tm + BF16_ROWS + 1])
        g_up = jnp.where(row == 0, g_prev, pltpu.roll(gate, 1, axis=0))
        g_dn = jnp.where(row == tm - 1, g_next, pltpu.roll(gate, tm - 1, axis=0))
        cw = cw_ref[:, c0:c0 + FFN_SUB]
        gc = g_up * cw[0:1] + gate * cw[1:2] + g_dn * cw[2:3]
        group.append((jax.nn.gelu(gc) * up).astype(BF16))
        if len(group) == FFN_DOWN_GROUP or s == n_sub - 1:
            a = jnp.concatenate(group, axis=1)
            acc = acc + jnp.dot(a, wd_ref[g0:c0 + FFN_SUB, :], preferred_element_type=F32)
            group, g0 = [], c0 + FFN_SUB

    x2 = acc
    p = _pick(nbp, pa_ref, pb_ref).astype(BF16)
    e = _rms(jnp.dot(p, pw_ref[...], preferred_element_type=F32), pn_ref[...])
    gt = jax.nn.sigmoid(jnp.dot(x2.astype(BF16), pg_ref[...], preferred_element_type=F32))
    x3 = x2 + gt * e
    if not final:
        o_refs[0][0] = x3
        return
    y = _rms(x3, fin_ref[...])
    b = pl.program_id(0)

    @pl.when(b < nbp)
    def _():
        o_refs[0][0] = y

    @pl.when(b >= nbp)
    def _():
        o_refs[1][0] = y


def _ffn(h2, x1, w_up, cw, w_down, p, layer, ple_w, ple_gate, ple_norm, fin, tm, final):
    pa, pb, nbp, off = p
    B, L, _ = x1.shape
    nt = L // tm
    halo = BF16_ROWS
    nh = L // halo
    per = tm // halo
    full = lambda shape: pl.BlockSpec(shape, lambda b, i: (0,) * len(shape), pipeline_mode=pl.Buffered(1))
    tile = pl.BlockSpec((1, tm, D_MODEL), lambda b, i: (b, i, 0))
    if final:
        out_specs = list(_pair_specs(tm, D_MODEL, nt, nbp, 0))
        out_shape = [jax.ShapeDtypeStruct((nbp, L, D_MODEL), F32), jax.ShapeDtypeStruct((B - nbp, L, D_MODEL), F32)]
    else:
        out_specs = [tile]
        out_shape = [jax.ShapeDtypeStruct((B, L, D_MODEL), F32)]
    return pl.pallas_call(
        functools.partial(_ffn_kernel, nbp=nbp, final=final),
        grid=(B, nt),
        in_specs=[
            tile,
            pl.BlockSpec((1, halo, D_MODEL), lambda b, i: (b, jnp.maximum(i * per - 1, 0), 0)),
            pl.BlockSpec((1, halo, D_MODEL), lambda b, i: (b, jnp.minimum((i + 1) * per, nh - 1), 0)),
            tile,
            full((D_MODEL, 2 * D_FF)), full((3, D_FF)), full((D_FF, D_MODEL)),
            *_pair_specs(tm, PLE_DIM, nt, nbp, off, lead=layer),
            full((PLE_DIM, D_MODEL)), full((D_MODEL, D_MODEL)), full((1, D_MODEL)), full((1, D_MODEL)),
        ],
        out_specs=out_specs,
        out_shape=out_shape,
        compiler_params=_cparams(("arbitrary", "arbitrary")),
        name="ffn_ple",
    )(h2, h2, h2, x1, w_up, cw, w_down, pa, pb, ple_w, ple_gate, ple_norm, fin)


def _rope_tables(L):
    half = HEAD_DIM // 2
    inv = ROPE_THETA ** (-jnp.arange(half, dtype=F32) / half)
    ang = jnp.arange(L, dtype=F32)[:, None] * inv[None, :]
    cos, sin = jnp.cos(ang), jnp.sin(ang)
    return jnp.concatenate([cos, cos], axis=-1), jnp.concatenate([-sin, sin], axis=-1)


def _block_diag(pool_w):
    out = jnp.zeros((POOL_WIDTH, POOL_WIDTH), pool_w.dtype)
    for g in range(len(POOL_WINDOWS)):
        sl = slice(g * POOL_GROUP_DIM, (g + 1) * POOL_GROUP_DIM)
        out = out.at[sl, sl].set(pool_w[g])
    return out


def _layer(x, B, p, layer, lw, cos, sin, fin, tm, final):
    (norm_mix, w_in, dec_f, dec_b, ret_gn, hy_conv, hy_w1, hy_b1, hy_freq, hy_w2, hy_b2, hy_w3, hy_bias,
     pool_w, pool_scale, w_out, norm_ffn, w_up, ffn_conv, w_down, ple_w, ple_gate, ple_norm) = lw
    L = x[0].shape[1]
    R = RET_WIDTH

    w_q, w_k, w_v, w_g, w_hy, w_pool = jnp.split(w_in, [R, 2 * R, 3 * R, 4 * R, 4 * R + HY_IN_WIDTH], axis=1)
    wa = jnp.concatenate([w_q, w_v, w_g, w_pool], axis=1).astype(BF16)
    qvg, kt, hy_t, pool_in = _in_proj(x, B, norm_mix[None], wa, w_k.T.astype(BF16), w_hy.astype(BF16), hy_conv,
                                      cos, sin, cos.T, sin.T, WIDE_TILE * tm)

    dec = jnp.broadcast_to(jnp.stack([dec_f, dec_b], axis=1)[:, :, None], (RET_HEADS, 2, LANES)).astype(F32)
    ret = _retention(qvg, kt, dec, ret_gn[None])

    w1t = jnp.zeros((HY_HIDDEN, HY_HIDDEN), F32).at[:, :hy_w1.shape[0]].set(hy_w1.T)
    w3t = hy_w3.T.reshape(2, 2, HY_WIDTH, HY_HIDDEN)
    kk = _hyena_filters(w1t, hy_b1[:, None], hy_freq[:, None], hy_w2.T, hy_b2[:, None], w3t, L)
    bias = jnp.broadcast_to(hy_bias.reshape(2, HY_WIDTH // SUBLANES, SUBLANES).transpose(1, 0, 2)[..., None],
                            (HY_WIDTH // SUBLANES, 2, SUBLANES, HY_BLOCK))
    hy_o = _hyena(kk, hy_t, bias, B)

    pool_o = _pool(pool_in, _block_diag(pool_w).astype(BF16), pool_scale[None])

    wo = w_out.astype(BF16)
    x1, h2 = _mix_out(x, ret, hy_o, pool_o, wo[:R], wo[R:R + HY_WIDTH], wo[R + HY_WIDTH:], norm_ffn[None],
                      WIDE_TILE * tm)

    return _ffn(h2, x1, w_up.astype(BF16), ffn_conv, w_down.astype(BF16), p, layer, ple_w.astype(BF16),
                ple_gate.astype(BF16), ple_norm[None], fin, tm, final)


def _trunk(x_a, x_b, p_a, p_b, layer_weights, norm_final, tm):
    depth = p_a.shape[0]
    nbp, L = x_a.shape[0], x_a.shape[1]
    B = nbp + x_b.shape[0]
    cos, sin = _rope_tables(L)
    fin = norm_final[None]
    x = (x_a, x_b, nbp, 0)
    p = (p_a, p_b, nbp, 0)
    for i in range(depth):
        out = _layer(x, B, p, i, [w[i] for w in layer_weights], cos, sin, fin, tm, i == depth - 1)
        x = (out[0], out[0], nbp, nbp)
    return out[0], out[1]


def kernel(x_prompt, x_sample, p_prompt, p_sample, norm_mix, w_in, ret_decay_fwd, ret_decay_bwd, ret_gn,
           hy_short_conv, hy_w1, hy_b1, hy_freq, hy_w2, hy_b2, hy_w3, hy_bias, pool_w, pool_scale, w_out,
           norm_ffn, ffn_w_up, ffn_conv, ffn_w_down, ple_w, ple_gate_w, ple_norm, norm_final):
    layer_weights = (norm_mix, w_in, ret_decay_fwd, ret_decay_bwd, ret_gn, hy_short_conv, hy_w1, hy_b1,
                     hy_freq, hy_w2, hy_b2, hy_w3, hy_bias, pool_w, pool_scale, w_out, norm_ffn, ffn_w_up,
                     ffn_conv, ffn_w_down, ple_w, ple_gate_w, ple_norm)
    return _trunk(x_prompt, x_sample, p_prompt, p_sample, layer_weights, norm_final, ROW_TILE)
```

```python
import functools
import math

import jax
import jax.numpy as jnp
from jax import lax
from jax.experimental import pallas as pl
from jax.experimental.pallas import tpu as pltpu

F32 = jnp.float32
BF16 = jnp.bfloat16

D_MODEL = 1024
RET_WIDTH = 512
RET_HEADS = 4
HEAD_DIM = 128
HY_WIDTH = 256
POOL_WIDTH = 256
POOL_GROUP_DIM = 64
POOL_WINDOWS = (2, 4, 8, 16)
POOL_PAD = 16
D_FF = 2816
PLE_DIM = 256
CHUNK = 128
ROPE_THETA = 10000.0
HY_EMB_BANDS = 16
HY_HIDDEN = 64
HY_MIN_DECAY = math.log(1e-2) / 1.5
HY_MAX_DECAY = math.log(1e-2) / 0.3
EPS = 1e-6

LANES = 128
SUBLANES = 8
BF16_ROWS = 16
MXU_DIM = 256
VMEM_LIMIT = 56 * 1024 * 1024

HY_BLOCK = MXU_DIM
HY_PAIR_UNROLL = 1
QVG_WIDTH = 3 * RET_WIDTH
RET_UNROLL = True
HY_IN_WIDTH = 3 * HY_WIDTH
ROW_TILE = 512
WIDE_TILE = 2
FFN_SUB = MXU_DIM
FFN_DOWN_GROUP = 4


def _cparams(sem):
    return pltpu.CompilerParams(dimension_semantics=sem, vmem_limit_bytes=VMEM_LIMIT)


def _rms(x, g):
    return x * lax.rsqrt(jnp.mean(x * x, axis=-1, keepdims=True) + EPS) * g


def _pair_specs(tm, width, nt, nbp, off, lead=None):
    pre = () if lead is None else (lead,)
    shape = (1, tm, width) if lead is None else (None, 1, tm, width)

    def first(b, i):
        return pre + (jnp.minimum(b, nbp - 1), jnp.where(b < nbp, i, nt - 1), 0)

    def second(b, i):
        return pre + (jnp.maximum(b - nbp, 0) + off, jnp.where(b < nbp, 0, i), 0)

    return pl.BlockSpec(shape, first), pl.BlockSpec(shape, second)


def _pick(nbp, a_ref, b_ref):
    return jnp.where(pl.program_id(0) < nbp, a_ref[0], b_ref[0])


def _pair_halo_specs(rows, width, tm, L, nbp, off):
    per, last = tm // rows, L // rows - 1

    def make(first, after):
        def index(b, i):
            r = jnp.minimum((i + 1) * per, last) if after else jnp.maximum(i * per - 1, 0)
            if first:
                return (jnp.minimum(b, nbp - 1), jnp.where(b < nbp, r, 0), 0)
            return (jnp.maximum(b - nbp, 0) + off, jnp.where(b < nbp, 0, r), 0)
        return pl.BlockSpec((1, rows, width), index)

    return make(True, False), make(False, False), make(True, True), make(False, True)


def _dwconv3_rows(u, tm, cw, first_tile, last_tile):
    main = u[:tm]
    prev = jnp.where(first_tile, 0.0, u[tm + BF16_ROWS - 1:tm + BF16_ROWS])
    nxt = jnp.where(last_tile, 0.0, u[tm + BF16_ROWS:tm + BF16_ROWS + 1])
    row = lax.broadcasted_iota(jnp.int32, main.shape, 0)
    up = jnp.where(row == 0, prev, pltpu.roll(main, 1, axis=0))
    dn = jnp.where(row == tm - 1, nxt, pltpu.roll(main, tm - 1, axis=0))
    return up * cw[0:1] + main * cw[1:2] + dn * cw[2:3]


def _in_proj_kernel(xa_ref, xb_ref, xpa_ref, xpb_ref, xna_ref, xnb_ref, g_ref, wa_ref, wk_ref, wh_ref, cw_ref,
                    cos_ref, sin_ref, cost_ref, sint_ref, qvg_ref, kt_ref, hy_ref, pool_ref, *, nbp):
    tm = xa_ref.shape[1]
    i = pl.program_id(1)
    g = g_ref[...]
    h = _rms(_pick(nbp, xa_ref, xb_ref), g).astype(BF16)
    hx = jnp.concatenate([h, _rms(_pick(nbp, xpa_ref, xpb_ref), g).astype(BF16),
                          _rms(_pick(nbp, xna_ref, xnb_ref), g).astype(BF16)], axis=0)
    u = jnp.dot(hx, wh_ref[...], preferred_element_type=F32)
    y = _dwconv3_rows(u, tm, cw_ref[...], i == 0, i == pl.num_programs(1) - 1)
    yt = y.T
    for j in range(tm // LANES):
        hy_ref[:, j, :, :] = yt[:, j * LANES:(j + 1) * LANES].reshape(HY_IN_WIDTH // SUBLANES, SUBLANES, LANES)
    kt = lax.dot_general(wk_ref[...], h, (((1,), (1,)), ((), ())), preferred_element_type=F32)
    cos_t, sin_t = cost_ref[...], sint_ref[...]
    for hd in range(RET_HEADS):
        kh = kt[hd * HEAD_DIM:(hd + 1) * HEAD_DIM]
        kt_ref[0, hd * HEAD_DIM:(hd + 1) * HEAD_DIM, :] = (
            kh * cos_t + pltpu.roll(kh, HEAD_DIM // 2, axis=0) * sin_t).astype(BF16)
    a = jnp.dot(h, wa_ref[...], preferred_element_type=F32)
    cos, sin = cos_ref[...], sin_ref[...]
    for hd in range(RET_HEADS):
        qh = a[:, hd * HEAD_DIM:(hd + 1) * HEAD_DIM]
        qvg_ref[0, :, hd * HEAD_DIM:(hd + 1) * HEAD_DIM] = (
            qh * cos + pltpu.roll(qh, HEAD_DIM // 2, axis=1) * sin).astype(BF16)
    qvg_ref[0, :, RET_WIDTH:] = a[:, RET_WIDTH:QVG_WIDTH].astype(BF16)
    pool_ref[0] = a[:, QVG_WIDTH:].astype(BF16)


def _in_proj(x, B, g, wa, wk_t, wh, cw, cos, sin, cos_t, sin_t, tm):
    xa, xb, nbp, off = x
    L = xa.shape[1]
    nt = L // tm
    const = lambda shape: pl.BlockSpec(shape, lambda b, i: (0,) * len(shape))
    return pl.pallas_call(
        functools.partial(_in_proj_kernel, nbp=nbp),
        grid=(B, nt),
        in_specs=[
            *_pair_specs(tm, D_MODEL, nt, nbp, off),
            *_pair_halo_specs(BF16_ROWS, D_MODEL, tm, L, nbp, off),
            const((1, D_MODEL)),
            const((D_MODEL, QVG_WIDTH + POOL_WIDTH)),
            const((RET_WIDTH, D_MODEL)),
            const((D_MODEL, HY_IN_WIDTH)),
            const((3, HY_IN_WIDTH)),
            pl.BlockSpec((tm, HEAD_DIM), lambda b, i: (i, 0)),
            pl.BlockSpec((tm, HEAD_DIM), lambda b, i: (i, 0)),
            pl.BlockSpec((HEAD_DIM, tm), lambda b, i: (0, i)),
            pl.BlockSpec((HEAD_DIM, tm), lambda b, i: (0, i)),
        ],
        out_specs=[
            pl.BlockSpec((1, tm, QVG_WIDTH), lambda b, i: (b, i, 0)),
            pl.BlockSpec((1, RET_WIDTH, tm), lambda b, i: (b, 0, i)),
            pl.BlockSpec((HY_IN_WIDTH // SUBLANES, tm // LANES, SUBLANES, LANES), lambda b, i: (0, i, b, 0)),
            pl.BlockSpec((1, tm, POOL_WIDTH), lambda b, i: (b, i, 0)),
        ],
        out_shape=[
            jax.ShapeDtypeStruct((B, L, QVG_WIDTH), BF16),
            jax.ShapeDtypeStruct((B, RET_WIDTH, L), BF16),
            jax.ShapeDtypeStruct((HY_IN_WIDTH // SUBLANES, L // LANES, B * SUBLANES, LANES), F32),
            jax.ShapeDtypeStruct((B, L, POOL_WIDTH), BF16),
        ],
        compiler_params=_cparams(("arbitrary", "arbitrary")),
        name="in_proj",
    )(xa, xb, xa, xb, xa, xb, g, wa, wk_t, wh, cw, cos, sin, cos_t, sin_t)


def _log_sigmoid(x):
    return jnp.minimum(x, 0.0) - jnp.log(1.0 + jnp.exp(-jnp.abs(x)))


def _retention_kernel(dec_ref, gn_ref, q_ref, kt_ref, v_ref, g_ref, o_ref, kv_s, st_s, tab_s):
    L = q_ref.shape[1]
    n_chunks = L // CHUNK
    C = CHUNK
    scale = HEAD_DIM ** -0.5

    lg = _log_sigmoid(dec_ref[0])
    lgf, lgb = lg[0:1], lg[1:2]
    ri = lax.broadcasted_iota(jnp.int32, (C, C), 0)
    ci = lax.broadcasted_iota(jnp.int32, (C, C), 1)
    rf = ri.astype(F32)
    cf = ci.astype(F32)
    dist = jnp.abs(ri - ci).astype(F32)
    tab_s[0] = jnp.exp(dist * jnp.where(ri >= ci, lgf, lgb)) * scale
    tab_s[1] = jnp.exp((rf + 1.0) * lgf) * scale
    tab_s[2] = jnp.exp((C - rf) * lgb) * scale
    tab_s[3] = jnp.exp((C - 1.0 - cf) * lgf)
    tab_s[4] = jnp.exp(cf * lgb)
    dec_f = jnp.exp(C * lgf)
    dec_b = jnp.exp(C * lgb)

    def summaries(n, carry):
        c0 = pl.multiple_of(n * C, C)
        kt = kt_ref[0, :, pl.ds(c0, C)].astype(F32)
        lhs = jnp.concatenate([kt * tab_s[3], kt * tab_s[4]], axis=0).astype(BF16)
        kv_s[n] = jnp.dot(lhs, v_ref[0, pl.ds(c0, C), :], preferred_element_type=F32)
        return carry

    lax.fori_loop(0, n_chunks, summaries, 0, unroll=RET_UNROLL)

    def scan(i, carry):
        sf, sb = carry
        m = n_chunks - 1 - i
        st_s[i, :C, :] = sf.astype(BF16)
        st_s[m, C:, :] = sb.astype(BF16)
        return sf * dec_f + kv_s[i, :C, :], sb * dec_b + kv_s[m, C:, :]

    zero = jnp.zeros((C, C), F32)
    lax.fori_loop(0, n_chunks, scan, (zero, zero), unroll=RET_UNROLL)

    gain = gn_ref[...]

    def outputs(n, carry):
        c0 = pl.multiple_of(n * C, C)
        qb = q_ref[0, pl.ds(c0, C), :]
        qn = qb.astype(F32)
        vn = v_ref[0, pl.ds(c0, C), :]
        sc = jnp.dot(qb, kt_ref[0, :, pl.ds(c0, C)], preferred_element_type=F32)
        lhs = jnp.concatenate([sc * tab_s[0], qn * tab_s[1], qn * tab_s[2]], axis=1).astype(BF16)
        rhs = jnp.concatenate([vn, st_s[n]], axis=0)
        o = jnp.dot(lhs, rhs, preferred_element_type=F32)
        mu = jnp.mean(o, axis=-1, keepdims=True)
        oc = o - mu
        var = jnp.mean(oc * oc, axis=-1, keepdims=True)
        gate = g_ref[0, pl.ds(c0, C), :].astype(F32)
        y = oc * lax.rsqrt(var + EPS) * gain * (gate * jax.nn.sigmoid(gate))
        o_ref[0, pl.ds(c0, C), :] = y.astype(BF16)
        return carry

    lax.fori_loop(0, n_chunks, outputs, 0, unroll=RET_UNROLL)


def _retention(qvg, kt, dec, gn):
    B, L, _ = qvg.shape
    H = RET_HEADS

    def col(off):
        return pl.BlockSpec((1, L, HEAD_DIM), lambda b, h: (b, 0, off + h))

    return pl.pallas_call(
        _retention_kernel,
        grid=(B, H),
        in_specs=[
            pl.BlockSpec((1, 2, LANES), lambda b, h: (h, 0, 0)),
            pl.BlockSpec((1, HEAD_DIM), lambda b, h: (0, h)),
            col(0),
            pl.BlockSpec((1, HEAD_DIM, L), lambda b, h: (b, h, 0)),
            col(H), col(2 * H),
        ],
        out_specs=pl.BlockSpec((1, L, HEAD_DIM), lambda b, h: (b, 0, h)),
        out_shape=jax.ShapeDtypeStruct((B, L, RET_WIDTH), BF16),
        scratch_shapes=[
            pltpu.VMEM((L // CHUNK, 2 * CHUNK, CHUNK), F32),
            pltpu.VMEM((L // CHUNK, 2 * CHUNK, CHUNK), BF16),
            pltpu.VMEM((5, CHUNK, CHUNK), F32),
        ],
        compiler_params=_cparams(("parallel", "parallel")),
        name="retention",
    )(dec, gn, qvg, kt, qvg, qvg)


def _filter_kernel(w1_ref, b1_ref, fr_ref, w2_ref, b2_ref, w3_ref, kk_ref, hid_s):
    L = kk_ref.shape[1] // 2
    hi = lax.Precision.HIGHEST

    @pl.when(pl.program_id(0) == 0)
    def _():
        r = lax.broadcasted_iota(jnp.int32, (HY_HIDDEN, L), 0)
        lane = lax.broadcasted_iota(jnp.int32, (HY_HIDDEN, L), 1)
        band_idx = jnp.where(r <= HY_EMB_BANDS, r - 1, r - 1 - HY_EMB_BANDS).astype(F32)
        band = 1e-4 + band_idx * ((HY_EMB_BANDS - 1 - 1e-4) / (HY_EMB_BANDS - 1))
        fr = fr_ref[...]

        def hidden(s_int):
            s = s_int.astype(F32)
            t = s / (L - 1.0)
            ang = (2.0 * math.pi / L) * s * band
            feats = jnp.where(r == 0, t,
                              jnp.where(r <= HY_EMB_BANDS, jnp.cos(ang),
                                        jnp.where(r <= 2 * HY_EMB_BANDS, -jnp.sin(ang), 0.0)))
            h = jnp.sin(fr * (jnp.dot(w1_ref[...], feats, precision=hi, preferred_element_type=F32) + b1_ref[...]))
            return jnp.sin(fr * (jnp.dot(w2_ref[...], h, precision=hi, preferred_element_type=F32) + b2_ref[...]))

        hid_s[0] = hidden(lane)
        hid_s[1] = hidden(L - lane)

    crow = lax.broadcasted_iota(jnp.int32, (HY_WIDTH, L), 0).astype(F32)
    delta = jnp.abs(HY_MIN_DECAY + crow * ((HY_MAX_DECAY - HY_MIN_DECAY) / (HY_WIDTH - 1)))
    clane = lax.broadcasted_iota(jnp.int32, (HY_WIDTH, L), 1)

    def taps(hid, s_row_int, w3):
        h = jnp.dot(w3, hid, precision=hi, preferred_element_type=F32)
        return h * jnp.exp(-(s_row_int.astype(F32) / (L - 1.0)) * delta)

    h_f = taps(hid_s[0], clane, w3_ref[0, 0])
    h_b = taps(hid_s[1], L - clane, w3_ref[0, 1])
    h_b = jnp.where(clane == 0, 0.0, h_b)
    norm = jnp.sum(jnp.abs(h_f), axis=-1, keepdims=True) + jnp.sum(jnp.abs(h_b), axis=-1, keepdims=True)
    inv = 1.0 / norm
    kk = jnp.concatenate([h_b * inv, h_f * inv], axis=1)
    bits = pltpu.bitcast(kk, jnp.uint32)
    bf = (bits + jnp.uint32(0x7FFF) + ((bits >> 16) & jnp.uint32(1))) >> 16
    lane2 = lax.broadcasted_iota(jnp.int32, bf.shape, 1)
    prev = jnp.where(lane2 == 0, jnp.uint32(0), pltpu.roll(bf, 1, axis=1))
    kk_ref[...] = bf | (prev << 16)


def _hyena_filters(w1t, b1, fr, w2t, b2, w3t, L):
    small = lambda shape: pl.BlockSpec(shape, lambda o: (0,) * len(shape))
    return pl.pallas_call(
        _filter_kernel,
        grid=(2,),
        in_specs=[
            small((HY_HIDDEN, HY_HIDDEN)), small((HY_HIDDEN, 1)), small((HY_HIDDEN, 1)),
            small((HY_HIDDEN, HY_HIDDEN)), small((HY_HIDDEN, 1)),
            pl.BlockSpec((1, 2, HY_WIDTH, HY_HIDDEN), lambda o: (o, 0, 0, 0)),
        ],
        out_specs=pl.BlockSpec((HY_WIDTH, 2 * L), lambda o: (o, 0)),
        out_shape=jax.ShapeDtypeStruct((2 * HY_WIDTH, 2 * L), jnp.uint32),
        scratch_shapes=[pltpu.VMEM((2, HY_HIDDEN, L), F32)],
        compiler_params=_cparams(("arbitrary",)),
        name="hyena_filters",
    )(w1t, b1, fr, w2t, b2, w3t)


def _hyena_kernel(kk1_ref, kk2_ref, xv_ref, x1_ref, x2_ref, bias_ref, out_ref,
                  toep1_s, toep2_s, u2_s, x1u2_s, y2_s, *, B, BP):
    n_tiles = xv_ref.shape[0]
    L = n_tiles * LANES
    T = HY_BLOCK
    NB = L // T
    TPB = T // LANES

    @pl.when(pl.program_id(0) == 0)
    def _():
        toep1_s[...] = jnp.zeros_like(toep1_s)
        toep2_s[...] = jnp.zeros_like(toep2_s)
        u2_s[...] = jnp.zeros_like(u2_s)
        x1u2_s[...] = jnp.zeros_like(x1u2_s)

    def build_toeplitz(kk_ref, c, toep_s):
        row = jnp.broadcast_to(kk_ref[pl.ds(c, 1), :], (SUBLANES, 2 * L))
        for g in range(LANES // BF16_ROWS):
            rolled = pltpu.roll(row, BF16_ROWS * g, axis=1, stride=2, stride_axis=0)
            tile = pltpu.bitcast(rolled, BF16)
            toep_s[BF16_ROWS * g:BF16_ROWS * (g + 1), :] = tile
            toep_s[LANES + BF16_ROWS * g:LANES + BF16_ROWS * (g + 1), LANES:] = tile[:, :2 * L - LANES]

    def long_conv(toep_s, u_s, y_s):
        y_s[...] = jnp.zeros_like(y_s)
        for d in range(-(NB - 1), NB):
            n = NB - abs(d)
            src = max(0, -d) * BP
            dst = max(0, d) * BP
            m = toep_s[:, L + d * T:L + (d + 1) * T]
            lhs = u_s[src:src + n * BP, :].astype(BF16)
            y_s[dst:dst + n * BP, :] += jnp.dot(lhs, m, preferred_element_type=F32)

    def channel(c, k):
        u_s, x1u_s, y_s = u2_s.at[k], x1u2_s.at[k], y2_s.at[k]
        rows = pl.ds(c, B, stride=SUBLANES)
        for j in range(n_tiles):
            J, l0 = j // TPB, (j % TPB) * LANES
            u_s[J * BP:J * BP + B, l0:l0 + LANES] = xv_ref.at[j][rows, :]
            x1u_s[J * BP:J * BP + B, l0:l0 + LANES] = x1_ref.at[j][rows, :]
        b0 = bias_ref[0, 0, pl.ds(c, 1), :]
        b1 = bias_ref[0, 1, pl.ds(c, 1), :]
        build_toeplitz(kk2_ref, c, toep2_s)
        long_conv(toep1_s, u_s, y_s)
        u_s[...] = x1u_s[...] * (y_s[...] + u_s[...] * b0)
        build_toeplitz(kk1_ref, jnp.minimum(c + 1, SUBLANES - 1), toep1_s)
        long_conv(toep2_s, u_s, y_s)
        for j in range(n_tiles):
            J, l0 = j // TPB, (j % TPB) * LANES
            yy = y_s[J * BP:J * BP + B, l0:l0 + LANES] + u_s[J * BP:J * BP + B, l0:l0 + LANES] * b1[:, :LANES]
            out_ref.at[j][rows, :] = x2_ref.at[j][rows, :] * yy

    def channel_pair(i, carry):
        channel(2 * i, 0)
        channel(2 * i + 1, 1)
        return carry

    build_toeplitz(kk1_ref, 0, toep1_s)
    lax.fori_loop(0, SUBLANES // 2, channel_pair, 0, unroll=HY_PAIR_UNROLL)


def _hyena(kk, hy_t, bias, B):
    n_in, n_tiles, R, _ = hy_t.shape
    L = n_tiles * LANES
    NB = L // HY_BLOCK
    BP = -(-B // SUBLANES) * SUBLANES
    n_c = HY_WIDTH // SUBLANES

    def sect(s):
        return pl.BlockSpec((None, n_tiles, R, LANES), lambda c: (s * n_c + c, 0, 0, 0))

    return pl.pallas_call(
        functools.partial(_hyena_kernel, B=B, BP=BP),
        grid=(n_c,),
        in_specs=[
            pl.BlockSpec((SUBLANES, 2 * L), lambda c: (c, 0)),
            pl.BlockSpec((SUBLANES, 2 * L), lambda c: (n_c + c, 0)),
            sect(0), sect(1), sect(2),
            pl.BlockSpec((1, 2, SUBLANES, HY_BLOCK), lambda c: (c, 0, 0, 0)),
        ],
        out_specs=pl.BlockSpec((None, n_tiles, R, LANES), lambda c: (c, 0, 0, 0)),
        out_shape=jax.ShapeDtypeStruct((n_c, n_tiles, R, LANES), F32),
        scratch_shapes=[
            pltpu.VMEM((2 * LANES, 2 * L), BF16), pltpu.VMEM((2 * LANES, 2 * L), BF16),
            pltpu.VMEM((2, NB * BP, HY_BLOCK), F32),
            pltpu.VMEM((2, NB * BP, HY_BLOCK), F32),
            pltpu.VMEM((2, NB * BP, HY_BLOCK), F32),
        ],
        compiler_params=_cparams(("arbitrary",)),
        name="hyena",
    )(kk, kk, hy_t, hy_t, hy_t, bias)


def _pool_kernel(u_ref, w_ref, sc_ref, o_ref):
    L = u_ref.shape[1]
    u = u_ref[0].astype(F32)
    t = lax.broadcasted_iota(jnp.int32, (L, POOL_WIDTH), 0)
    lane = lax.broadcasted_iota(jnp.int32, (L, POOL_WIDTH), 1)
    pad = jnp.zeros((POOL_PAD, POOL_WIDTH), F32)
    up = jnp.concatenate([pad, u, pad], axis=0)
    n = L + 2 * POOL_PAD

    def both(x, k):
        return pltpu.roll(x, k, axis=0) + pltpu.roll(x, n - k, axis=0)

    q2 = pltpu.roll(up, 1, axis=0) + up
    q4 = both(q2, 1)
    q8 = both(q4, 2)
    q16 = both(q8, 4)
    p2, p4, p8, p16 = (q[POOL_PAD:POOL_PAD + L] for q in (q2, q4, q8, q16))
    g = lane // POOL_GROUP_DIM
    half = jnp.where(g == 0, 1, jnp.where(g == 1, 2, jnp.where(g == 2, 4, 8)))
    total = jnp.where(g == 0, p2, jnp.where(g == 1, p4, jnp.where(g == 2, p8, p16)))
    lo = jnp.maximum(t - half, 0)
    hi = jnp.minimum(t + half - 1, L - 1)
    cnt = (hi - lo + 1).astype(F32)
    d = (total / cnt - u).astype(BF16)
    o_ref[0] = (jnp.dot(d, w_ref[...], preferred_element_type=F32) * sc_ref[...]).astype(BF16)


def _pool(pool_in, w_bd, scale):
    B, L, _ = pool_in.shape
    return pl.pallas_call(
        _pool_kernel,
        grid=(B,),
        in_specs=[
            pl.BlockSpec((1, L, POOL_WIDTH), lambda b: (b, 0, 0)),
            pl.BlockSpec((POOL_WIDTH, POOL_WIDTH), lambda b: (0, 0)),
            pl.BlockSpec((1, POOL_WIDTH), lambda b: (0, 0)),
        ],
        out_specs=pl.BlockSpec((1, L, POOL_WIDTH), lambda b: (b, 0, 0)),
        out_shape=jax.ShapeDtypeStruct((B, L, POOL_WIDTH), BF16),
        compiler_params=_cparams(("parallel",)),
        name="pool_mixer",
    )(pool_in, w_bd, scale)


def _mix_out_kernel(xa_ref, xb_ref, ret_ref, hy_ref, pool_ref, wr_ref, wh_ref, wp_ref, g_ref, x1_ref, h2_ref, *, nbp):
    tm = xa_ref.shape[1]
    hy_t = jnp.concatenate([hy_ref[:, j].reshape(HY_WIDTH, LANES) for j in range(tm // LANES)],
                           axis=1).astype(BF16)
    mix = jnp.dot(ret_ref[0], wr_ref[...], preferred_element_type=F32)
    mix += lax.dot_general(hy_t, wh_ref[...], (((0,), (0,)), ((), ())), preferred_element_type=F32)
    mix += jnp.dot(pool_ref[0], wp_ref[...], preferred_element_type=F32)
    x1 = _pick(nbp, xa_ref, xb_ref) + mix
    x1_ref[0] = x1
    h2_ref[0] = _rms(x1, g_ref[...]).astype(BF16)


def _mix_out(x, ret, hy_o, pool_o, wr, wh, wp, g, tm):
    xa, xb, nbp, off = x
    B, L, _ = ret.shape
    nt = L // tm
    return pl.pallas_call(
        functools.partial(_mix_out_kernel, nbp=nbp),
        grid=(B, nt),
        in_specs=[
            *_pair_specs(tm, D_MODEL, nt, nbp, off),
            pl.BlockSpec((1, tm, RET_WIDTH), lambda b, i: (b, i, 0)),
            pl.BlockSpec((HY_WIDTH // SUBLANES, tm // LANES, SUBLANES, LANES), lambda b, i: (0, i, b, 0)),
            pl.BlockSpec((1, tm, POOL_WIDTH), lambda b, i: (b, i, 0)),
            pl.BlockSpec((RET_WIDTH, D_MODEL), lambda b, i: (0, 0)),
            pl.BlockSpec((HY_WIDTH, D_MODEL), lambda b, i: (0, 0)),
            pl.BlockSpec((POOL_WIDTH, D_MODEL), lambda b, i: (0, 0)),
            pl.BlockSpec((1, D_MODEL), lambda b, i: (0, 0)),
        ],
        out_specs=[
            pl.BlockSpec((1, tm, D_MODEL), lambda b, i: (b, i, 0)),
            pl.BlockSpec((1, tm, D_MODEL), lambda b, i: (b, i, 0)),
        ],
        out_shape=[
            jax.ShapeDtypeStruct((B, L, D_MODEL), F32),
            jax.ShapeDtypeStruct((B, L, D_MODEL), BF16),
        ],
        compiler_params=_cparams(("arbitrary", "arbitrary")),
        name="mix_out",
    )(xa, xb, ret, hy_o, pool_o, wr, wh, wp, g)


def _ffn_kernel(h_ref, hp_ref, hn_ref, x1_ref, wup_ref, cw_ref, wd_ref, pa_ref, pb_ref, pw_ref, pg_ref,
                pn_ref, fin_ref, *o_refs, nbp, final):
    tm = h_ref.shape[1]
    i = pl.program_id(1)
    nt = pl.num_programs(1)

    hx = jnp.concatenate([h_ref[0], hp_ref[0], hn_ref[0]], axis=0)
    row = lax.broadcasted_iota(jnp.int32, (tm, FFN_SUB), 0)
    n_sub = D_FF // FFN_SUB

    def up_proj(s):
        c0 = s * FFN_SUB
        gate = jnp.dot(hx, wup_ref[:, c0:c0 + FFN_SUB], preferred_element_type=F32)
        up = jnp.dot(hx[:tm], wup_ref[:, D_FF + c0:D_FF + c0 + FFN_SUB], preferred_element_type=F32)
        return gate, up

    acc = x1_ref[0]
    nxt = up_proj(0)
    group, g0 = [], 0
    for s in range(n_sub):
        c0 = s * FFN_SUB
        gx, up = nxt
        if s + 1 < n_sub:
            nxt = up_proj(s + 1)
        gate = gx[:tm]
        g_prev = jnp.where(i == 0, 0.0, gx[tm + BF16_ROWS - 1:tm + BF16_ROWS])
        g_next = jnp.where(i == nt - 1, 0.0, gx[tm + BF16_ROWS:tm + BF16_ROWS + 1])
        g_up = jnp.where(row == 0, g_prev, pltpu.roll(gate, 1, axis=0))
        g_dn = jnp.where(row == tm - 1, g_next, pltpu.roll(gate, tm - 1, axis=0))
        cw = cw_ref[:, c0:c0 + FFN_SUB]
        gc = g_up * cw[0:1] + gate * cw[1:2] + g_dn * cw[2:3]
        group.append((jax.nn.gelu(gc) * up).astype(BF16))
        if len(group) == FFN_DOWN_GROUP or s == n_sub - 1:
            a = jnp.concatenate(group, axis=1)
            acc = acc + jnp.dot(a, wd_ref[g0:c0 + FFN_SUB, :], preferred_element_type=F32)
            group, g0 = [], c0 + FFN_SUB

    x2 = acc
    p = _pick(nbp, pa_ref, pb_ref).astype(BF16)
    e = _rms(jnp.dot(p, pw_ref[...], preferred_element_type=F32), pn_ref[...])
    gt = jax.nn.sigmoid(jnp.dot(x2.astype(BF16), pg_ref[...], preferred_element_type=F32))
    x3 = x2 + gt * e
    if not final:
        o_refs[0][0] = x3
        return
    y = _rms(x3, fin_ref[...])
    b = pl.program_id(0)

    @pl.when(b < nbp)
    def _():
        o_refs[0][0] = y

    @pl.when(b >= nbp)
    def _():
        o_refs[1][0] = y


def _ffn(h2, x1, w_up, cw, w_down, p, layer, ple_w, ple_gate, ple_norm, fin, tm, final):
    pa, pb, nbp, off = p
    B, L, _ = x1.shape
    nt = L // tm
    halo = BF16_ROWS
    nh = L // halo
    per = tm // halo
    full = lambda shape: pl.BlockSpec(shape, lambda b, i: (0,) * len(shape), pipeline_mode=pl.Buffered(1))
    tile = pl.BlockSpec((1, tm, D_MODEL), lambda b, i: (b, i, 0))
    if final:
        out_specs = list(_pair_specs(tm, D_MODEL, nt, nbp, 0))
        out_shape = [jax.ShapeDtypeStruct((nbp, L, D_MODEL), F32), jax.ShapeDtypeStruct((B - nbp, L, D_MODEL), F32)]
    else:
        out_specs = [tile]
        out_shape = [jax.ShapeDtypeStruct((B, L, D_MODEL), F32)]
    return pl.pallas_call(
        functools.partial(_ffn_kernel, nbp=nbp, final=final),
        grid=(B, nt),
        in_specs=[
            tile,
            pl.BlockSpec((1, halo, D_MODEL), lambda b, i: (b, jnp.maximum(i * per - 1, 0), 0)),
            pl.BlockSpec((1, halo, D_MODEL), lambda b, i: (b, jnp.minimum((i + 1) * per, nh - 1), 0)),
            tile,
            full((D_MODEL, 2 * D_FF)), full((3, D_FF)), full((D_FF, D_MODEL)),
            *_pair_specs(tm, PLE_DIM, nt, nbp, off, lead=layer),
            full((PLE_DIM, D_MODEL)), full((D_MODEL, D_MODEL)), full((1, D_MODEL)), full((1, D_MODEL)),
        ],
        out_specs=out_specs,
        out_shape=out_shape,
        compiler_params=_cparams(("arbitrary", "arbitrary")),
        name="ffn_ple",
    )(h2, h2, h2, x1, w_up, cw, w_down, pa, pb, ple_w, ple_gate, ple_norm, fin)


def _rope_tables(L):
    half = HEAD_DIM // 2
    inv = ROPE_THETA ** (-jnp.arange(half, dtype=F32) / half)
    ang = jnp.arange(L, dtype=F32)[:, None] * inv[None, :]
    cos, sin = jnp.cos(ang), jnp.sin(ang)
    return jnp.concatenate([cos, cos], axis=-1), jnp.concatenate([-sin, sin], axis=-1)


def _block_diag(pool_w):
    out = jnp.zeros((POOL_WIDTH, POOL_WIDTH), pool_w.dtype)
    for g in range(len(POOL_WINDOWS)):
        sl = slice(g * POOL_GROUP_DIM, (g + 1) * POOL_GROUP_DIM)
        out = out.at[sl, sl].set(pool_w[g])
    return out


def _layer(x, B, p, layer, lw, cos, sin, fin, tm, final):
    (norm_mix, w_in, dec_f, dec_b, ret_gn, hy_conv, hy_w1, hy_b1, hy_freq, hy_w2, hy_b2, hy_w3, hy_bias,
     pool_w, pool_scale, w_out, norm_ffn, w_up, ffn_conv, w_down, ple_w, ple_gate, ple_norm) = lw
    L = x[0].shape[1]
    R = RET_WIDTH

    w_q, w_k, w_v, w_g, w_hy, w_pool = jnp.split(w_in, [R, 2 * R, 3 * R, 4 * R, 4 * R + HY_IN_WIDTH], axis=1)
    wa = jnp.concatenate([w_q, w_v, w_g, w_pool], axis=1).astype(BF16)
    qvg, kt, hy_t, pool_in = _in_proj(x, B, norm_mix[None], wa, w_k.T.astype(BF16), w_hy.astype(BF16), hy_conv,
                                      cos, sin, cos.T, sin.T, WIDE_TILE * tm)

    dec = jnp.broadcast_to(jnp.stack([dec_f, dec_b], axis=1)[:, :, None], (RET_HEADS, 2, LANES)).astype(F32)
    ret = _retention(qvg, kt, dec, ret_gn[None])

    w1t = jnp.zeros((HY_HIDDEN, HY_HIDDEN), F32).at[:, :hy_w1.shape[0]].set(hy_w1.T)
    w3t = hy_w3.T.reshape(2, 2, HY_WIDTH, HY_HIDDEN)
    kk = _hyena_filters(w1t, hy_b1[:, None], hy_freq[:, None], hy_w2.T, hy_b2[:, None], w3t, L)
    bias = jnp.broadcast_to(hy_bias.reshape(2, HY_WIDTH // SUBLANES, SUBLANES).transpose(1, 0, 2)[..., None],
                            (HY_WIDTH // SUBLANES, 2, SUBLANES, HY_BLOCK))
    hy_o = _hyena(kk, hy_t, bias, B)

    pool_o = _pool(pool_in, _block_diag(pool_w).astype(BF16), pool_scale[None])

    wo = w_out.astype(BF16)
    x1, h2 = _mix_out(x, ret, hy_o, pool_o, wo[:R], wo[R:R + HY_WIDTH], wo[R + HY_WIDTH:], norm_ffn[None],
                      WIDE_TILE * tm)

    return _ffn(h2, x1, w_up.astype(BF16), ffn_conv, w_down.astype(BF16), p, layer, ple_w.astype(BF16),
                ple_gate.astype(BF16), ple_norm[None], fin, tm, final)


def _trunk(x_a, x_b, p_a, p_b, layer_weights, norm_final, tm):
    depth = p_a.shape[0]
    nbp, L = x_a.shape[0], x_a.shape[1]
    B = nbp + x_b.shape[0]
    cos, sin = _rope_tables(L)
    fin = norm_final[None]
    x = (x_a, x_b, nbp, 0)
    p = (p_a, p_b, nbp, 0)
    for i in range(depth):
        out = _layer(x, B, p, i, [w[i] for w in layer_weights], cos, sin, fin, tm, i == depth - 1)
        x = (out[0], out[0], nbp, nbp)
    return out[0], out[1]


def kernel(x_prompt, x_sample, p_prompt, p_sample, norm_mix, w_in, ret_decay_fwd, ret_decay_bwd, ret_gn,
           hy_short_conv, hy_w1, hy_b1, hy_freq, hy_w2, hy_b2, hy_w3, hy_bias, pool_w, pool_scale, w_out,
           norm_ffn, ffn_w_up, ffn_conv, ffn_w_down, ple_w, ple_gate_w, ple_norm, norm_final):
    layer_weights = (norm_mix, w_in, ret_decay_fwd, ret_decay_bwd, ret_gn, hy_short_conv, hy_w1, hy_b1,
                     hy_freq, hy_w2, hy_b2, hy_w3, hy_bias, pool_w, pool_scale, w_out, norm_ffn, ffn_w_up,
                     ffn_conv, ffn_w_down, ple_w, ple_gate_w, ple_norm)
    return _trunk(x_prompt, x_sample, p_prompt, p_sample, layer_weights, norm_final, ROW_TILE)
```

```python
import functools
import math

import jax
import jax.numpy as jnp
from jax import lax
from jax.experimental import pallas as pl
from jax.experimental.pallas import tpu as pltpu

F32 = jnp.float32
BF16 = jnp.bfloat16

D_MODEL = 1024
RET_WIDTH = 512
RET_HEADS = 4
HEAD_DIM = 128
HY_WIDTH = 256
POOL_WIDTH = 256
POOL_GROUP_DIM = 64
POOL_WINDOWS = (2, 4, 8, 16)
POOL_PAD = 16
D_FF = 2816
PLE_DIM = 256
CHUNK = 128
ROPE_THETA = 10000.0
HY_EMB_BANDS = 16
HY_HIDDEN = 64
HY_MIN_DECAY = math.log(1e-2) / 1.5
HY_MAX_DECAY = math.log(1e-2) / 0.3
EPS = 1e-6

LANES = 128
SUBLANES = 8
BF16_ROWS = 16
MXU_DIM = 256
VMEM_LIMIT = 56 * 1024 * 1024

HY_BLOCK = MXU_DIM
HY_PAIR_UNROLL = 1
QVG_WIDTH = 3 * RET_WIDTH
RET_UNROLL = True
HY_IN_WIDTH = 3 * HY_WIDTH
ROW_TILE = 512
WIDE_TILE = 2
FFN_SUB = MXU_DIM
FFN_DOWN_GROUP = 4


def _cparams(sem):
    return pltpu.CompilerParams(dimension_semantics=sem, vmem_limit_bytes=VMEM_LIMIT)


def _rms(x, g):
    return x * lax.rsqrt(jnp.mean(x * x, axis=-1, keepdims=True) + EPS) * g


def _pair_specs(tm, width, nt, nbp, off, lead=None):
    pre = () if lead is None else (lead,)
    shape = (1, tm, width) if lead is None else (None, 1, tm, width)

    def first(b, i):
        return pre + (jnp.minimum(b, nbp - 1), jnp.where(b < nbp, i, nt - 1), 0)

    def second(b, i):
        return pre + (jnp.maximum(b - nbp, 0) + off, jnp.where(b < nbp, 0, i), 0)

    return pl.BlockSpec(shape, first), pl.BlockSpec(shape, second)


def _pick(nbp, a_ref, b_ref):
    return jnp.where(pl.program_id(0) < nbp, a_ref[0], b_ref[0])


def _pair_halo_specs(rows, width, tm, L, nbp, off):
    per, last = tm // rows, L // rows - 1

    def make(first, after):
        def index(b, i):
            r = jnp.minimum((i + 1) * per, last) if after else jnp.maximum(i * per - 1, 0)
            if first:
                return (jnp.minimum(b, nbp - 1), jnp.where(b < nbp, r, 0), 0)
            return (jnp.maximum(b - nbp, 0) + off, jnp.where(b < nbp, 0, r), 0)
        return pl.BlockSpec((1, rows, width), index)

    return make(True, False), make(False, False), make(True, True), make(False, True)


def _dwconv3_rows(u, tm, cw, first_tile, last_tile):
    main = u[:tm]
    prev = jnp.where(first_tile, 0.0, u[tm + BF16_ROWS - 1:tm + BF16_ROWS])
    nxt = jnp.where(last_tile, 0.0, u[tm + BF16_ROWS:tm + BF16_ROWS + 1])
    row = lax.broadcasted_iota(jnp.int32, main.shape, 0)
    up = jnp.where(row == 0, prev, pltpu.roll(main, 1, axis=0))
    dn = jnp.where(row == tm - 1, nxt, pltpu.roll(main, tm - 1, axis=0))
    return up * cw[0:1] + main * cw[1:2] + dn * cw[2:3]


def _in_proj_kernel(xa_ref, xb_ref, xpa_ref, xpb_ref, xna_ref, xnb_ref, g_ref, wa_ref, wk_ref, wh_ref, cw_ref,
                    cos_ref, sin_ref, cost_ref, sint_ref, qvg_ref, kt_ref, hy_ref, pool_ref, *, nbp):
    tm = xa_ref.shape[1]
    i = pl.program_id(1)
    g = g_ref[...]
    h = _rms(_pick(nbp, xa_ref, xb_ref), g).astype(BF16)
    hx = jnp.concatenate([h, _rms(_pick(nbp, xpa_ref, xpb_ref), g).astype(BF16),
                          _rms(_pick(nbp, xna_ref, xnb_ref), g).astype(BF16)], axis=0)
    u = jnp.dot(hx, wh_ref[...], preferred_element_type=F32)
    y = _dwconv3_rows(u, tm, cw_ref[...], i == 0, i == pl.num_programs(1) - 1)
    yt = y.T
    for j in range(tm // LANES):
        hy_ref[:, j, :, :] = yt[:, j * LANES:(j + 1) * LANES].reshape(HY_IN_WIDTH // SUBLANES, SUBLANES, LANES)
    kt = lax.dot_general(wk_ref[...], h, (((1,), (1,)), ((), ())), preferred_element_type=F32)
    cos_t, sin_t = cost_ref[...], sint_ref[...]
    for hd in range(RET_HEADS):
        kh = kt[hd * HEAD_DIM:(hd + 1) * HEAD_DIM]
        kt_ref[0, hd * HEAD_DIM:(hd + 1) * HEAD_DIM, :] = (
            kh * cos_t + pltpu.roll(kh, HEAD_DIM // 2, axis=0) * sin_t).astype(BF16)
    a = jnp.dot(h, wa_ref[...], preferred_element_type=F32)
    cos, sin = cos_ref[...], sin_ref[...]
    for hd in range(RET_HEADS):
        qh = a[:, hd * HEAD_DIM:(hd + 1) * HEAD_DIM]
        qvg_ref[0, :, hd * HEAD_DIM:(hd + 1) * HEAD_DIM] = (
            qh * cos + pltpu.roll(qh, HEAD_DIM // 2, axis=1) * sin).astype(BF16)
    qvg_ref[0, :, RET_WIDTH:] = a[:, RET_WIDTH:QVG_WIDTH].astype(BF16)
    pool_ref[0] = a[:, QVG_WIDTH:].astype(BF16)


def _in_proj(x, B, g, wa, wk_t, wh, cw, cos, sin, cos_t, sin_t, tm):
    xa, xb, nbp, off = x
    L = xa.shape[1]
    nt = L // tm
    const = lambda shape: pl.BlockSpec(shape, lambda b, i: (0,) * len(shape))
    return pl.pallas_call(
        functools.partial(_in_proj_kernel, nbp=nbp),
        grid=(B, nt),
        in_specs=[
            *_pair_specs(tm, D_MODEL, nt, nbp, off),
            *_pair_halo_specs(BF16_ROWS, D_MODEL, tm, L, nbp, off),
            const((1, D_MODEL)),
            const((D_MODEL, QVG_WIDTH + POOL_WIDTH)),
            const((RET_WIDTH, D_MODEL)),
            const((D_MODEL, HY_IN_WIDTH)),
            const((3, HY_IN_WIDTH)),
            pl.BlockSpec((tm, HEAD_DIM), lambda b, i: (i, 0)),
            pl.BlockSpec((tm, HEAD_DIM), lambda b, i: (i, 0)),
            pl.BlockSpec((HEAD_DIM, tm), lambda b, i: (0, i)),
            pl.BlockSpec((HEAD_DIM, tm), lambda b, i: (0, i)),
        ],
        out_specs=[
            pl.BlockSpec((1, tm, QVG_WIDTH), lambda b, i: (b, i, 0)),
            pl.BlockSpec((1, RET_WIDTH, tm), lambda b, i: (b, 0, i)),
            pl.BlockSpec((HY_IN_WIDTH // SUBLANES, tm // LANES, SUBLANES, LANES), lambda b, i: (0, i, b, 0)),
            pl.BlockSpec((1, tm, POOL_WIDTH), lambda b, i: (b, i, 0)),
        ],
        out_shape=[
            jax.ShapeDtypeStruct((B, L, QVG_WIDTH), BF16),
            jax.ShapeDtypeStruct((B, RET_WIDTH, L), BF16),
            jax.ShapeDtypeStruct((HY_IN_WIDTH // SUBLANES, L // LANES, B * SUBLANES, LANES), F32),
            jax.ShapeDtypeStruct((B, L, POOL_WIDTH), BF16),
        ],
        compiler_params=_cparams(("arbitrary", "arbitrary")),
        name="in_proj",
    )(xa, xb, xa, xb, xa, xb, g, wa, wk_t, wh, cw, cos, sin, cos_t, sin_t)


def _log_sigmoid(x):
    return jnp.minimum(x, 0.0) - jnp.log(1.0 + jnp.exp(-jnp.abs(x)))


def _retention_kernel(dec_ref, gn_ref, q_ref, kt_ref, v_ref, g_ref, o_ref, kv_s, st_s, tab_s):
    L = q_ref.shape[1]
    n_chunks = L // CHUNK
    C = CHUNK
    scale = HEAD_DIM ** -0.5

    lg = _log_sigmoid(dec_ref[0])
    lgf, lgb = lg[0:1], lg[1:2]
    ri = lax.broadcasted_iota(jnp.int32, (C, C), 0)
    ci = lax.broadcasted_iota(jnp.int32, (C, C), 1)
    rf = ri.astype(F32)
    cf = ci.astype(F32)
    dist = jnp.abs(ri - ci).astype(F32)
    tab_s[0] = jnp.exp(dist * jnp.where(ri >= ci, lgf, lgb)) * scale
    tab_s[1] = jnp.exp((rf + 1.0) * lgf) * scale
    tab_s[2] = jnp.exp((C - rf) * lgb) * scale
    tab_s[3] = jnp.exp((C - 1.0 - cf) * lgf)
    tab_s[4] = jnp.exp(cf * lgb)
    dec_f = jnp.exp(C * lgf)
    dec_b = jnp.exp(C * lgb)

    def summaries(n, carry):
        c0 = pl.multiple_of(n * C, C)
        kt = kt_ref[0, :, pl.ds(c0, C)].astype(F32)
        lhs = jnp.concatenate([kt * tab_s[3], kt * tab_s[4]], axis=0).astype(BF16)
        kv_s[n] = jnp.dot(lhs, v_ref[0, pl.ds(c0, C), :], preferred_element_type=F32)
        return carry

    lax.fori_loop(0, n_chunks, summaries, 0, unroll=RET_UNROLL)

    def scan(i, carry):
        sf, sb = carry
        m = n_chunks - 1 - i
        st_s[i, :C, :] = sf.astype(BF16)
        st_s[m, C:, :] = sb.astype(BF16)
        return sf * dec_f + kv_s[i, :C, :], sb * dec_b + kv_s[m, C:, :]

    zero = jnp.zeros((C, C), F32)
    lax.fori_loop(0, n_chunks, scan, (zero, zero), unroll=RET_UNROLL)

    gain = gn_ref[...]

    def outputs(n, carry):
        c0 = pl.multiple_of(n * C, C)
        qb = q_ref[0, pl.ds(c0, C), :]
        qn = qb.astype(F32)
        vn = v_ref[0, pl.ds(c0, C), :]
        sc = jnp.dot(qb, kt_ref[0, :, pl.ds(c0, C)], preferred_element_type=F32)
        lhs = jnp.concatenate([sc * tab_s[0], qn * tab_s[1], qn * tab_s[2]], axis=1).astype(BF16)
        rhs = jnp.concatenate([vn, st_s[n]], axis=0)
        o = jnp.dot(lhs, rhs, preferred_element_type=F32)
        mu = jnp.mean(o, axis=-1, keepdims=True)
        oc = o - mu
        var = jnp.mean(oc * oc, axis=-1, keepdims=True)
        gate = g_ref[0, pl.ds(c0, C), :].astype(F32)
        y = oc * lax.rsqrt(var + EPS) * gain * (gate * jax.nn.sigmoid(gate))
        o_ref[0, pl.ds(c0, C), :] = y.astype(BF16)
        return carry

    lax.fori_loop(0, n_chunks, outputs, 0, unroll=RET_UNROLL)


def _retention(qvg, kt, dec, gn):
    B, L, _ = qvg.shape
    H = RET_HEADS

    def col(off):
        return pl.BlockSpec((1, L, HEAD_DIM), lambda b, h: (b, 0, off + h))

    return pl.pallas_call(
        _retention_kernel,
        grid=(B, H),
        in_specs=[
            pl.BlockSpec((1, 2, LANES), lambda b, h: (h, 0, 0)),
            pl.BlockSpec((1, HEAD_DIM), lambda b, h: (0, h)),
            col(0),
            pl.BlockSpec((1, HEAD_DIM, L), lambda b, h: (b, h, 0)),
            col(H), col(2 * H),
        ],
        out_specs=pl.BlockSpec((1, L, HEAD_DIM), lambda b, h: (b, 0, h)),
        out_shape=jax.ShapeDtypeStruct((B, L, RET_WIDTH), BF16),
        scratch_shapes=[
            pltpu.VMEM((L // CHUNK, 2 * CHUNK, CHUNK), F32),
            pltpu.VMEM((L // CHUNK, 2 * CHUNK, CHUNK), BF16),
            pltpu.VMEM((5, CHUNK, CHUNK), F32),
        ],
        compiler_params=_cparams(("parallel", "parallel")),
        name="retention",
    )(dec, gn, qvg, kt, qvg, qvg)


def _filter_kernel(w1_ref, b1_ref, fr_ref, w2_ref, b2_ref, w3_ref, kk_ref, hid_s):
    L = kk_ref.shape[1] // 2
    hi = lax.Precision.HIGHEST

    @pl.when(pl.program_id(0) == 0)
    def _():
        r = lax.broadcasted_iota(jnp.int32, (HY_HIDDEN, L), 0)
        lane = lax.broadcasted_iota(jnp.int32, (HY_HIDDEN, L), 1)
        band_idx = jnp.where(r <= HY_EMB_BANDS, r - 1, r - 1 - HY_EMB_BANDS).astype(F32)
        band = 1e-4 + band_idx * ((HY_EMB_BANDS - 1 - 1e-4) / (HY_EMB_BANDS - 1))
        fr = fr_ref[...]

        def hidden(s_int):
            s = s_int.astype(F32)
            t = s / (L - 1.0)
            ang = (2.0 * math.pi / L) * s * band
            feats = jnp.where(r == 0, t,
                              jnp.where(r <= HY_EMB_BANDS, jnp.cos(ang),
                                        jnp.where(r <= 2 * HY_EMB_BANDS, -jnp.sin(ang), 0.0)))
            h = jnp.sin(fr * (jnp.dot(w1_ref[...], feats, precision=hi, preferred_element_type=F32) + b1_ref[...]))
            return jnp.sin(fr * (jnp.dot(w2_ref[...], h, precision=hi, preferred_element_type=F32) + b2_ref[...]))

        hid_s[0] = hidden(lane)
        hid_s[1] = hidden(L - lane)

    crow = lax.broadcasted_iota(jnp.int32, (HY_WIDTH, L), 0).astype(F32)
    delta = jnp.abs(HY_MIN_DECAY + crow * ((HY_MAX_DECAY - HY_MIN_DECAY) / (HY_WIDTH - 1)))
    clane = lax.broadcasted_iota(jnp.int32, (HY_WIDTH, L), 1)

    def taps(hid, s_row_int, w3):
        h = jnp.dot(w3, hid, precision=hi, preferred_element_type=F32)
        return h * jnp.exp(-(s_row_int.astype(F32) / (L - 1.0)) * delta)

    h_f = taps(hid_s[0], clane, w3_ref[0, 0])
    h_b = taps(hid_s[1], L - clane, w3_ref[0, 1])
    h_b = jnp.where(clane == 0, 0.0, h_b)
    norm = jnp.sum(jnp.abs(h_f), axis=-1, keepdims=True) + jnp.sum(jnp.abs(h_b), axis=-1, keepdims=True)
    inv = 1.0 / norm
    kk = jnp.concatenate([h_b * inv, h_f * inv], axis=1)
    bits = pltpu.bitcast(kk, jnp.uint32)
    bf = (bits + jnp.uint32(0x7FFF) + ((bits >> 16) & jnp.uint32(1))) >> 16
    lane2 = lax.broadcasted_iota(jnp.int32, bf.shape, 1)
    prev = jnp.where(lane2 == 0, jnp.uint32(0), pltpu.roll(bf, 1, axis=1))
    kk_ref[...] = bf | (prev << 16)


def _hyena_filters(w1t, b1, fr, w2t, b2, w3t, L):
    small = lambda shape: pl.BlockSpec(shape, lambda o: (0,) * len(shape))
    return pl.pallas_call(
        _filter_kernel,
        grid=(2,),
        in_specs=[
            small((HY_HIDDEN, HY_HIDDEN)), small((HY_HIDDEN, 1)), small((HY_HIDDEN, 1)),
            small((HY_HIDDEN, HY_HIDDEN)), small((HY_HIDDEN, 1)),
            pl.BlockSpec((1, 2, HY_WIDTH, HY_HIDDEN), lambda o: (o, 0, 0, 0)),
        ],
        out_specs=pl.BlockSpec((HY_WIDTH, 2 * L), lambda o: (o, 0)),
        out_shape=jax.ShapeDtypeStruct((2 * HY_WIDTH, 2 * L), jnp.uint32),
        scratch_shapes=[pltpu.VMEM((2, HY_HIDDEN, L), F32)],
        compiler_params=_cparams(("arbitrary",)),
        name="hyena_filters",
    )(w1t, b1, fr, w2t, b2, w3t)


def _hyena_kernel(kk1_ref, kk2_ref, xv_ref, x1_ref, x2_ref, bias_ref, out_ref,
                  toep1_s, toep2_s, u2_s, x1u2_s, y2_s, *, B, BP):
    n_tiles = xv_ref.shape[0]
    L = n_tiles * LANES
    T = HY_BLOCK
    NB = L // T
    TPB = T // LANES

    @pl.when(pl.program_id(0) == 0)
    def _():
        toep1_s[...] = jnp.zeros_like(toep1_s)
        toep2_s[...] = jnp.zeros_like(toep2_s)
        u2_s[...] = jnp.zeros_like(u2_s)
        x1u2_s[...] = jnp.zeros_like(x1u2_s)

    def build_toeplitz(kk_ref, c, toep_s):
        row = jnp.broadcast_to(kk_ref[pl.ds(c, 1), :], (SUBLANES, 2 * L))
        for g in range(LANES // BF16_ROWS):
            rolled = pltpu.roll(row, BF16_ROWS * g, axis=1, stride=2, stride_axis=0)
            tile = pltpu.bitcast(rolled, BF16)
            toep_s[BF16_ROWS * g:BF16_ROWS * (g + 1), :] = tile

    def long_conv(toep_s, u_s, y_s):
        y_s[...] = jnp.zeros_like(y_s)
        for d in range(-(NB - 1), NB):
            n = NB - abs(d)
            src = max(0, -d) * BP
            dst = max(0, d) * BP
            w0 = L + d * T
            m = jnp.concatenate([toep_s[:, w0:w0 + T], toep_s[:, w0 - LANES:w0 - LANES + T]], axis=0)
            lhs = u_s[src:src + n * BP, :].astype(BF16)
            y_s[dst:dst + n * BP, :] += jnp.dot(lhs, m, preferred_element_type=F32)

    def channel(c, k):
        u_s, x1u_s, y_s = u2_s.at[k], x1u2_s.at[k], y2_s.at[k]
        rows = pl.ds(c, B, stride=SUBLANES)
        for j in range(n_tiles):
            J, l0 = j // TPB, (j % TPB) * LANES
            u_s[J * BP:J * BP + B, l0:l0 + LANES] = xv_ref.at[j][rows, :]
            x1u_s[J * BP:J * BP + B, l0:l0 + LANES] = x1_ref.at[j][rows, :]
        b0 = bias_ref[0, 0, pl.ds(c, 1), :]
        b1 = bias_ref[0, 1, pl.ds(c, 1), :]
        build_toeplitz(kk2_ref, c, toep2_s)
        long_conv(toep1_s, u_s, y_s)
        u_s[...] = x1u_s[...] * (y_s[...] + u_s[...] * b0)
        build_toeplitz(kk1_ref, jnp.minimum(c + 1, SUBLANES - 1), toep1_s)
        long_conv(toep2_s, u_s, y_s)
        for j in range(n_tiles):
            J, l0 = j // TPB, (j % TPB) * LANES
            yy = y_s[J * BP:J * BP + B, l0:l0 + LANES] + u_s[J * BP:J * BP + B, l0:l0 + LANES] * b1[:, :LANES]
            out_ref.at[j][rows, :] = x2_ref.at[j][rows, :] * yy

    def channel_pair(i, carry):
        channel(2 * i, 0)
        channel(2 * i + 1, 1)
        return carry

    build_toeplitz(kk1_ref, 0, toep1_s)
    lax.fori_loop(0, SUBLANES // 2, channel_pair, 0, unroll=HY_PAIR_UNROLL)


def _hyena(kk, hy_t, bias, B):
    n_in, n_tiles, R, _ = hy_t.shape
    L = n_tiles * LANES
    NB = L // HY_BLOCK
    BP = -(-B // SUBLANES) * SUBLANES
    n_c = HY_WIDTH // SUBLANES

    def sect(s):
        return pl.BlockSpec((None, n_tiles, R, LANES), lambda c: (s * n_c + c, 0, 0, 0))

    return pl.pallas_call(
        functools.partial(_hyena_kernel, B=B, BP=BP),
        grid=(n_c,),
        in_specs=[
            pl.BlockSpec((SUBLANES, 2 * L), lambda c: (c, 0)),
            pl.BlockSpec((SUBLANES, 2 * L), lambda c: (n_c + c, 0)),
            sect(0), sect(1), sect(2),
            pl.BlockSpec((1, 2, SUBLANES, HY_BLOCK), lambda c: (c, 0, 0, 0)),
        ],
        out_specs=pl.BlockSpec((None, n_tiles, R, LANES), lambda c: (c, 0, 0, 0)),
        out_shape=jax.ShapeDtypeStruct((n_c, n_tiles, R, LANES), F32),
        scratch_shapes=[
            pltpu.VMEM((LANES, 2 * L), BF16), pltpu.VMEM((LANES, 2 * L), BF16),
            pltpu.VMEM((2, NB * BP, HY_BLOCK), F32),
            pltpu.VMEM((2, NB * BP, HY_BLOCK), F32),
            pltpu.VMEM((2, NB * BP, HY_BLOCK), F32),
        ],
        compiler_params=_cparams(("arbitrary",)),
        name="hyena",
    )(kk, kk, hy_t, hy_t, hy_t, bias)


def _pool_kernel(u_ref, w_ref, sc_ref, o_ref):
    L = u_ref.shape[1]
    u = u_ref[0].astype(F32)
    t = lax.broadcasted_iota(jnp.int32, (L, POOL_WIDTH), 0)
    lane = lax.broadcasted_iota(jnp.int32, (L, POOL_WIDTH), 1)
    pad = jnp.zeros((POOL_PAD, POOL_WIDTH), F32)
    up = jnp.concatenate([pad, u, pad], axis=0)
    n = L + 2 * POOL_PAD

    def both(x, k):
        return pltpu.roll(x, k, axis=0) + pltpu.roll(x, n - k, axis=0)

    q2 = pltpu.roll(up, 1, axis=0) + up
    q4 = both(q2, 1)
    q8 = both(q4, 2)
    q16 = both(q8, 4)
    p2, p4, p8, p16 = (q[POOL_PAD:POOL_PAD + L] for q in (q2, q4, q8, q16))
    g = lane // POOL_GROUP_DIM
    half = jnp.where(g == 0, 1, jnp.where(g == 1, 2, jnp.where(g == 2, 4, 8)))
    total = jnp.where(g == 0, p2, jnp.where(g == 1, p4, jnp.where(g == 2, p8, p16)))
    lo = jnp.maximum(t - half, 0)
    hi = jnp.minimum(t + half - 1, L - 1)
    cnt = (hi - lo + 1).astype(F32)
    d = (total / cnt - u).astype(BF16)
    o_ref[0] = (jnp.dot(d, w_ref[...], preferred_element_type=F32) * sc_ref[...]).astype(BF16)


def _pool(pool_in, w_bd, scale):
    B, L, _ = pool_in.shape
    return pl.pallas_call(
        _pool_kernel,
        grid=(B,),
        in_specs=[
            pl.BlockSpec((1, L, POOL_WIDTH), lambda b: (b, 0, 0)),
            pl.BlockSpec((POOL_WIDTH, POOL_WIDTH), lambda b: (0, 0)),
            pl.BlockSpec((1, POOL_WIDTH), lambda b: (0, 0)),
        ],
        out_specs=pl.BlockSpec((1, L, POOL_WIDTH), lambda b: (b, 0, 0)),
        out_shape=jax.ShapeDtypeStruct((B, L, POOL_WIDTH), BF16),
        compiler_params=_cparams(("parallel",)),
        name="pool_mixer",
    )(pool_in, w_bd, scale)


def _mix_out_kernel(xa_ref, xb_ref, ret_ref, hy_ref, pool_ref, wr_ref, wh_ref, wp_ref, g_ref, x1_ref, h2_ref, *, nbp):
    tm = xa_ref.shape[1]
    hy_t = jnp.concatenate([hy_ref[:, j].reshape(HY_WIDTH, LANES) for j in range(tm // LANES)],
                           axis=1).astype(BF16)
    mix = jnp.dot(ret_ref[0], wr_ref[...], preferred_element_type=F32)
    mix += lax.dot_general(hy_t, wh_ref[...], (((0,), (0,)), ((), ())), preferred_element_type=F32)
    mix += jnp.dot(pool_ref[0], wp_ref[...], preferred_element_type=F32)
    x1 = _pick(nbp, xa_ref, xb_ref) + mix
    x1_ref[0] = x1
    h2_ref[0] = _rms(x1, g_ref[...]).astype(BF16)


def _mix_out(x, ret, hy_o, pool_o, wr, wh, wp, g, tm):
    xa, xb, nbp, off = x
    B, L, _ = ret.shape
    nt = L // tm
    return pl.pallas_call(
        functools.partial(_mix_out_kernel, nbp=nbp),
        grid=(B, nt),
        in_specs=[
            *_pair_specs(tm, D_MODEL, nt, nbp, off),
            pl.BlockSpec((1, tm, RET_WIDTH), lambda b, i: (b, i, 0)),
            pl.BlockSpec((HY_WIDTH // SUBLANES, tm // LANES, SUBLANES, LANES), lambda b, i: (0, i, b, 0)),
            pl.BlockSpec((1, tm, POOL_WIDTH), lambda b, i: (b, i, 0)),
            pl.BlockSpec((RET_WIDTH, D_MODEL), lambda b, i: (0, 0)),
            pl.BlockSpec((HY_WIDTH, D_MODEL), lambda b, i: (0, 0)),
            pl.BlockSpec((POOL_WIDTH, D_MODEL), lambda b, i: (0, 0)),
            pl.BlockSpec((1, D_MODEL), lambda b, i: (0, 0)),
        ],
        out_specs=[
            pl.BlockSpec((1, tm, D_MODEL), lambda b, i: (b, i, 0)),
            pl.BlockSpec((1, tm, D_MODEL), lambda b, i: (b, i, 0)),
        ],
        out_shape=[
            jax.ShapeDtypeStruct((B, L, D_MODEL), F32),
            jax.ShapeDtypeStruct((B, L, D_MODEL), BF16),
        ],
        compiler_params=_cparams(("arbitrary", "arbitrary")),
        name="mix_out",
    )(xa, xb, ret, hy_o, pool_o, wr, wh, wp, g)


def _ffn_kernel(h_ref, hp_ref, hn_ref, x1_ref, wup_ref, cw_ref, wd_ref, pa_ref, pb_ref, pw_ref, pg_ref,
                pn_ref, fin_ref, *o_refs, nbp, final):
    tm = h_ref.shape[1]
    i = pl.program_id(1)
    nt = pl.num_programs(1)

    hx = jnp.concatenate([h_ref[0], hp_ref[0], hn_ref[0]], axis=0)
    row = lax.broadcasted_iota(jnp.int32, (tm, FFN_SUB), 0)
    n_sub = D_FF // FFN_SUB

    def up_proj(s):
        c0 = s * FFN_SUB
        gate = jnp.dot(hx, wup_ref[:, c0:c0 + FFN_SUB], preferred_element_type=F32)
        up = jnp.dot(hx[:tm], wup_ref[:, D_FF + c0:D_FF + c0 + FFN_SUB], preferred_element_type=F32)
        return gate, up

    acc = x1_ref[0]
    nxt = up_proj(0)
    group, g0 = [], 0
    for s in range(n_sub):
        c0 = s * FFN_SUB
        gx, up = nxt
        if s + 1 < n_sub:
            nxt = up_proj(s + 1)
        gate = gx[:tm]
        g_prev = jnp.where(i == 0, 0.0, gx[tm + BF16_ROWS - 1:tm + BF16_ROWS])
        g_next = jnp.where(i == nt - 1, 0.0, gx[tm + BF16_ROWS:tm + BF16_ROWS + 1])
        g_up = jnp.where(row == 0, g_prev, pltpu.roll(gate, 1, axis=0))
        g_dn = jnp.where(row == tm - 1, g_next, pltpu.roll(gate, tm - 1, axis=0))
        cw = cw_ref[:, c0:c0 + FFN_SUB]
        gc = g_up * cw[0:1] + gate * cw[1:2] + g_dn * cw[2:3]
        group.append((jax.nn.gelu(gc) * up).astype(BF16))
        if len(group) == FFN_DOWN_GROUP or s == n_sub - 1:
            a = jnp.concatenate(group, axis=1)
            acc = acc + jnp.dot(a, wd_ref[g0:c0 + FFN_SUB, :], preferred_element_type=F32)
            group, g0 = [], c0 + FFN_SUB

    x2 = acc
    p = _pick(nbp, pa_ref, pb_ref).astype(BF16)
    e = _rms(jnp.dot(p, pw_ref[...], preferred_element_type=F32), pn_ref[...])
    gt = jax.nn.sigmoid(jnp.dot(x2.astype(BF16), pg_ref[...], preferred_element_type=F32))
    x3 = x2 + gt * e
    if not final:
        o_refs[0][0] = x3
        return
    y = _rms(x3, fin_ref[...])
    b = pl.program_id(0)

    @pl.when(b < nbp)
    def _():
        o_refs[0][0] = y

    @pl.when(b >= nbp)
    def _():
        o_refs[1][0] = y


def _ffn(h2, x1, w_up, cw, w_down, p, layer, ple_w, ple_gate, ple_norm, fin, tm, final):
    pa, pb, nbp, off = p
    B, L, _ = x1.shape
    nt = L // tm
    halo = BF16_ROWS
    nh = L // halo
    per = tm // halo
    full = lambda shape: pl.BlockSpec(shape, lambda b, i: (0,) * len(shape), pipeline_mode=pl.Buffered(1))
    tile = pl.BlockSpec((1, tm, D_MODEL), lambda b, i: (b, i, 0))
    if final:
        out_specs = list(_pair_specs(tm, D_MODEL, nt, nbp, 0))
        out_shape = [jax.ShapeDtypeStruct((nbp, L, D_MODEL), F32), jax.ShapeDtypeStruct((B - nbp, L, D_MODEL), F32)]
    else:
        out_specs = [tile]
        out_shape = [jax.ShapeDtypeStruct((B, L, D_MODEL), F32)]
    return pl.pallas_call(
        functools.partial(_ffn_kernel, nbp=nbp, final=final),
        grid=(B, nt),
        in_specs=[
            tile,
            pl.BlockSpec((1, halo, D_MODEL), lambda b, i: (b, jnp.maximum(i * per - 1, 0), 0)),
            pl.BlockSpec((1, halo, D_MODEL), lambda b, i: (b, jnp.minimum((i + 1) * per, nh - 1), 0)),
            tile,
            full((D_MODEL, 2 * D_FF)), full((3, D_FF)), full((D_FF, D_MODEL)),
            *_pair_specs(tm, PLE_DIM, nt, nbp, off, lead=layer),
            full((PLE_DIM, D_MODEL)), full((D_MODEL, D_MODEL)), full((1, D_MODEL)), full((1, D_MODEL)),
        ],
        out_specs=out_specs,
        out_shape=out_shape,
        compiler_params=_cparams(("arbitrary", "arbitrary")),
        name="ffn_ple",
    )(h2, h2, h2, x1, w_up, cw, w_down, pa, pb, ple_w, ple_gate, ple_norm, fin)


def _rope_tables(L):
    half = HEAD_DIM // 2
    inv = ROPE_THETA ** (-jnp.arange(half, dtype=F32) / half)
    ang = jnp.arange(L, dtype=F32)[:, None] * inv[None, :]
    cos, sin = jnp.cos(ang), jnp.sin(ang)
    return jnp.concatenate([cos, cos], axis=-1), jnp.concatenate([-sin, sin], axis=-1)


def _block_diag(pool_w):
    out = jnp.zeros((POOL_WIDTH, POOL_WIDTH), pool_w.dtype)
    for g in range(len(POOL_WINDOWS)):
        sl = slice(g * POOL_GROUP_DIM, (g + 1) * POOL_GROUP_DIM)
        out = out.at[sl, sl].set(pool_w[g])
    return out


def _layer(x, B, p, layer, lw, cos, sin, fin, tm, final):
    (norm_mix, w_in, dec_f, dec_b, ret_gn, hy_conv, hy_w1, hy_b1, hy_freq, hy_w2, hy_b2, hy_w3, hy_bias,
     pool_w, pool_scale, w_out, norm_ffn, w_up, ffn_conv, w_down, ple_w, ple_gate, ple_norm) = lw
    L = x[0].shape[1]
    R = RET_WIDTH

    w_q, w_k, w_v, w_g, w_hy, w_pool = jnp.split(w_in, [R, 2 * R, 3 * R, 4 * R, 4 * R + HY_IN_WIDTH], axis=1)
    wa = jnp.concatenate([w_q, w_v, w_g, w_pool], axis=1).astype(BF16)
    qvg, kt, hy_t, pool_in = _in_proj(x, B, norm_mix[None], wa, w_k.T.astype(BF16), w_hy.astype(BF16), hy_conv,
                                      cos, sin, cos.T, sin.T, WIDE_TILE * tm)

    dec = jnp.broadcast_to(jnp.stack([dec_f, dec_b], axis=1)[:, :, None], (RET_HEADS, 2, LANES)).astype(F32)
    ret = _retention(qvg, kt, dec, ret_gn[None])

    w1t = jnp.zeros((HY_HIDDEN, HY_HIDDEN), F32).at[:, :hy_w1.shape[0]].set(hy_w1.T)
    w3t = hy_w3.T.reshape(2, 2, HY_WIDTH, HY_HIDDEN)
    kk = _hyena_filters(w1t, hy_b1[:, None], hy_freq[:, None], hy_w2.T, hy_b2[:, None], w3t, L)
    bias = jnp.broadcast_to(hy_bias.reshape(2, HY_WIDTH // SUBLANES, SUBLANES).transpose(1, 0, 2)[..., None],
                            (HY_WIDTH // SUBLANES, 2, SUBLANES, HY_BLOCK))
    hy_o = _hyena(kk, hy_t, bias, B)

    pool_o = _pool(pool_in, _block_diag(pool_w).astype(BF16), pool_scale[None])

    wo = w_out.astype(BF16)
    x1, h2 = _mix_out(x, ret, hy_o, pool_o, wo[:R], wo[R:R + HY_WIDTH], wo[R + HY_WIDTH:], norm_ffn[None],
                      WIDE_TILE * tm)

    return _ffn(h2, x1, w_up.astype(BF16), ffn_conv, w_down.astype(BF16), p, layer, ple_w.astype(BF16),
                ple_gate.astype(BF16), ple_norm[None], fin, tm, final)


def _trunk(x_a, x_b, p_a, p_b, layer_weights, norm_final, tm):
    depth = p_a.shape[0]
    nbp, L = x_a.shape[0], x_a.shape[1]
    B = nbp + x_b.shape[0]
    cos, sin = _rope_tables(L)
    fin = norm_final[None]
    x = (x_a, x_b, nbp, 0)
    p = (p_a, p_b, nbp, 0)
    for i in range(depth):
        out = _layer(x, B, p, i, [w[i] for w in layer_weights], cos, sin, fin, tm, i == depth - 1)
        x = (out[0], out[0], nbp, nbp)
    return out[0], out[1]


def kernel(x_prompt, x_sample, p_prompt, p_sample, norm_mix, w_in, ret_decay_fwd, ret_decay_bwd, ret_gn,
           hy_short_conv, hy_w1, hy_b1, hy_freq, hy_w2, hy_b2, hy_w3, hy_bias, pool_w, pool_scale, w_out,
           norm_ffn, ffn_w_up, ffn_conv, ffn_w_down, ple_w, ple_gate_w, ple_norm, norm_final):
    layer_weights = (norm_mix, w_in, ret_decay_fwd, ret_decay_bwd, ret_gn, hy_short_conv, hy_w1, hy_b1,
                     hy_freq, hy_w2, hy_b2, hy_w3, hy_bias, pool_w, pool_scale, w_out, norm_ffn, ffn_w_up,
                     ffn_conv, ffn_w_down, ple_w, ple_gate_w, ple_norm)
    return _trunk(x_prompt, x_sample, p_prompt, p_sample, layer_weights, norm_final, ROW_TILE)
```

```python
import functools
import math

import jax
import jax.numpy as jnp
from jax import lax
from jax.experimental import pallas as pl
from jax.experimental.pallas import tpu as pltpu

F32 = jnp.float32
BF16 = jnp.bfloat16

D_MODEL = 1024
RET_WIDTH = 512
RET_HEADS = 4
HEAD_DIM = 128
HY_WIDTH = 256
POOL_WIDTH = 256
POOL_GROUP_DIM = 64
POOL_WINDOWS = (2, 4, 8, 16)
POOL_PAD = 16
D_FF = 2816
PLE_DIM = 256
CHUNK = 128
ROPE_THETA = 10000.0
HY_EMB_BANDS = 16
HY_HIDDEN = 64
HY_MIN_DECAY = math.log(1e-2) / 1.5
HY_MAX_DECAY = math.log(1e-2) / 0.3
EPS = 1e-6

LANES = 128
SUBLANES = 8
BF16_ROWS = 16
MXU_DIM = 256
VMEM_LIMIT = 56 * 1024 * 1024

HY_BLOCK = MXU_DIM
HY_PAIR_UNROLL = 1
QVG_WIDTH = 3 * RET_WIDTH
RET_UNROLL = True
HY_IN_WIDTH = 3 * HY_WIDTH
ROW_TILE = 512
WIDE_TILE = 2
FFN_SUB = MXU_DIM
FFN_DOWN_GROUP = 4


def _cparams(sem):
    return pltpu.CompilerParams(dimension_semantics=sem, vmem_limit_bytes=VMEM_LIMIT)


def _rms(x, g):
    return x * lax.rsqrt(jnp.mean(x * x, axis=-1, keepdims=True) + EPS) * g


def _pair_specs(tm, width, nt, nbp, off, lead=None):
    pre = () if lead is None else (lead,)
    shape = (1, tm, width) if lead is None else (None, 1, tm, width)

    def first(b, i):
        return pre + (jnp.minimum(b, nbp - 1), jnp.where(b < nbp, i, nt - 1), 0)

    def second(b, i):
        return pre + (jnp.maximum(b - nbp, 0) + off, jnp.where(b < nbp, 0, i), 0)

    return pl.BlockSpec(shape, first), pl.BlockSpec(shape, second)


def _pick(nbp, a_ref, b_ref):
    return jnp.where(pl.program_id(0) < nbp, a_ref[0], b_ref[0])


def _pair_halo_specs(rows, width, tm, L, nbp, off):
    per, last = tm // rows, L // rows - 1

    def make(first, after):
        def index(b, i):
            r = jnp.minimum((i + 1) * per, last) if after else jnp.maximum(i * per - 1, 0)
            if first:
                return (jnp.minimum(b, nbp - 1), jnp.where(b < nbp, r, 0), 0)
            return (jnp.maximum(b - nbp, 0) + off, jnp.where(b < nbp, 0, r), 0)
        return pl.BlockSpec((1, rows, width), index)

    return make(True, False), make(False, False), make(True, True), make(False, True)


def _dwconv3_rows(u, tm, cw, first_tile, last_tile):
    main = u[:tm]
    prev = jnp.where(first_tile, 0.0, u[tm + BF16_ROWS - 1:tm + BF16_ROWS])
    nxt = jnp.where(last_tile, 0.0, u[tm + BF16_ROWS:tm + BF16_ROWS + 1])
    row = lax.broadcasted_iota(jnp.int32, main.shape, 0)
    up = jnp.where(row == 0, prev, pltpu.roll(main, 1, axis=0))
    dn = jnp.where(row == tm - 1, nxt, pltpu.roll(main, tm - 1, axis=0))
    return up * cw[0:1] + main * cw[1:2] + dn * cw[2:3]


def _in_proj_kernel(xa_ref, xb_ref, xpa_ref, xpb_ref, xna_ref, xnb_ref, g_ref, wa_ref, wk_ref, wh_ref, cw_ref,
                    cos_ref, sin_ref, cost_ref, sint_ref, qvg_ref, kt_ref, hy_ref, pool_ref, *, nbp):
    tm = xa_ref.shape[1]
    i = pl.program_id(1)
    g = g_ref[...]
    h = _rms(_pick(nbp, xa_ref, xb_ref), g).astype(BF16)
    hx = jnp.concatenate([h, _rms(_pick(nbp, xpa_ref, xpb_ref), g).astype(BF16),
                          _rms(_pick(nbp, xna_ref, xnb_ref), g).astype(BF16)], axis=0)
    u = jnp.dot(hx, wh_ref[...], preferred_element_type=F32)
    y = _dwconv3_rows(u, tm, cw_ref[...], i == 0, i == pl.num_programs(1) - 1)
    yt = y.T
    for j in range(tm // LANES):
        hy_ref[:, j, :, :] = yt[:, j * LANES:(j + 1) * LANES].reshape(HY_IN_WIDTH // SUBLANES, SUBLANES, LANES)
    kt = lax.dot_general(wk_ref[...], h, (((1,), (1,)), ((), ())), preferred_element_type=F32)
    cos_t, sin_t = cost_ref[...], sint_ref[...]
    for hd in range(RET_HEADS):
        kh = kt[hd * HEAD_DIM:(hd + 1) * HEAD_DIM]
        kt_ref[0, hd * HEAD_DIM:(hd + 1) * HEAD_DIM, :] = (
            kh * cos_t + pltpu.roll(kh, HEAD_DIM // 2, axis=0) * sin_t).astype(BF16)
    a = jnp.dot(h, wa_ref[...], preferred_element_type=F32)
    cos, sin = cos_ref[...], sin_ref[...]
    for hd in range(RET_HEADS):
        qh = a[:, hd * HEAD_DIM:(hd + 1) * HEAD_DIM]
        qvg_ref[0, :, hd * HEAD_DIM:(hd + 1) * HEAD_DIM] = (
            qh * cos + pltpu.roll(qh, HEAD_DIM // 2, axis=1) * sin).astype(BF16)
    qvg_ref[0, :, RET_WIDTH:] = a[:, RET_WIDTH:QVG_WIDTH].astype(BF16)
    pool_ref[0] = a[:, QVG_WIDTH:].astype(BF16)


def _in_proj(x, B, g, wa, wk_t, wh, cw, cos, sin, cos_t, sin_t, tm):
    xa, xb, nbp, off = x
    L = xa.shape[1]
    nt = L // tm
    const = lambda shape: pl.BlockSpec(shape, lambda b, i: (0,) * len(shape))
    return pl.pallas_call(
        functools.partial(_in_proj_kernel, nbp=nbp),
        grid=(B, nt),
        in_specs=[
            *_pair_specs(tm, D_MODEL, nt, nbp, off),
            *_pair_halo_specs(BF16_ROWS, D_MODEL, tm, L, nbp, off),
            const((1, D_MODEL)),
            const((D_MODEL, QVG_WIDTH + POOL_WIDTH)),
            const((RET_WIDTH, D_MODEL)),
            const((D_MODEL, HY_IN_WIDTH)),
            const((3, HY_IN_WIDTH)),
            pl.BlockSpec((tm, HEAD_DIM), lambda b, i: (i, 0)),
            pl.BlockSpec((tm, HEAD_DIM), lambda b, i: (i, 0)),
            pl.BlockSpec((HEAD_DIM, tm), lambda b, i: (0, i)),
            pl.BlockSpec((HEAD_DIM, tm), lambda b, i: (0, i)),
        ],
        out_specs=[
            pl.BlockSpec((1, tm, QVG_WIDTH), lambda b, i: (b, i, 0)),
            pl.BlockSpec((1, RET_WIDTH, tm), lambda b, i: (b, 0, i)),
            pl.BlockSpec((HY_IN_WIDTH // SUBLANES, tm // LANES, SUBLANES, LANES), lambda b, i: (0, i, b, 0)),
            pl.BlockSpec((1, tm, POOL_WIDTH), lambda b, i: (b, i, 0)),
        ],
        out_shape=[
            jax.ShapeDtypeStruct((B, L, QVG_WIDTH), BF16),
            jax.ShapeDtypeStruct((B, RET_WIDTH, L), BF16),
            jax.ShapeDtypeStruct((HY_IN_WIDTH // SUBLANES, L // LANES, B * SUBLANES, LANES), F32),
            jax.ShapeDtypeStruct((B, L, POOL_WIDTH), BF16),
        ],
        compiler_params=_cparams(("arbitrary", "arbitrary")),
        name="in_proj",
    )(xa, xb, xa, xb, xa, xb, g, wa, wk_t, wh, cw, cos, sin, cos_t, sin_t)


def _log_sigmoid(x):
    return jnp.minimum(x, 0.0) - jnp.log(1.0 + jnp.exp(-jnp.abs(x)))


def _retention_kernel(dec_ref, gn_ref, q_ref, kt_ref, v_ref, g_ref, o_ref, kv_s, st_s, tab_s):
    L = q_ref.shape[1]
    n_chunks = L // CHUNK
    C = CHUNK
    scale = HEAD_DIM ** -0.5

    lg = _log_sigmoid(dec_ref[0])
    lgf, lgb = lg[0:1], lg[1:2]
    ri = lax.broadcasted_iota(jnp.int32, (C, C), 0)
    ci = lax.broadcasted_iota(jnp.int32, (C, C), 1)
    rf = ri.astype(F32)
    cf = ci.astype(F32)
    dist = jnp.abs(ri - ci).astype(F32)
    tab_s[0] = jnp.exp(dist * jnp.where(ri >= ci, lgf, lgb)) * scale
    tab_s[1] = jnp.exp((rf + 1.0) * lgf) * scale
    tab_s[2] = jnp.exp((C - rf) * lgb) * scale
    tab_s[3] = jnp.exp((C - 1.0 - cf) * lgf)
    tab_s[4] = jnp.exp(cf * lgb)
    dec_f = jnp.exp(C * lgf)
    dec_b = jnp.exp(C * lgb)

    def summaries(n, carry):
        c0 = pl.multiple_of(n * C, C)
        kt = kt_ref[0, :, pl.ds(c0, C)].astype(F32)
        lhs = jnp.concatenate([kt * tab_s[3], kt * tab_s[4]], axis=0).astype(BF16)
        kv_s[n] = jnp.dot(lhs, v_ref[0, pl.ds(c0, C), :], preferred_element_type=F32)
        return carry

    lax.fori_loop(0, n_chunks, summaries, 0, unroll=RET_UNROLL)

    def scan(i, carry):
        sf, sb = carry
        m = n_chunks - 1 - i
        st_s[i, :C, :] = sf.astype(BF16)
        st_s[m, C:, :] = sb.astype(BF16)
        return sf * dec_f + kv_s[i, :C, :], sb * dec_b + kv_s[m, C:, :]

    zero = jnp.zeros((C, C), F32)
    lax.fori_loop(0, n_chunks, scan, (zero, zero), unroll=RET_UNROLL)

    gain = gn_ref[...]

    def outputs(n, carry):
        c0 = pl.multiple_of(n * C, C)
        qb = q_ref[0, pl.ds(c0, C), :]
        qn = qb.astype(F32)
        vn = v_ref[0, pl.ds(c0, C), :]
        sc = jnp.dot(qb, kt_ref[0, :, pl.ds(c0, C)], preferred_element_type=F32)
        lhs = jnp.concatenate([sc * tab_s[0], qn * tab_s[1], qn * tab_s[2]], axis=1).astype(BF16)
        rhs = jnp.concatenate([vn, st_s[n]], axis=0)
        o = jnp.dot(lhs, rhs, preferred_element_type=F32)
        mu = jnp.mean(o, axis=-1, keepdims=True)
        oc = o - mu
        var = jnp.mean(oc * oc, axis=-1, keepdims=True)
        gate = g_ref[0, pl.ds(c0, C), :].astype(F32)
        y = oc * lax.rsqrt(var + EPS) * gain * (gate * jax.nn.sigmoid(gate))
        o_ref[0, pl.ds(c0, C), :] = y.astype(BF16)
        return carry

    lax.fori_loop(0, n_chunks, outputs, 0, unroll=RET_UNROLL)


def _retention(qvg, kt, dec, gn):
    B, L, _ = qvg.shape
    H = RET_HEADS

    def col(off):
        return pl.BlockSpec((1, L, HEAD_DIM), lambda b, h: (b, 0, off + h))

    return pl.pallas_call(
        _retention_kernel,
        grid=(B, H),
        in_specs=[
            pl.BlockSpec((1, 2, LANES), lambda b, h: (h, 0, 0)),
            pl.BlockSpec((1, HEAD_DIM), lambda b, h: (0, h)),
            col(0),
            pl.BlockSpec((1, HEAD_DIM, L), lambda b, h: (b, h, 0)),
            col(H), col(2 * H),
        ],
        out_specs=pl.BlockSpec((1, L, HEAD_DIM), lambda b, h: (b, 0, h)),
        out_shape=jax.ShapeDtypeStruct((B, L, RET_WIDTH), BF16),
        scratch_shapes=[
            pltpu.VMEM((L // CHUNK, 2 * CHUNK, CHUNK), F32),
            pltpu.VMEM((L // CHUNK, 2 * CHUNK, CHUNK), BF16),
            pltpu.VMEM((5, CHUNK, CHUNK), F32),
        ],
        compiler_params=_cparams(("parallel", "parallel")),
        name="retention",
    )(dec, gn, qvg, kt, qvg, qvg)


def _filter_kernel(w1_ref, b1_ref, fr_ref, w2_ref, b2_ref, w3_ref, kk_ref, hid_s):
    L = kk_ref.shape[1] // 2
    hi = lax.Precision.HIGHEST

    @pl.when(pl.program_id(0) == 0)
    def _():
        r = lax.broadcasted_iota(jnp.int32, (HY_HIDDEN, L), 0)
        lane = lax.broadcasted_iota(jnp.int32, (HY_HIDDEN, L), 1)
        band_idx = jnp.where(r <= HY_EMB_BANDS, r - 1, r - 1 - HY_EMB_BANDS).astype(F32)
        band = 1e-4 + band_idx * ((HY_EMB_BANDS - 1 - 1e-4) / (HY_EMB_BANDS - 1))
        fr = fr_ref[...]

        def hidden(s_int):
            s = s_int.astype(F32)
            t = s / (L - 1.0)
            ang = (2.0 * math.pi / L) * s * band
            feats = jnp.where(r == 0, t,
                              jnp.where(r <= HY_EMB_BANDS, jnp.cos(ang),
                                        jnp.where(r <= 2 * HY_EMB_BANDS, -jnp.sin(ang), 0.0)))
            h = jnp.sin(fr * (jnp.dot(w1_ref[...], feats, precision=hi, preferred_element_type=F32) + b1_ref[...]))
            return jnp.sin(fr * (jnp.dot(w2_ref[...], h, precision=hi, preferred_element_type=F32) + b2_ref[...]))

        hid_s[0] = hidden(lane)
        hid_s[1] = hidden(L - lane)

    crow = lax.broadcasted_iota(jnp.int32, (HY_WIDTH, L), 0).astype(F32)
    delta = jnp.abs(HY_MIN_DECAY + crow * ((HY_MAX_DECAY - HY_MIN_DECAY) / (HY_WIDTH - 1)))
    clane = lax.broadcasted_iota(jnp.int32, (HY_WIDTH, L), 1)

    def taps(hid, s_row_int, w3):
        h = jnp.dot(w3, hid, precision=hi, preferred_element_type=F32)
        return h * jnp.exp(-(s_row_int.astype(F32) / (L - 1.0)) * delta)

    h_f = taps(hid_s[0], clane, w3_ref[0, 0])
    h_b = taps(hid_s[1], L - clane, w3_ref[0, 1])
    h_b = jnp.where(clane == 0, 0.0, h_b)
    norm = jnp.sum(jnp.abs(h_f), axis=-1, keepdims=True) + jnp.sum(jnp.abs(h_b), axis=-1, keepdims=True)
    inv = 1.0 / norm
    kk = jnp.concatenate([h_b * inv, h_f * inv], axis=1)
    bits = pltpu.bitcast(kk, jnp.uint32)
    bf = (bits + jnp.uint32(0x7FFF) + ((bits >> 16) & jnp.uint32(1))) >> 16
    lane2 = lax.broadcasted_iota(jnp.int32, bf.shape, 1)
    prev = jnp.where(lane2 == 0, jnp.uint32(0), pltpu.roll(bf, 1, axis=1))
    kk_ref[...] = bf | (prev << 16)


def _hyena_filters(w1t, b1, fr, w2t, b2, w3t, L):
    small = lambda shape: pl.BlockSpec(shape, lambda o: (0,) * len(shape))
    return pl.pallas_call(
        _filter_kernel,
        grid=(2,),
        in_specs=[
            small((HY_HIDDEN, HY_HIDDEN)), small((HY_HIDDEN, 1)), small((HY_HIDDEN, 1)),
            small((HY_HIDDEN, HY_HIDDEN)), small((HY_HIDDEN, 1)),
            pl.BlockSpec((1, 2, HY_WIDTH, HY_HIDDEN), lambda o: (o, 0, 0, 0)),
        ],
        out_specs=pl.BlockSpec((HY_WIDTH, 2 * L), lambda o: (o, 0)),
        out_shape=jax.ShapeDtypeStruct((2 * HY_WIDTH, 2 * L), jnp.uint32),
        scratch_shapes=[pltpu.VMEM((2, HY_HIDDEN, L), F32)],
        compiler_params=_cparams(("arbitrary",)),
        name="hyena_filters",
    )(w1t, b1, fr, w2t, b2, w3t)


def _hyena_kernel(kk1_ref, kk2_ref, xv_ref, x1_ref, x2_ref, bias_ref, out_ref,
                  toep1_s, toep2_s, u2_s, x1u2_s, y2_s, *, B, BP):
    n_tiles = xv_ref.shape[0]
    L = n_tiles * LANES
    T = HY_BLOCK
    NB = L // T
    TPB = T // LANES

    @pl.when(pl.program_id(0) == 0)
    def _():
        toep1_s[...] = jnp.zeros_like(toep1_s)
        toep2_s[...] = jnp.zeros_like(toep2_s)
        u2_s[...] = jnp.zeros_like(u2_s)
        x1u2_s[...] = jnp.zeros_like(x1u2_s)

    def build_toeplitz(kk_ref, c, toep_s):
        row = jnp.broadcast_to(kk_ref[pl.ds(c, 1), :], (SUBLANES, 2 * L))
        for g in range(LANES // BF16_ROWS):
            rolled = pltpu.roll(row, BF16_ROWS * g, axis=1, stride=2, stride_axis=0)
            tile = pltpu.bitcast(rolled, BF16)
            toep_s[BF16_ROWS * g:BF16_ROWS * (g + 1), :] = tile

    def long_conv(toep_s, u_s, y_s):
        y_s[...] = jnp.zeros_like(y_s)
        for d in range(-(NB - 1), NB):
            n = NB - abs(d)
            src = max(0, -d) * BP
            dst = max(0, d) * BP
            w0 = L + d * T
            m = jnp.concatenate([toep_s[:, w0:w0 + T], toep_s[:, w0 - LANES:w0 - LANES + T]], axis=0)
            lhs = u_s[src:src + n * BP, :].astype(BF16)
            y_s[dst:dst + n * BP, :] += jnp.dot(lhs, m, preferred_element_type=F32)

    def channel(c, k):
        u_s, x1u_s, y_s = u2_s.at[k], x1u2_s.at[k], y2_s.at[k]
        rows = pl.ds(c, B, stride=SUBLANES)
        for j in range(n_tiles):
            J, l0 = j // TPB, (j % TPB) * LANES
            u_s[J * BP:J * BP + B, l0:l0 + LANES] = xv_ref.at[j][rows, :]
            x1u_s[J * BP:J * BP + B, l0:l0 + LANES] = x1_ref.at[j][rows, :]
        b0 = bias_ref[0, 0, pl.ds(c, 1), :]
        b1 = bias_ref[0, 1, pl.ds(c, 1), :]
        build_toeplitz(kk2_ref, c, toep2_s)
        long_conv(toep1_s, u_s, y_s)
        u_s[...] = x1u_s[...] * (y_s[...] + u_s[...] * b0)
        build_toeplitz(kk1_ref, jnp.minimum(c + 1, SUBLANES - 1), toep1_s)
        long_conv(toep2_s, u_s, y_s)
        for j in range(n_tiles):
            J, l0 = j // TPB, (j % TPB) * LANES
            yy = y_s[J * BP:J * BP + B, l0:l0 + LANES] + u_s[J * BP:J * BP + B, l0:l0 + LANES] * b1[:, :LANES]
            out_ref.at[j][rows, :] = x2_ref.at[j][rows, :] * yy

    def channel_pair(i, carry):
        channel(2 * i, 0)
        channel(2 * i + 1, 1)
        return carry

    build_toeplitz(kk1_ref, 0, toep1_s)
    lax.fori_loop(0, SUBLANES // 2, channel_pair, 0, unroll=HY_PAIR_UNROLL)


def _hyena(kk, hy_t, bias, B):
    n_in, n_tiles, R, _ = hy_t.shape
    L = n_tiles * LANES
    NB = L // HY_BLOCK
    BP = -(-B // SUBLANES) * SUBLANES
    n_c = HY_WIDTH // SUBLANES

    def sect(s):
        return pl.BlockSpec((None, n_tiles, R, LANES), lambda c: (s * n_c + c, 0, 0, 0))

    return pl.pallas_call(
        functools.partial(_hyena_kernel, B=B, BP=BP),
        grid=(n_c,),
        in_specs=[
            pl.BlockSpec((SUBLANES, 2 * L), lambda c: (c, 0)),
            pl.BlockSpec((SUBLANES, 2 * L), lambda c: (n_c + c, 0)),
            sect(0), sect(1), sect(2),
            pl.BlockSpec((1, 2, SUBLANES, HY_BLOCK), lambda c: (c, 0, 0, 0)),
        ],
        out_specs=pl.BlockSpec((None, n_tiles, R, LANES), lambda c: (c, 0, 0, 0)),
        out_shape=jax.ShapeDtypeStruct((n_c, n_tiles, R, LANES), F32),
        scratch_shapes=[
            pltpu.VMEM((LANES, 2 * L), BF16), pltpu.VMEM((LANES, 2 * L), BF16),
            pltpu.VMEM((2, NB * BP, HY_BLOCK), F32),
            pltpu.VMEM((2, NB * BP, HY_BLOCK), F32),
            pltpu.VMEM((2, NB * BP, HY_BLOCK), F32),
        ],
        compiler_params=_cparams(("arbitrary",)),
        name="hyena",
    )(kk, kk, hy_t, hy_t, hy_t, bias)


def _pool_kernel(u_ref, inv_ref, w_ref, sc_ref, o_ref):
    L = u_ref.shape[1]
    u = u_ref[0].astype(F32)
    pad = jnp.zeros((POOL_PAD, POOL_WIDTH), F32)
    up = jnp.concatenate([pad, u, pad], axis=0)
    n = L + 2 * POOL_PAD
    left = lax.broadcasted_iota(jnp.int32, (L, LANES), 1) < POOL_GROUP_DIM

    def both(x, k):
        return pltpu.roll(x, k, axis=0) + pltpu.roll(x, n - k, axis=0)

    def windows(x, levels):
        q = pltpu.roll(x, 1, axis=0) + x
        out = [q]
        for k in (1, 2, 4)[:levels - 1]:
            q = both(q, k)
            out.append(q)
        return [w[POOL_PAD:POOL_PAD + L] for w in out]

    p2, p4 = windows(up[:, :LANES], 2)
    _, _, p8, p16 = windows(up[:, LANES:], 4)
    total = jnp.concatenate([jnp.where(left, p2, p4), jnp.where(left, p8, p16)], axis=1)
    d = (total * inv_ref[...] - u).astype(BF16)
    o_ref[0] = (jnp.dot(d, w_ref[...], preferred_element_type=F32) * sc_ref[...]).astype(BF16)


def _pool_inv_counts(L):
    t = jnp.arange(L)[:, None]
    half = jnp.repeat(jnp.asarray([w // 2 for w in POOL_WINDOWS]), POOL_GROUP_DIM)[None, :]
    cnt = jnp.minimum(t + half - 1, L - 1) - jnp.maximum(t - half, 0) + 1
    return 1.0 / cnt.astype(F32)


def _pool(pool_in, w_bd, scale):
    B, L, _ = pool_in.shape
    return pl.pallas_call(
        _pool_kernel,
        grid=(B,),
        in_specs=[
            pl.BlockSpec((1, L, POOL_WIDTH), lambda b: (b, 0, 0)),
            pl.BlockSpec((L, POOL_WIDTH), lambda b: (0, 0)),
            pl.BlockSpec((POOL_WIDTH, POOL_WIDTH), lambda b: (0, 0)),
            pl.BlockSpec((1, POOL_WIDTH), lambda b: (0, 0)),
        ],
        out_specs=pl.BlockSpec((1, L, POOL_WIDTH), lambda b: (b, 0, 0)),
        out_shape=jax.ShapeDtypeStruct((B, L, POOL_WIDTH), BF16),
        compiler_params=_cparams(("parallel",)),
        name="pool_mixer",
    )(pool_in, _pool_inv_counts(L), w_bd, scale)


def _mix_out_kernel(xa_ref, xb_ref, ret_ref, hy_ref, pool_ref, wr_ref, wh_ref, wp_ref, g_ref, x1_ref, h2_ref, *, nbp):
    tm = xa_ref.shape[1]
    hy_t = jnp.concatenate([hy_ref[:, j].reshape(HY_WIDTH, LANES) for j in range(tm // LANES)],
                           axis=1).astype(BF16)
    mix = jnp.dot(ret_ref[0], wr_ref[...], preferred_element_type=F32)
    mix += lax.dot_general(hy_t, wh_ref[...], (((0,), (0,)), ((), ())), preferred_element_type=F32)
    mix += jnp.dot(pool_ref[0], wp_ref[...], preferred_element_type=F32)
    x1 = _pick(nbp, xa_ref, xb_ref) + mix
    x1_ref[0] = x1
    h2_ref[0] = _rms(x1, g_ref[...]).astype(BF16)


def _mix_out(x, ret, hy_o, pool_o, wr, wh, wp, g, tm):
    xa, xb, nbp, off = x
    B, L, _ = ret.shape
    nt = L // tm
    return pl.pallas_call(
        functools.partial(_mix_out_kernel, nbp=nbp),
        grid=(B, nt),
        in_specs=[
            *_pair_specs(tm, D_MODEL, nt, nbp, off),
            pl.BlockSpec((1, tm, RET_WIDTH), lambda b, i: (b, i, 0)),
            pl.BlockSpec((HY_WIDTH // SUBLANES, tm // LANES, SUBLANES, LANES), lambda b, i: (0, i, b, 0)),
            pl.BlockSpec((1, tm, POOL_WIDTH), lambda b, i: (b, i, 0)),
            pl.BlockSpec((RET_WIDTH, D_MODEL), lambda b, i: (0, 0)),
            pl.BlockSpec((HY_WIDTH, D_MODEL), lambda b, i: (0, 0)),
            pl.BlockSpec((POOL_WIDTH, D_MODEL), lambda b, i: (0, 0)),
            pl.BlockSpec((1, D_MODEL), lambda b, i: (0, 0)),
        ],
        out_specs=[
            pl.BlockSpec((1, tm, D_MODEL), lambda b, i: (b, i, 0)),
            pl.BlockSpec((1, tm, D_MODEL), lambda b, i: (b, i, 0)),
        ],
        out_shape=[
            jax.ShapeDtypeStruct((B, L, D_MODEL), F32),
            jax.ShapeDtypeStruct((B, L, D_MODEL), BF16),
        ],
        compiler_params=_cparams(("arbitrary", "arbitrary")),
        name="mix_out",
    )(xa, xb, ret, hy_o, pool_o, wr, wh, wp, g)


def _ffn_kernel(h_ref, hp_ref, hn_ref, x1_ref, wup_ref, cw_ref, wd_ref, pa_ref, pb_ref, pw_ref, pg_ref,
                pn_ref, fin_ref, *o_refs, nbp, final):
    tm = h_ref.shape[1]
    i = pl.program_id(1)
    nt = pl.num_programs(1)

    hx = jnp.concatenate([h_ref[0], hp_ref[0], hn_ref[0]], axis=0)
    row = lax.broadcasted_iota(jnp.int32, (tm, FFN_SUB), 0)
    n_sub = D_FF // FFN_SUB

    def up_proj(s):
        c0 = s * FFN_SUB
        gate = jnp.dot(hx, wup_ref[:, c0:c0 + FFN_SUB], preferred_element_type=F32)
        up = jnp.dot(hx[:tm], wup_ref[:, D_FF + c0:D_FF + c0 + FFN_SUB], preferred_element_type=F32)
        return gate, up

    acc = x1_ref[0]
    nxt = up_proj(0)
    group, g0 = [], 0
    for s in range(n_sub):
        c0 = s * FFN_SUB
        gx, up = nxt
        if s + 1 < n_sub:
            nxt = up_proj(s + 1)
        gate = gx[:tm]
        g_prev = jnp.where(i == 0, 0.0, gx[tm + BF16_ROWS - 1:tm + BF16_ROWS])
        g_next = jnp.where(i == nt - 1, 0.0, gx[tm + BF16_ROWS:tm + BF16_ROWS + 1])
        g_up = jnp.where(row == 0, g_prev, pltpu.roll(gate, 1, axis=0))
        g_dn = jnp.where(row == tm - 1, g_next, pltpu.roll(gate, tm - 1, axis=0))
        cw = cw_ref[:, c0:c0 + FFN_SUB]
        gc = g_up * cw[0:1] + gate * cw[1:2] + g_dn * cw[2:3]
        group.append((jax.nn.gelu(gc) * up).astype(BF16))
        if len(group) == FFN_DOWN_GROUP or s == n_sub - 1:
            a = jnp.concatenate(group, axis=1)
            acc = acc + jnp.dot(a, wd_ref[g0:c0 + FFN_SUB, :], preferred_element_type=F32)
            group, g0 = [], c0 + FFN_SUB

    x2 = acc
    p = _pick(nbp, pa_ref, pb_ref).astype(BF16)
    e = _rms(jnp.dot(p, pw_ref[...], preferred_element_type=F32), pn_ref[...])
    gt = jax.nn.sigmoid(jnp.dot(x2.astype(BF16), pg_ref[...], preferred_element_type=F32))
    x3 = x2 + gt * e
    if not final:
        o_refs[0][0] = x3
        return
    y = _rms(x3, fin_ref[...])
    b = pl.program_id(0)

    @pl.when(b < nbp)
    def _():
        o_refs[0][0] = y

    @pl.when(b >= nbp)
    def _():
        o_refs[1][0] = y


def _ffn(h2, x1, w_up, cw, w_down, p, layer, ple_w, ple_gate, ple_norm, fin, tm, final):
    pa, pb, nbp, off = p
    B, L, _ = x1.shape
    nt = L // tm
    halo = BF16_ROWS
    nh = L // halo
    per = tm // halo
    full = lambda shape: pl.BlockSpec(shape, lambda b, i: (0,) * len(shape), pipeline_mode=pl.Buffered(1))
    tile = pl.BlockSpec((1, tm, D_MODEL), lambda b, i: (b, i, 0))
    if final:
        out_specs = list(_pair_specs(tm, D_MODEL, nt, nbp, 0))
        out_shape = [jax.ShapeDtypeStruct((nbp, L, D_MODEL), F32), jax.ShapeDtypeStruct((B - nbp, L, D_MODEL), F32)]
    else:
        out_specs = [tile]
        out_shape = [jax.ShapeDtypeStruct((B, L, D_MODEL), F32)]
    return pl.pallas_call(
        functools.partial(_ffn_kernel, nbp=nbp, final=final),
        grid=(B, nt),
        in_specs=[
            tile,
            pl.BlockSpec((1, halo, D_MODEL), lambda b, i: (b, jnp.maximum(i * per - 1, 0), 0)),
            pl.BlockSpec((1, halo, D_MODEL), lambda b, i: (b, jnp.minimum((i + 1) * per, nh - 1), 0)),
            tile,
            full((D_MODEL, 2 * D_FF)), full((3, D_FF)), full((D_FF, D_MODEL)),
            *_pair_specs(tm, PLE_DIM, nt, nbp, off, lead=layer),
            full((PLE_DIM, D_MODEL)), full((D_MODEL, D_MODEL)), full((1, D_MODEL)), full((1, D_MODEL)),
        ],
        out_specs=out_specs,
        out_shape=out_shape,
        compiler_params=_cparams(("arbitrary", "arbitrary")),
        name="ffn_ple",
    )(h2, h2, h2, x1, w_up, cw, w_down, pa, pb, ple_w, ple_gate, ple_norm, fin)


def _rope_tables(L):
    half = HEAD_DIM // 2
    inv = ROPE_THETA ** (-jnp.arange(half, dtype=F32) / half)
    ang = jnp.arange(L, dtype=F32)[:, None] * inv[None, :]
    cos, sin = jnp.cos(ang), jnp.sin(ang)
    return jnp.concatenate([cos, cos], axis=-1), jnp.concatenate([-sin, sin], axis=-1)


def _block_diag(pool_w):
    out = jnp.zeros((POOL_WIDTH, POOL_WIDTH), pool_w.dtype)
    for g in range(len(POOL_WINDOWS)):
        sl = slice(g * POOL_GROUP_DIM, (g + 1) * POOL_GROUP_DIM)
        out = out.at[sl, sl].set(pool_w[g])
    return out


def _layer(x, B, p, layer, lw, cos, sin, fin, tm, final):
    (norm_mix, w_in, dec_f, dec_b, ret_gn, hy_conv, hy_w1, hy_b1, hy_freq, hy_w2, hy_b2, hy_w3, hy_bias,
     pool_w, pool_scale, w_out, norm_ffn, w_up, ffn_conv, w_down, ple_w, ple_gate, ple_norm) = lw
    L = x[0].shape[1]
    R = RET_WIDTH

    w_q, w_k, w_v, w_g, w_hy, w_pool = jnp.split(w_in, [R, 2 * R, 3 * R, 4 * R, 4 * R + HY_IN_WIDTH], axis=1)
    wa = jnp.concatenate([w_q, w_v, w_g, w_pool], axis=1).astype(BF16)
    qvg, kt, hy_t, pool_in = _in_proj(x, B, norm_mix[None], wa, w_k.T.astype(BF16), w_hy.astype(BF16), hy_conv,
                                      cos, sin, cos.T, sin.T, WIDE_TILE * tm)

    dec = jnp.broadcast_to(jnp.stack([dec_f, dec_b], axis=1)[:, :, None], (RET_HEADS, 2, LANES)).astype(F32)
    ret = _retention(qvg, kt, dec, ret_gn[None])

    w1t = jnp.zeros((HY_HIDDEN, HY_HIDDEN), F32).at[:, :hy_w1.shape[0]].set(hy_w1.T)
    w3t = hy_w3.T.reshape(2, 2, HY_WIDTH, HY_HIDDEN)
    kk = _hyena_filters(w1t, hy_b1[:, None], hy_freq[:, None], hy_w2.T, hy_b2[:, None], w3t, L)
    bias = jnp.broadcast_to(hy_bias.reshape(2, HY_WIDTH // SUBLANES, SUBLANES).transpose(1, 0, 2)[..., None],
                            (HY_WIDTH // SUBLANES, 2, SUBLANES, HY_BLOCK))
    hy_o = _hyena(kk, hy_t, bias, B)

    pool_o = _pool(pool_in, _block_diag(pool_w).astype(BF16), pool_scale[None])

    wo = w_out.astype(BF16)
    x1, h2 = _mix_out(x, ret, hy_o, pool_o, wo[:R], wo[R:R + HY_WIDTH], wo[R + HY_WIDTH:], norm_ffn[None],
                      WIDE_TILE * tm)

    return _ffn(h2, x1, w_up.astype(BF16), ffn_conv, w_down.astype(BF16), p, layer, ple_w.astype(BF16),
                ple_gate.astype(BF16), ple_norm[None], fin, tm, final)


def _trunk(x_a, x_b, p_a, p_b, layer_weights, norm_final, tm):
    depth = p_a.shape[0]
    nbp, L = x_a.shape[0], x_a.shape[1]
    B = nbp + x_b.shape[0]
    cos, sin = _rope_tables(L)
    fin = norm_final[None]
    x = (x_a, x_b, nbp, 0)
    p = (p_a, p_b, nbp, 0)
    for i in range(depth):
        out = _layer(x, B, p, i, [w[i] for w in layer_weights], cos, sin, fin, tm, i == depth - 1)
        x = (out[0], out[0], nbp, nbp)
    return out[0], out[1]


def kernel(x_prompt, x_sample, p_prompt, p_sample, norm_mix, w_in, ret_decay_fwd, ret_decay_bwd, ret_gn,
           hy_short_conv, hy_w1, hy_b1, hy_freq, hy_w2, hy_b2, hy_w3, hy_bias, pool_w, pool_scale, w_out,
           norm_ffn, ffn_w_up, ffn_conv, ffn_w_down, ple_w, ple_gate_w, ple_norm, norm_final):
    layer_weights = (norm_mix, w_in, ret_decay_fwd, ret_decay_bwd, ret_gn, hy_short_conv, hy_w1, hy_b1,
                     hy_freq, hy_w2, hy_b2, hy_w3, hy_bias, pool_w, pool_scale, w_out, norm_ffn, ffn_w_up,
                     ffn_conv, ffn_w_down, ple_w, ple_gate_w, ple_norm)
    return _trunk(x_prompt, x_sample, p_prompt, p_sample, layer_weights, norm_final, ROW_TILE)
```

```python
import functools
import math

import jax
import jax.numpy as jnp
from jax import lax
from jax.experimental import pallas as pl
from jax.experimental.pallas import tpu as pltpu

F32 = jnp.float32
BF16 = jnp.bfloat16

D_MODEL = 1024
RET_WIDTH = 512
RET_HEADS = 4
HEAD_DIM = 128
HY_WIDTH = 256
POOL_WIDTH = 256
POOL_GROUP_DIM = 64
POOL_WINDOWS = (2, 4, 8, 16)
POOL_PAD = 16
D_FF = 2816
PLE_DIM = 256
CHUNK = 128
ROPE_THETA = 10000.0
HY_EMB_BANDS = 16
HY_HIDDEN = 64
HY_MIN_DECAY = math.log(1e-2) / 1.5
HY_MAX_DECAY = math.log(1e-2) / 0.3
EPS = 1e-6

LANES = 128
SUBLANES = 8
BF16_ROWS = 16
MXU_DIM = 256
VMEM_LIMIT = 56 * 1024 * 1024

HY_BLOCK = MXU_DIM
HY_PAIR_UNROLL = 1
QVG_WIDTH = 3 * RET_WIDTH
RET_UNROLL = True
HY_IN_WIDTH = 3 * HY_WIDTH
ROW_TILE = 512
WIDE_TILE = 2
FFN_SUB = MXU_DIM
FFN_DOWN_GROUP = 4


def _cparams(sem):
    return pltpu.CompilerParams(dimension_semantics=sem, vmem_limit_bytes=VMEM_LIMIT)


def _rms(x, g):
    return x * lax.rsqrt(jnp.mean(x * x, axis=-1, keepdims=True) + EPS) * g


def _pair_specs(tm, width, nt, nbp, off, lead=None):
    pre = () if lead is None else (lead,)
    shape = (1, tm, width) if lead is None else (None, 1, tm, width)

    def first(b, i):
        return pre + (jnp.minimum(b, nbp - 1), jnp.where(b < nbp, i, nt - 1), 0)

    def second(b, i):
        return pre + (jnp.maximum(b - nbp, 0) + off, jnp.where(b < nbp, 0, i), 0)

    return pl.BlockSpec(shape, first), pl.BlockSpec(shape, second)


def _pick(nbp, a_ref, b_ref):
    return jnp.where(pl.program_id(0) < nbp, a_ref[0], b_ref[0])


def _pair_halo_specs(rows, width, tm, L, nbp, off):
    per, last = tm // rows, L // rows - 1

    def make(first, after):
        def index(b, i):
            r = jnp.minimum((i + 1) * per, last) if after else jnp.maximum(i * per - 1, 0)
            if first:
                return (jnp.minimum(b, nbp - 1), jnp.where(b < nbp, r, 0), 0)
            return (jnp.maximum(b - nbp, 0) + off, jnp.where(b < nbp, 0, r), 0)
        return pl.BlockSpec((1, rows, width), index)

    return make(True, False), make(False, False), make(True, True), make(False, True)


def _dwconv3_rows(u, tm, cw, first_tile, last_tile):
    main = u[:tm]
    prev = jnp.where(first_tile, 0.0, u[tm + BF16_ROWS - 1:tm + BF16_ROWS])
    nxt = jnp.where(last_tile, 0.0, u[tm + BF16_ROWS:tm + BF16_ROWS + 1])
    row = lax.broadcasted_iota(jnp.int32, main.shape, 0)
    up = jnp.where(row == 0, prev, pltpu.roll(main, 1, axis=0))
    dn = jnp.where(row == tm - 1, nxt, pltpu.roll(main, tm - 1, axis=0))
    return up * cw[0:1] + main * cw[1:2] + dn * cw[2:3]


def _in_proj_kernel(xa_ref, xb_ref, xpa_ref, xpb_ref, xna_ref, xnb_ref, g_ref, wa_ref, wk_ref, wh_ref, cw_ref,
                    cos_ref, sin_ref, cost_ref, sint_ref, qvg_ref, kt_ref, hy_ref, pool_ref, *, nbp):
    tm = xa_ref.shape[1]
    i = pl.program_id(1)
    g = g_ref[...]
    h = _rms(_pick(nbp, xa_ref, xb_ref), g).astype(BF16)
    hx = jnp.concatenate([h, _rms(_pick(nbp, xpa_ref, xpb_ref), g).astype(BF16),
                          _rms(_pick(nbp, xna_ref, xnb_ref), g).astype(BF16)], axis=0)
    u = jnp.dot(hx, wh_ref[...], preferred_element_type=F32)
    y = _dwconv3_rows(u, tm, cw_ref[...], i == 0, i == pl.num_programs(1) - 1)
    yt = y.T
    for j in range(tm // LANES):
        hy_ref[:, j, :, :] = yt[:, j * LANES:(j + 1) * LANES].reshape(HY_IN_WIDTH // SUBLANES, SUBLANES, LANES)
    kt = lax.dot_general(wk_ref[...], h, (((1,), (1,)), ((), ())), preferred_element_type=F32)
    cos_t, sin_t = cost_ref[...], sint_ref[...]
    for hd in range(RET_HEADS):
        kh = kt[hd * HEAD_DIM:(hd + 1) * HEAD_DIM]
        kt_ref[0, hd * HEAD_DIM:(hd + 1) * HEAD_DIM, :] = (
            kh * cos_t + pltpu.roll(kh, HEAD_DIM // 2, axis=0) * sin_t).astype(BF16)
    a = jnp.dot(h, wa_ref[...], preferred_element_type=F32)
    cos, sin = cos_ref[...], sin_ref[...]
    for hd in range(RET_HEADS):
        qh = a[:, hd * HEAD_DIM:(hd + 1) * HEAD_DIM]
        qvg_ref[0, :, hd * HEAD_DIM:(hd + 1) * HEAD_DIM] = (
            qh * cos + pltpu.roll(qh, HEAD_DIM // 2, axis=1) * sin).astype(BF16)
    qvg_ref[0, :, RET_WIDTH:] = a[:, RET_WIDTH:QVG_WIDTH].astype(BF16)
    pool_ref[0] = a[:, QVG_WIDTH:].astype(BF16)


def _in_proj(x, B, g, wa, wk_t, wh, cw, cos, sin, cos_t, sin_t, tm):
    xa, xb, nbp, off = x
    L = xa.shape[1]
    nt = L // tm
    const = lambda shape: pl.BlockSpec(shape, lambda b, i: (0,) * len(shape))
    return pl.pallas_call(
        functools.partial(_in_proj_kernel, nbp=nbp),
        grid=(B, nt),
        in_specs=[
            *_pair_specs(tm, D_MODEL, nt, nbp, off),
            *_pair_halo_specs(BF16_ROWS, D_MODEL, tm, L, nbp, off),
            const((1, D_MODEL)),
            const((D_MODEL, QVG_WIDTH + POOL_WIDTH)),
            const((RET_WIDTH, D_MODEL)),
            const((D_MODEL, HY_IN_WIDTH)),
            const((3, HY_IN_WIDTH)),
            pl.BlockSpec((tm, HEAD_DIM), lambda b, i: (i, 0)),
            pl.BlockSpec((tm, HEAD_DIM), lambda b, i: (i, 0)),
            pl.BlockSpec((HEAD_DIM, tm), lambda b, i: (0, i)),
            pl.BlockSpec((HEAD_DIM, tm), lambda b, i: (0, i)),
        ],
        out_specs=[
            pl.BlockSpec((1, tm, QVG_WIDTH), lambda b, i: (b, i, 0)),
            pl.BlockSpec((1, RET_WIDTH, tm), lambda b, i: (b, 0, i)),
            pl.BlockSpec((HY_IN_WIDTH // SUBLANES, tm // LANES, SUBLANES, LANES), lambda b, i: (0, i, b, 0)),
            pl.BlockSpec((1, tm, POOL_WIDTH), lambda b, i: (b, i, 0)),
        ],
        out_shape=[
            jax.ShapeDtypeStruct((B, L, QVG_WIDTH), BF16),
            jax.ShapeDtypeStruct((B, RET_WIDTH, L), BF16),
            jax.ShapeDtypeStruct((HY_IN_WIDTH // SUBLANES, L // LANES, B * SUBLANES, LANES), F32),
            jax.ShapeDtypeStruct((B, L, POOL_WIDTH), BF16),
        ],
        compiler_params=_cparams(("arbitrary", "arbitrary")),
        name="in_proj",
    )(xa, xb, xa, xb, xa, xb, g, wa, wk_t, wh, cw, cos, sin, cos_t, sin_t)


def _log_sigmoid(x):
    return jnp.minimum(x, 0.0) - jnp.log(1.0 + jnp.exp(-jnp.abs(x)))


def _retention_kernel(dec_ref, gn_ref, q_ref, kt_ref, v_ref, g_ref, o_ref, kv_s, st_s, tab_s):
    L = q_ref.shape[1]
    n_chunks = L // CHUNK
    C = CHUNK
    scale = HEAD_DIM ** -0.5

    lg = _log_sigmoid(dec_ref[0])
    lgf, lgb = lg[0:1], lg[1:2]
    ri = lax.broadcasted_iota(jnp.int32, (C, C), 0)
    ci = lax.broadcasted_iota(jnp.int32, (C, C), 1)
    rf = ri.astype(F32)
    cf = ci.astype(F32)
    dist = jnp.abs(ri - ci).astype(F32)
    tab_s[0] = jnp.exp(dist * jnp.where(ri >= ci, lgf, lgb)) * scale
    tab_s[1] = jnp.exp((rf + 1.0) * lgf) * scale
    tab_s[2] = jnp.exp((C - rf) * lgb) * scale
    tab_s[3] = jnp.exp((C - 1.0 - cf) * lgf)
    tab_s[4] = jnp.exp(cf * lgb)
    dec_f = jnp.exp(C * lgf)
    dec_b = jnp.exp(C * lgb)

    def summaries(n, carry):
        c0 = pl.multiple_of(n * C, C)
        kt = kt_ref[0, :, pl.ds(c0, C)].astype(F32)
        lhs = jnp.concatenate([kt * tab_s[3], kt * tab_s[4]], axis=0).astype(BF16)
        kv_s[n] = jnp.dot(lhs, v_ref[0, pl.ds(c0, C), :], preferred_element_type=F32)
        return carry

    lax.fori_loop(0, n_chunks, summaries, 0, unroll=RET_UNROLL)

    def scan(i, carry):
        sf, sb = carry
        m = n_chunks - 1 - i
        st_s[i, :C, :] = sf.astype(BF16)
        st_s[m, C:, :] = sb.astype(BF16)
        return sf * dec_f + kv_s[i, :C, :], sb * dec_b + kv_s[m, C:, :]

    zero = jnp.zeros((C, C), F32)
    lax.fori_loop(0, n_chunks, scan, (zero, zero), unroll=RET_UNROLL)

    gain = gn_ref[...]

    def outputs(n, carry):
        c0 = pl.multiple_of(n * C, C)
        qb = q_ref[0, pl.ds(c0, C), :]
        qn = qb.astype(F32)
        vn = v_ref[0, pl.ds(c0, C), :]
        sc = jnp.dot(qb, kt_ref[0, :, pl.ds(c0, C)], preferred_element_type=F32)
        lhs = jnp.concatenate([sc * tab_s[0], qn * tab_s[1], qn * tab_s[2]], axis=1).astype(BF16)
        rhs = jnp.concatenate([vn, st_s[n]], axis=0)
        o = jnp.dot(lhs, rhs, preferred_element_type=F32)
        mu = jnp.mean(o, axis=-1, keepdims=True)
        oc = o - mu
        var = jnp.mean(oc * oc, axis=-1, keepdims=True)
        gate = g_ref[0, pl.ds(c0, C), :].astype(F32)
        y = oc * lax.rsqrt(var + EPS) * gain * (gate * jax.nn.sigmoid(gate))
        o_ref[0, pl.ds(c0, C), :] = y.astype(BF16)
        return carry

    lax.fori_loop(0, n_chunks, outputs, 0, unroll=RET_UNROLL)


def _retention(qvg, kt, dec, gn):
    B, L, _ = qvg.shape
    H = RET_HEADS

    def col(off):
        return pl.BlockSpec((1, L, HEAD_DIM), lambda b, h: (b, 0, off + h))

    return pl.pallas_call(
        _retention_kernel,
        grid=(B, H),
        in_specs=[
            pl.BlockSpec((1, 2, LANES), lambda b, h: (h, 0, 0)),
            pl.BlockSpec((1, HEAD_DIM), lambda b, h: (0, h)),
            col(0),
            pl.BlockSpec((1, HEAD_DIM, L), lambda b, h: (b, h, 0)),
            col(H), col(2 * H),
        ],
        out_specs=pl.BlockSpec((1, L, HEAD_DIM), lambda b, h: (b, 0, h)),
        out_shape=jax.ShapeDtypeStruct((B, L, RET_WIDTH), BF16),
        scratch_shapes=[
            pltpu.VMEM((L // CHUNK, 2 * CHUNK, CHUNK), F32),
            pltpu.VMEM((L // CHUNK, 2 * CHUNK, CHUNK), BF16),
            pltpu.VMEM((5, CHUNK, CHUNK), F32),
        ],
        compiler_params=_cparams(("parallel", "parallel")),
        name="retention",
    )(dec, gn, qvg, kt, qvg, qvg)


def _split_bf16(x):
    hi = x.astype(BF16)
    return hi, (x - hi.astype(F32)).astype(BF16)


def _filter_kernel(w1_ref, b1_ref, fr_ref, w2_ref, b2_ref, w3_ref, kk_ref, hid_s):
    L = kk_ref.shape[1] // 2
    hi = lax.Precision.HIGHEST

    @pl.when(pl.program_id(0) == 0)
    def _():
        r = lax.broadcasted_iota(jnp.int32, (HY_HIDDEN, L), 0)
        lane = lax.broadcasted_iota(jnp.int32, (HY_HIDDEN, L), 1)
        band_idx = jnp.where(r <= HY_EMB_BANDS, r - 1, r - 1 - HY_EMB_BANDS).astype(F32)
        band = 1e-4 + band_idx * ((HY_EMB_BANDS - 1 - 1e-4) / (HY_EMB_BANDS - 1))
        fr = fr_ref[...]

        def hidden(s_int):
            s = s_int.astype(F32)
            t = s / (L - 1.0)
            ang = (2.0 * math.pi / L) * s * band
            feats = jnp.where(r == 0, t,
                              jnp.where(r <= HY_EMB_BANDS, jnp.cos(ang),
                                        jnp.where(r <= 2 * HY_EMB_BANDS, -jnp.sin(ang), 0.0)))
            h = jnp.sin(fr * (jnp.dot(w1_ref[...], feats, precision=hi, preferred_element_type=F32) + b1_ref[...]))
            return jnp.sin(fr * (jnp.dot(w2_ref[...], h, precision=hi, preferred_element_type=F32) + b2_ref[...]))

        hid_s[0], hid_s[1] = _split_bf16(hidden(lane))
        hid_s[2], hid_s[3] = _split_bf16(hidden(L - lane))

    crow = lax.broadcasted_iota(jnp.int32, (HY_WIDTH, L), 0).astype(F32)
    delta = jnp.abs(HY_MIN_DECAY + crow * ((HY_MAX_DECAY - HY_MIN_DECAY) / (HY_WIDTH - 1)))
    clane = lax.broadcasted_iota(jnp.int32, (HY_WIDTH, L), 1)

    def taps(h_hi, h_lo, s_row_int, w3):
        w_hi, w_lo = _split_bf16(w3)
        h = (jnp.dot(w_hi, h_hi, preferred_element_type=F32) + jnp.dot(w_hi, h_lo, preferred_element_type=F32)
             + jnp.dot(w_lo, h_hi, preferred_element_type=F32))
        return h * jnp.exp(-(s_row_int.astype(F32) / (L - 1.0)) * delta)

    h_f = taps(hid_s[0], hid_s[1], clane, w3_ref[0, 0])
    h_b = taps(hid_s[2], hid_s[3], L - clane, w3_ref[0, 1])
    h_b = jnp.where(clane == 0, 0.0, h_b)
    norm = jnp.sum(jnp.abs(h_f), axis=-1, keepdims=True) + jnp.sum(jnp.abs(h_b), axis=-1, keepdims=True)
    inv = 1.0 / norm
    kk = jnp.concatenate([h_b * inv, h_f * inv], axis=1)
    bits = pltpu.bitcast(kk, jnp.uint32)
    bf = (bits + jnp.uint32(0x7FFF) + ((bits >> 16) & jnp.uint32(1))) >> 16
    lane2 = lax.broadcasted_iota(jnp.int32, bf.shape, 1)
    prev = jnp.where(lane2 == 0, jnp.uint32(0), pltpu.roll(bf, 1, axis=1))
    kk_ref[...] = bf | (prev << 16)


def _hyena_filters(w1t, b1, fr, w2t, b2, w3t, L):
    small = lambda shape: pl.BlockSpec(shape, lambda o: (0,) * len(shape))
    return pl.pallas_call(
        _filter_kernel,
        grid=(2,),
        in_specs=[
            small((HY_HIDDEN, HY_HIDDEN)), small((HY_HIDDEN, 1)), small((HY_HIDDEN, 1)),
            small((HY_HIDDEN, HY_HIDDEN)), small((HY_HIDDEN, 1)),
            pl.BlockSpec((1, 2, HY_WIDTH, HY_HIDDEN), lambda o: (o, 0, 0, 0)),
        ],
        out_specs=pl.BlockSpec((HY_WIDTH, 2 * L), lambda o: (o, 0)),
        out_shape=jax.ShapeDtypeStruct((2 * HY_WIDTH, 2 * L), jnp.uint32),
        scratch_shapes=[pltpu.VMEM((4, HY_HIDDEN, L), BF16)],
        compiler_params=_cparams(("arbitrary",)),
        name="hyena_filters",
    )(w1t, b1, fr, w2t, b2, w3t)


def _hyena_kernel(kk1_ref, kk2_ref, xv_ref, x1_ref, x2_ref, bias_ref, out_ref,
                  toep1_s, toep2_s, u2_s, x1u2_s, y2_s, *, B, BP):
    n_tiles = xv_ref.shape[0]
    L = n_tiles * LANES
    T = HY_BLOCK
    NB = L // T
    TPB = T // LANES

    @pl.when(pl.program_id(0) == 0)
    def _():
        toep1_s[...] = jnp.zeros_like(toep1_s)
        toep2_s[...] = jnp.zeros_like(toep2_s)
        u2_s[...] = jnp.zeros_like(u2_s)
        x1u2_s[...] = jnp.zeros_like(x1u2_s)

    def build_toeplitz(kk_ref, c, toep_s):
        row = jnp.broadcast_to(kk_ref[pl.ds(c, 1), :], (SUBLANES, 2 * L))
        for g in range(LANES // BF16_ROWS):
            rolled = pltpu.roll(row, BF16_ROWS * g, axis=1, stride=2, stride_axis=0)
            tile = pltpu.bitcast(rolled, BF16)
            toep_s[BF16_ROWS * g:BF16_ROWS * (g + 1), :] = tile

    def long_conv(toep_s, u_s, y_s):
        for d in [0] + [e for e in range(-(NB - 1), NB) if e != 0]:
            n = NB - abs(d)
            src = max(0, -d) * BP
            dst = max(0, d) * BP
            w0 = L + d * T
            m = jnp.concatenate([toep_s[:, w0:w0 + T], toep_s[:, w0 - LANES:w0 - LANES + T]], axis=0)
            lhs = u_s[src:src + n * BP, :].astype(BF16)
            part = jnp.dot(lhs, m, preferred_element_type=F32)
            if d == 0:
                y_s[...] = part
            else:
                y_s[dst:dst + n * BP, :] += part

    def channel(c, k):
        u_s, x1u_s, y_s = u2_s.at[k], x1u2_s.at[k], y2_s.at[k]
        rows = pl.ds(c, B, stride=SUBLANES)
        for j in range(n_tiles):
            J, l0 = j // TPB, (j % TPB) * LANES
            u_s[J * BP:J * BP + B, l0:l0 + LANES] = xv_ref.at[j][rows, :]
            x1u_s[J * BP:J * BP + B, l0:l0 + LANES] = x1_ref.at[j][rows, :]
        b0 = bias_ref[0, 0, pl.ds(c, 1), :]
        b1 = bias_ref[0, 1, pl.ds(c, 1), :]
        build_toeplitz(kk2_ref, c, toep2_s)
        long_conv(toep1_s, u_s, y_s)
        u_s[...] = x1u_s[...] * (y_s[...] + u_s[...] * b0)
        build_toeplitz(kk1_ref, jnp.minimum(c + 1, SUBLANES - 1), toep1_s)
        long_conv(toep2_s, u_s, y_s)
        for j in range(n_tiles):
            J, l0 = j // TPB, (j % TPB) * LANES
            yy = y_s[J * BP:J * BP + B, l0:l0 + LANES] + u_s[J * BP:J * BP + B, l0:l0 + LANES] * b1[:, :LANES]
            out_ref.at[j][rows, :] = x2_ref.at[j][rows, :] * yy

    def channel_pair(i, carry):
        channel(2 * i, 0)
        channel(2 * i + 1, 1)
        return carry

    build_toeplitz(kk1_ref, 0, toep1_s)
    lax.fori_loop(0, SUBLANES // 2, channel_pair, 0, unroll=HY_PAIR_UNROLL)


def _hyena(kk, hy_t, bias, B):
    n_in, n_tiles, R, _ = hy_t.shape
    L = n_tiles * LANES
    NB = L // HY_BLOCK
    BP = -(-B // SUBLANES) * SUBLANES
    n_c = HY_WIDTH // SUBLANES

    def sect(s):
        return pl.BlockSpec((None, n_tiles, R, LANES), lambda c: (s * n_c + c, 0, 0, 0))

    return pl.pallas_call(
        functools.partial(_hyena_kernel, B=B, BP=BP),
        grid=(n_c,),
        in_specs=[
            pl.BlockSpec((SUBLANES, 2 * L), lambda c: (c, 0)),
            pl.BlockSpec((SUBLANES, 2 * L), lambda c: (n_c + c, 0)),
            sect(0), sect(1), sect(2),
            pl.BlockSpec((1, 2, SUBLANES, HY_BLOCK), lambda c: (c, 0, 0, 0)),
        ],
        out_specs=pl.BlockSpec((None, n_tiles, R, LANES), lambda c: (c, 0, 0, 0)),
        out_shape=jax.ShapeDtypeStruct((n_c, n_tiles, R, LANES), F32),
        scratch_shapes=[
            pltpu.VMEM((LANES, 2 * L), BF16), pltpu.VMEM((LANES, 2 * L), BF16),
            pltpu.VMEM((2, NB * BP, HY_BLOCK), F32),
            pltpu.VMEM((2, NB * BP, HY_BLOCK), F32),
            pltpu.VMEM((2, NB * BP, HY_BLOCK), F32),
        ],
        compiler_params=_cparams(("arbitrary",)),
        name="hyena",
    )(kk, kk, hy_t, hy_t, hy_t, bias)


def _pool_kernel(u_ref, inv_ref, w_ref, sc_ref, o_ref):
    L = u_ref.shape[1]
    u = u_ref[0].astype(F32)
    pad = jnp.zeros((POOL_PAD, POOL_WIDTH), F32)
    up = jnp.concatenate([pad, u, pad], axis=0)
    n = L + 2 * POOL_PAD
    left = lax.broadcasted_iota(jnp.int32, (L, LANES), 1) < POOL_GROUP_DIM

    def both(x, k):
        return pltpu.roll(x, k, axis=0) + pltpu.roll(x, n - k, axis=0)

    def windows(x, levels):
        q = pltpu.roll(x, 1, axis=0) + x
        out = [q]
        for k in (1, 2, 4)[:levels - 1]:
            q = both(q, k)
            out.append(q)
        return [w[POOL_PAD:POOL_PAD + L] for w in out]

    p2, p4 = windows(up[:, :LANES], 2)
    _, _, p8, p16 = windows(up[:, LANES:], 4)
    total = jnp.concatenate([jnp.where(left, p2, p4), jnp.where(left, p8, p16)], axis=1)
    d = (total * inv_ref[...] - u).astype(BF16)
    o_ref[0] = (jnp.dot(d, w_ref[...], preferred_element_type=F32) * sc_ref[...]).astype(BF16)


def _pool_inv_counts(L):
    t = jnp.arange(L)[:, None]
    half = jnp.repeat(jnp.asarray([w // 2 for w in POOL_WINDOWS]), POOL_GROUP_DIM)[None, :]
    cnt = jnp.minimum(t + half - 1, L - 1) - jnp.maximum(t - half, 0) + 1
    return 1.0 / cnt.astype(F32)


def _pool(pool_in, w_bd, scale):
    B, L, _ = pool_in.shape
    return pl.pallas_call(
        _pool_kernel,
        grid=(B,),
        in_specs=[
            pl.BlockSpec((1, L, POOL_WIDTH), lambda b: (b, 0, 0)),
            pl.BlockSpec((L, POOL_WIDTH), lambda b: (0, 0)),
            pl.BlockSpec((POOL_WIDTH, POOL_WIDTH), lambda b: (0, 0)),
            pl.BlockSpec((1, POOL_WIDTH), lambda b: (0, 0)),
        ],
        out_specs=pl.BlockSpec((1, L, POOL_WIDTH), lambda b: (b, 0, 0)),
        out_shape=jax.ShapeDtypeStruct((B, L, POOL_WIDTH), BF16),
        compiler_params=_cparams(("parallel",)),
        name="pool_mixer",
    )(pool_in, _pool_inv_counts(L), w_bd, scale)


def _mix_out_kernel(xa_ref, xb_ref, ret_ref, hy_ref, pool_ref, wr_ref, wh_ref, wp_ref, g_ref, x1_ref, h2_ref, *, nbp):
    tm = xa_ref.shape[1]
    hy_t = jnp.concatenate([hy_ref[:, j].reshape(HY_WIDTH, LANES) for j in range(tm // LANES)],
                           axis=1).astype(BF16)
    mix = jnp.dot(ret_ref[0], wr_ref[...], preferred_element_type=F32)
    mix += lax.dot_general(hy_t, wh_ref[...], (((0,), (0,)), ((), ())), preferred_element_type=F32)
    mix += jnp.dot(pool_ref[0], wp_ref[...], preferred_element_type=F32)
    x1 = _pick(nbp, xa_ref, xb_ref) + mix
    x1_ref[0] = x1
    h2_ref[0] = _rms(x1, g_ref[...]).astype(BF16)


def _mix_out(x, ret, hy_o, pool_o, wr, wh, wp, g, tm):
    xa, xb, nbp, off = x
    B, L, _ = ret.shape
    nt = L // tm
    return pl.pallas_call(
        functools.partial(_mix_out_kernel, nbp=nbp),
        grid=(B, nt),
        in_specs=[
            *_pair_specs(tm, D_MODEL, nt, nbp, off),
            pl.BlockSpec((1, tm, RET_WIDTH), lambda b, i: (b, i, 0)),
            pl.BlockSpec((HY_WIDTH // SUBLANES, tm // LANES, SUBLANES, LANES), lambda b, i: (0, i, b, 0)),
            pl.BlockSpec((1, tm, POOL_WIDTH), lambda b, i: (b, i, 0)),
            pl.BlockSpec((RET_WIDTH, D_MODEL), lambda b, i: (0, 0)),
            pl.BlockSpec((HY_WIDTH, D_MODEL), lambda b, i: (0, 0)),
            pl.BlockSpec((POOL_WIDTH, D_MODEL), lambda b, i: (0, 0)),
            pl.BlockSpec((1, D_MODEL), lambda b, i: (0, 0)),
        ],
        out_specs=[
            pl.BlockSpec((1, tm, D_MODEL), lambda b, i: (b, i, 0)),
            pl.BlockSpec((1, tm, D_MODEL), lambda b, i: (b, i, 0)),
        ],
        out_shape=[
            jax.ShapeDtypeStruct((B, L, D_MODEL), F32),
            jax.ShapeDtypeStruct((B, L, D_MODEL), BF16),
        ],
        compiler_params=_cparams(("arbitrary", "arbitrary")),
        name="mix_out",
    )(xa, xb, ret, hy_o, pool_o, wr, wh, wp, g)


def _ffn_kernel(h_ref, hp_ref, hn_ref, x1_ref, wup_ref, cw_ref, wd_ref, pa_ref, pb_ref, pw_ref, pg_ref,
                pn_ref, fin_ref, *o_refs, nbp, final):
    tm = h_ref.shape[1]
    i = pl.program_id(1)
    nt = pl.num_programs(1)

    hx = jnp.concatenate([h_ref[0], hp_ref[0], hn_ref[0]], axis=0)
    row = lax.broadcasted_iota(jnp.int32, (tm, FFN_SUB), 0)
    n_sub = D_FF // FFN_SUB

    def up_proj(s):
        c0 = s * FFN_SUB
        gate = jnp.dot(hx, wup_ref[:, c0:c0 + FFN_SUB], preferred_element_type=F32)
        up = jnp.dot(hx[:tm], wup_ref[:, D_FF + c0:D_FF + c0 + FFN_SUB], preferred_element_type=F32)
        return gate, up

    acc = x1_ref[0]
    nxt = up_proj(0)
    group, g0 = [], 0
    for s in range(n_sub):
        c0 = s * FFN_SUB
        gx, up = nxt
        if s + 1 < n_sub:
            nxt = up_proj(s + 1)
        gate = gx[:tm]
        g_prev = jnp.where(i == 0, 0.0, gx[tm + BF16_ROWS - 1:tm + BF16_ROWS])
        g_next = jnp.where(i == nt - 1, 0.0, gx[tm + BF16_ROWS:tm + BF16_ROWS + 1])
        g_up = jnp.where(row == 0, g_prev, pltpu.roll(gate, 1, axis=0))
        g_dn = jnp.where(row == tm - 1, g_next, pltpu.roll(gate, tm - 1, axis=0))
        cw = cw_ref[:, c0:c0 + FFN_SUB]
        gc = g_up * cw[0:1] + gate * cw[1:2] + g_dn * cw[2:3]
        group.append((jax.nn.gelu(gc) * up).astype(BF16))
        if len(group) == FFN_DOWN_GROUP or s == n_sub - 1:
            a = jnp.concatenate(group, axis=1)
            acc = acc + jnp.dot(a, wd_ref[g0:c0 + FFN_SUB, :], preferred_element_type=F32)
            group, g0 = [], c0 + FFN_SUB

    x2 = acc
    p = _pick(nbp, pa_ref, pb_ref).astype(BF16)
    e = _rms(jnp.dot(p, pw_ref[...], preferred_element_type=F32), pn_ref[...])
    gt = jax.nn.sigmoid(jnp.dot(x2.astype(BF16), pg_ref[...], preferred_element_type=F32))
    x3 = x2 + gt * e
    if not final:
        o_refs[0][0] = x3
        return
    y = _rms(x3, fin_ref[...])
    b = pl.program_id(0)

    @pl.when(b < nbp)
    def _():
        o_refs[0][0] = y

    @pl.when(b >= nbp)
    def _():
        o_refs[1][0] = y


def _ffn(h2, x1, w_up, cw, w_down, p, layer, ple_w, ple_gate, ple_norm, fin, tm, final):
    pa, pb, nbp, off = p
    B, L, _ = x1.shape
    nt = L // tm
    halo = BF16_ROWS
    nh = L // halo
    per = tm // halo
    full = lambda shape: pl.BlockSpec(shape, lambda b, i: (0,) * len(shape), pipeline_mode=pl.Buffered(1))
    tile = pl.BlockSpec((1, tm, D_MODEL), lambda b, i: (b, i, 0))
    if final:
        out_specs = list(_pair_specs(tm, D_MODEL, nt, nbp, 0))
        out_shape = [jax.ShapeDtypeStruct((nbp, L, D_MODEL), F32), jax.ShapeDtypeStruct((B - nbp, L, D_MODEL), F32)]
    else:
        out_specs = [tile]
        out_shape = [jax.ShapeDtypeStruct((B, L, D_MODEL), F32)]
    return pl.pallas_call(
        functools.partial(_ffn_kernel, nbp=nbp, final=final),
        grid=(B, nt),
        in_specs=[
            tile,
            pl.BlockSpec((1, halo, D_MODEL), lambda b, i: (b, jnp.maximum(i * per - 1, 0), 0)),
            pl.BlockSpec((1, halo, D_MODEL), lambda b, i: (b, jnp.minimum((i + 1) * per, nh - 1), 0)),
            tile,
            full((D_MODEL, 2 * D_FF)), full((3, D_FF)), full((D_FF, D_MODEL)),
            *_pair_specs(tm, PLE_DIM, nt, nbp, off, lead=layer),
            full((PLE_DIM, D_MODEL)), full((D_MODEL, D_MODEL)), full((1, D_MODEL)), full((1, D_MODEL)),
        ],
        out_specs=out_specs,
        out_shape=out_shape,
        compiler_params=_cparams(("arbitrary", "arbitrary")),
        name="ffn_ple",
    )(h2, h2, h2, x1, w_up, cw, w_down, pa, pb, ple_w, ple_gate, ple_norm, fin)


def _rope_tables(L):
    half = HEAD_DIM // 2
    inv = ROPE_THETA ** (-jnp.arange(half, dtype=F32) / half)
    ang = jnp.arange(L, dtype=F32)[:, None] * inv[None, :]
    cos, sin = jnp.cos(ang), jnp.sin(ang)
    return jnp.concatenate([cos, cos], axis=-1), jnp.concatenate([-sin, sin], axis=-1)


def _block_diag(pool_w):
    out = jnp.zeros((POOL_WIDTH, POOL_WIDTH), pool_w.dtype)
    for g in range(len(POOL_WINDOWS)):
        sl = slice(g * POOL_GROUP_DIM, (g + 1) * POOL_GROUP_DIM)
        out = out.at[sl, sl].set(pool_w[g])
    return out


def _layer(x, B, p, layer, lw, cos, sin, fin, tm, final):
    (norm_mix, w_in, dec_f, dec_b, ret_gn, hy_conv, hy_w1, hy_b1, hy_freq, hy_w2, hy_b2, hy_w3, hy_bias,
     pool_w, pool_scale, w_out, norm_ffn, w_up, ffn_conv, w_down, ple_w, ple_gate, ple_norm) = lw
    L = x[0].shape[1]
    R = RET_WIDTH

    w_q, w_k, w_v, w_g, w_hy, w_pool = jnp.split(w_in, [R, 2 * R, 3 * R, 4 * R, 4 * R + HY_IN_WIDTH], axis=1)
    wa = jnp.concatenate([w_q, w_v, w_g, w_pool], axis=1).astype(BF16)
    qvg, kt, hy_t, pool_in = _in_proj(x, B, norm_mix[None], wa, w_k.T.astype(BF16), w_hy.astype(BF16), hy_conv,
                                      cos, sin, cos.T, sin.T, WIDE_TILE * tm)

    dec = jnp.broadcast_to(jnp.stack([dec_f, dec_b], axis=1)[:, :, None], (RET_HEADS, 2, LANES)).astype(F32)
    ret = _retention(qvg, kt, dec, ret_gn[None])

    w1t = jnp.zeros((HY_HIDDEN, HY_HIDDEN), F32).at[:, :hy_w1.shape[0]].set(hy_w1.T)
    w3t = hy_w3.T.reshape(2, 2, HY_WIDTH, HY_HIDDEN)
    kk = _hyena_filters(w1t, hy_b1[:, None], hy_freq[:, None], hy_w2.T, hy_b2[:, None], w3t, L)
    bias = jnp.broadcast_to(hy_bias.reshape(2, HY_WIDTH // SUBLANES, SUBLANES).transpose(1, 0, 2)[..., None],
                            (HY_WIDTH // SUBLANES, 2, SUBLANES, HY_BLOCK))
    hy_o = _hyena(kk, hy_t, bias, B)

    pool_o = _pool(pool_in, _block_diag(pool_w).astype(BF16), pool_scale[None])

    wo = w_out.astype(BF16)
    x1, h2 = _mix_out(x, ret, hy_o, pool_o, wo[:R], wo[R:R + HY_WIDTH], wo[R + HY_WIDTH:], norm_ffn[None],
                      WIDE_TILE * tm)

    return _ffn(h2, x1, w_up.astype(BF16), ffn_conv, w_down.astype(BF16), p, layer, ple_w.astype(BF16),
                ple_gate.astype(BF16), ple_norm[None], fin, tm, final)


def _trunk(x_a, x_b, p_a, p_b, layer_weights, norm_final, tm):
    depth = p_a.shape[0]
    nbp, L = x_a.shape[0], x_a.shape[1]
    B = nbp + x_b.shape[0]
    cos, sin = _rope_tables(L)
    fin = norm_final[None]
    x = (x_a, x_b, nbp, 0)
    p = (p_a, p_b, nbp, 0)
    for i in range(depth):
        out = _layer(x, B, p, i, [w[i] for w in layer_weights], cos, sin, fin, tm, i == depth - 1)
        x = (out[0], out[0], nbp, nbp)
    return out[0], out[1]


def kernel(x_prompt, x_sample, p_prompt, p_sample, norm_mix, w_in, ret_decay_fwd, ret_decay_bwd, ret_gn,
           hy_short_conv, hy_w1, hy_b1, hy_freq, hy_w2, hy_b2, hy_w3, hy_bias, pool_w, pool_scale, w_out,
           norm_ffn, ffn_w_up, ffn_conv, ffn_w_down, ple_w, ple_gate_w, ple_norm, norm_final):
    layer_weights = (norm_mix, w_in, ret_decay_fwd, ret_decay_bwd, ret_gn, hy_short_conv, hy_w1, hy_b1,
                     hy_freq, hy_w2, hy_b2, hy_w3, hy_bias, pool_w, pool_scale, w_out, norm_ffn, ffn_w_up,
                     ffn_conv, ffn_w_down, ple_w, ple_gate_w, ple_norm)
    return _trunk(x_prompt, x_sample, p_prompt, p_sample, layer_weights, norm_final, ROW_TILE)
```

```python
import functools
import math

import jax
import jax.numpy as jnp
from jax import lax
from jax.experimental import pallas as pl
from jax.experimental.pallas import tpu as pltpu

F32 = jnp.float32
BF16 = jnp.bfloat16

D_MODEL = 1024
RET_WIDTH = 512
RET_HEADS = 4
HEAD_DIM = 128
HY_WIDTH = 256
POOL_WIDTH = 256
POOL_GROUP_DIM = 64
POOL_WINDOWS = (2, 4, 8, 16)
POOL_PAD = 16
D_FF = 2816
PLE_DIM = 256
CHUNK = 128
ROPE_THETA = 10000.0
HY_EMB_BANDS = 16
HY_HIDDEN = 64
HY_MIN_DECAY = math.log(1e-2) / 1.5
HY_MAX_DECAY = math.log(1e-2) / 0.3
EPS = 1e-6

LANES = 128
SUBLANES = 8
BF16_ROWS = 16
MXU_DIM = 256
VMEM_LIMIT = 56 * 1024 * 1024

HY_BLOCK = MXU_DIM
HY_PAIR_UNROLL = 1
QVG_WIDTH = 3 * RET_WIDTH
RET_UNROLL = True
HY_IN_WIDTH = 3 * HY_WIDTH
ROW_TILE = 512
WIDE_TILE = 2
FFN_SUB = MXU_DIM
FFN_DOWN_GROUP = 4


def _cparams(sem):
    return pltpu.CompilerParams(dimension_semantics=sem, vmem_limit_bytes=VMEM_LIMIT)


def _rms(x, g):
    return x * lax.rsqrt(jnp.mean(x * x, axis=-1, keepdims=True) + EPS) * g


def _pair_specs(tm, width, nt, nbp, off, lead=None):
    pre = () if lead is None else (lead,)
    shape = (1, tm, width) if lead is None else (None, 1, tm, width)

    def first(b, i):
        return pre + (jnp.minimum(b, nbp - 1), jnp.where(b < nbp, i, nt - 1), 0)

    def second(b, i):
        return pre + (jnp.maximum(b - nbp, 0) + off, jnp.where(b < nbp, 0, i), 0)

    return pl.BlockSpec(shape, first), pl.BlockSpec(shape, second)


def _pick(nbp, a_ref, b_ref):
    return jnp.where(pl.program_id(0) < nbp, a_ref[0], b_ref[0])


def _pair_halo_specs(rows, width, tm, L, nbp, off):
    per, last = tm // rows, L // rows - 1

    def make(first, after):
        def index(b, i):
            r = jnp.minimum((i + 1) * per, last) if after else jnp.maximum(i * per - 1, 0)
            if first:
                return (jnp.minimum(b, nbp - 1), jnp.where(b < nbp, r, 0), 0)
            return (jnp.maximum(b - nbp, 0) + off, jnp.where(b < nbp, 0, r), 0)
        return pl.BlockSpec((1, rows, width), index)

    return make(True, False), make(False, False), make(True, True), make(False, True)


def _dwconv3_rows(u, tm, cw, first_tile, last_tile):
    main = u[:tm]
    prev = jnp.where(first_tile, 0.0, u[tm + BF16_ROWS - 1:tm + BF16_ROWS])
    nxt = jnp.where(last_tile, 0.0, u[tm + BF16_ROWS:tm + BF16_ROWS + 1])
    row = lax.broadcasted_iota(jnp.int32, main.shape, 0)
    up = jnp.where(row == 0, prev, pltpu.roll(main, 1, axis=0))
    dn = jnp.where(row == tm - 1, nxt, pltpu.roll(main, tm - 1, axis=0))
    return up * cw[0:1] + main * cw[1:2] + dn * cw[2:3]


def _in_proj_kernel(xa_ref, xb_ref, xpa_ref, xpb_ref, xna_ref, xnb_ref, g_ref, wa_ref, wk_ref, wh_ref, cw_ref,
                    cos_ref, sin_ref, cost_ref, sint_ref, qvg_ref, kt_ref, hy_ref, pool_ref, *, nbp):
    tm = xa_ref.shape[1]
    i = pl.program_id(1)
    g = g_ref[...]
    h = _rms(_pick(nbp, xa_ref, xb_ref), g).astype(BF16)
    hx = jnp.concatenate([h, _rms(_pick(nbp, xpa_ref, xpb_ref), g).astype(BF16),
                          _rms(_pick(nbp, xna_ref, xnb_ref), g).astype(BF16)], axis=0)
    u = jnp.dot(hx, wh_ref[...], preferred_element_type=F32)
    y = _dwconv3_rows(u, tm, cw_ref[...], i == 0, i == pl.num_programs(1) - 1)
    yt = y.T
    for j in range(tm // LANES):
        hy_ref[:, j, :, :] = yt[:, j * LANES:(j + 1) * LANES].reshape(HY_IN_WIDTH // SUBLANES, SUBLANES, LANES)
    kt = lax.dot_general(wk_ref[...], h, (((1,), (1,)), ((), ())), preferred_element_type=F32)
    cos_t, sin_t = cost_ref[...], sint_ref[...]
    for hd in range(RET_HEADS):
        kh = kt[hd * HEAD_DIM:(hd + 1) * HEAD_DIM]
        kt_ref[0, hd * HEAD_DIM:(hd + 1) * HEAD_DIM, :] = (
            kh * cos_t + pltpu.roll(kh, HEAD_DIM // 2, axis=0) * sin_t).astype(BF16)
    a = jnp.dot(h, wa_ref[...], preferred_element_type=F32)
    cos, sin = cos_ref[...], sin_ref[...]
    for hd in range(RET_HEADS):
        qh = a[:, hd * HEAD_DIM:(hd + 1) * HEAD_DIM]
        qvg_ref[0, :, hd * HEAD_DIM:(hd + 1) * HEAD_DIM] = (
            qh * cos + pltpu.roll(qh, HEAD_DIM // 2, axis=1) * sin).astype(BF16)
    qvg_ref[0, :, RET_WIDTH:] = a[:, RET_WIDTH:QVG_WIDTH].astype(BF16)
    pool_ref[0] = a[:, QVG_WIDTH:].astype(BF16)


def _in_proj(x, B, g, wa, wk_t, wh, cw, cos, sin, cos_t, sin_t, tm):
    xa, xb, nbp, off = x
    L = xa.shape[1]
    nt = L // tm
    const = lambda shape: pl.BlockSpec(shape, lambda b, i: (0,) * len(shape))
    return pl.pallas_call(
        functools.partial(_in_proj_kernel, nbp=nbp),
        grid=(B, nt),
        in_specs=[
            *_pair_specs(tm, D_MODEL, nt, nbp, off),
            *_pair_halo_specs(BF16_ROWS, D_MODEL, tm, L, nbp, off),
            const((1, D_MODEL)),
            const((D_MODEL, QVG_WIDTH + POOL_WIDTH)),
            const((RET_WIDTH, D_MODEL)),
            const((D_MODEL, HY_IN_WIDTH)),
            const((3, HY_IN_WIDTH)),
            pl.BlockSpec((tm, HEAD_DIM), lambda b, i: (i, 0)),
            pl.BlockSpec((tm, HEAD_DIM), lambda b, i: (i, 0)),
            pl.BlockSpec((HEAD_DIM, tm), lambda b, i: (0, i)),
            pl.BlockSpec((HEAD_DIM, tm), lambda b, i: (0, i)),
        ],
        out_specs=[
            pl.BlockSpec((1, tm, QVG_WIDTH), lambda b, i: (b, i, 0)),
            pl.BlockSpec((1, RET_WIDTH, tm), lambda b, i: (b, 0, i)),
            pl.BlockSpec((HY_IN_WIDTH // SUBLANES, tm // LANES, SUBLANES, LANES), lambda b, i: (0, i, b, 0)),
            pl.BlockSpec((1, tm, POOL_WIDTH), lambda b, i: (b, i, 0)),
        ],
        out_shape=[
            jax.ShapeDtypeStruct((B, L, QVG_WIDTH), BF16),
            jax.ShapeDtypeStruct((B, RET_WIDTH, L), BF16),
            jax.ShapeDtypeStruct((HY_IN_WIDTH // SUBLANES, L // LANES, B * SUBLANES, LANES), F32),
            jax.ShapeDtypeStruct((B, L, POOL_WIDTH), BF16),
        ],
        compiler_params=_cparams(("arbitrary", "arbitrary")),
        name="in_proj",
    )(xa, xb, xa, xb, xa, xb, g, wa, wk_t, wh, cw, cos, sin, cos_t, sin_t)


def _log_sigmoid(x):
    return jnp.minimum(x, 0.0) - jnp.log(1.0 + jnp.exp(-jnp.abs(x)))


def _retention_kernel(dec_ref, gn_ref, q_ref, kt_ref, v_ref, g_ref, o_ref, kv_s, st_s, tab_s):
    L = q_ref.shape[1]
    n_chunks = L // CHUNK
    C = CHUNK
    scale = HEAD_DIM ** -0.5

    lg = _log_sigmoid(dec_ref[0])
    lgf, lgb = lg[0:1], lg[1:2]
    ri = lax.broadcasted_iota(jnp.int32, (C, C), 0)
    ci = lax.broadcasted_iota(jnp.int32, (C, C), 1)
    rf = ri.astype(F32)
    cf = ci.astype(F32)
    dist = jnp.abs(ri - ci).astype(F32)
    tab_s[0] = jnp.exp(dist * jnp.where(ri >= ci, lgf, lgb)) * scale
    tab_s[1] = jnp.exp((rf + 1.0) * lgf) * scale
    tab_s[2] = jnp.exp((C - rf) * lgb) * scale
    tab_s[3] = jnp.exp((C - 1.0 - cf) * lgf)
    tab_s[4] = jnp.exp(cf * lgb)
    dec_f = jnp.exp(C * lgf)
    dec_b = jnp.exp(C * lgb)

    def summaries(n, carry):
        c0 = pl.multiple_of(n * C, C)
        kt = kt_ref[0, :, pl.ds(c0, C)].astype(F32)
        lhs = jnp.concatenate([kt * tab_s[3], kt * tab_s[4]], axis=0).astype(BF16)
        kv_s[n] = jnp.dot(lhs, v_ref[0, pl.ds(c0, C), :], preferred_element_type=F32)
        return carry

    lax.fori_loop(0, n_chunks, summaries, 0, unroll=RET_UNROLL)

    def scan(i, carry):
        sf, sb = carry
        m = n_chunks - 1 - i
        st_s[i, :C, :] = sf.astype(BF16)
        st_s[m, C:, :] = sb.astype(BF16)
        return sf * dec_f + kv_s[i, :C, :], sb * dec_b + kv_s[m, C:, :]

    zero = jnp.zeros((C, C), F32)
    lax.fori_loop(0, n_chunks, scan, (zero, zero), unroll=RET_UNROLL)

    gain = gn_ref[...]

    def outputs(n, carry):
        c0 = pl.multiple_of(n * C, C)
        qb = q_ref[0, pl.ds(c0, C), :]
        qn = qb.astype(F32)
        vn = v_ref[0, pl.ds(c0, C), :]
        sc = jnp.dot(qb, kt_ref[0, :, pl.ds(c0, C)], preferred_element_type=F32)
        lhs = jnp.concatenate([sc * tab_s[0], qn * tab_s[1], qn * tab_s[2]], axis=1).astype(BF16)
        rhs = jnp.concatenate([vn, st_s[n]], axis=0)
        o = jnp.dot(lhs, rhs, preferred_element_type=F32)
        mu = jnp.mean(o, axis=-1, keepdims=True)
        oc = o - mu
        var = jnp.mean(oc * oc, axis=-1, keepdims=True)
        gate = g_ref[0, pl.ds(c0, C), :].astype(F32)
        y = oc * lax.rsqrt(var + EPS) * gain * (gate * jax.nn.sigmoid(gate))
        o_ref[0, pl.ds(c0, C), :] = y.astype(BF16)
        return carry

    lax.fori_loop(0, n_chunks, outputs, 0, unroll=RET_UNROLL)


def _retention(qvg, kt, dec, gn):
    B, L, _ = qvg.shape
    H = RET_HEADS

    def col(off):
        return pl.BlockSpec((1, L, HEAD_DIM), lambda b, h: (b, 0, off + h))

    return pl.pallas_call(
        _retention_kernel,
        grid=(B, H),
        in_specs=[
            pl.BlockSpec((1, 2, LANES), lambda b, h: (h, 0, 0)),
            pl.BlockSpec((1, HEAD_DIM), lambda b, h: (0, h)),
            col(0),
            pl.BlockSpec((1, HEAD_DIM, L), lambda b, h: (b, h, 0)),
            col(H), col(2 * H),
        ],
        out_specs=pl.BlockSpec((1, L, HEAD_DIM), lambda b, h: (b, 0, h)),
        out_shape=jax.ShapeDtypeStruct((B, L, RET_WIDTH), BF16),
        scratch_shapes=[
            pltpu.VMEM((L // CHUNK, 2 * CHUNK, CHUNK), F32),
            pltpu.VMEM((L // CHUNK, 2 * CHUNK, CHUNK), BF16),
            pltpu.VMEM((5, CHUNK, CHUNK), F32),
        ],
        compiler_params=_cparams(("parallel", "parallel")),
        name="retention",
    )(dec, gn, qvg, kt, qvg, qvg)


def _filter_kernel(w1_ref, b1_ref, fr_ref, w2_ref, b2_ref, w3_ref, kk_ref, hid_s):
    L = kk_ref.shape[1] // 2
    hi = lax.Precision.HIGHEST

    @pl.when(pl.program_id(0) == 0)
    def _():
        r = lax.broadcasted_iota(jnp.int32, (HY_HIDDEN, L), 0)
        lane = lax.broadcasted_iota(jnp.int32, (HY_HIDDEN, L), 1)
        band_idx = jnp.where(r <= HY_EMB_BANDS, r - 1, r - 1 - HY_EMB_BANDS).astype(F32)
        band = 1e-4 + band_idx * ((HY_EMB_BANDS - 1 - 1e-4) / (HY_EMB_BANDS - 1))
        fr = fr_ref[...]

        def hidden(s_int):
            s = s_int.astype(F32)
            t = s / (L - 1.0)
            ang = (2.0 * math.pi / L) * s * band
            feats = jnp.where(r == 0, t,
                              jnp.where(r <= HY_EMB_BANDS, jnp.cos(ang),
                                        jnp.where(r <= 2 * HY_EMB_BANDS, -jnp.sin(ang), 0.0)))
            h = jnp.sin(fr * (jnp.dot(w1_ref[...], feats, precision=hi, preferred_element_type=F32) + b1_ref[...]))
            return jnp.sin(fr * (jnp.dot(w2_ref[...], h, precision=hi, preferred_element_type=F32) + b2_ref[...]))

        hid_s[0] = hidden(lane)
        hid_s[1] = hidden(L - lane)

    crow = lax.broadcasted_iota(jnp.int32, (HY_WIDTH, L), 0).astype(F32)
    delta = jnp.abs(HY_MIN_DECAY + crow * ((HY_MAX_DECAY - HY_MIN_DECAY) / (HY_WIDTH - 1)))
    clane = lax.broadcasted_iota(jnp.int32, (HY_WIDTH, L), 1)

    def taps(hid, s_row_int, w3):
        h = jnp.dot(w3, hid, precision=hi, preferred_element_type=F32)
        return h * jnp.exp(-(s_row_int.astype(F32) / (L - 1.0)) * delta)

    h_f = taps(hid_s[0], clane, w3_ref[0, 0])
    h_b = taps(hid_s[1], L - clane, w3_ref[0, 1])
    h_b = jnp.where(clane == 0, 0.0, h_b)
    norm = jnp.sum(jnp.abs(h_f), axis=-1, keepdims=True) + jnp.sum(jnp.abs(h_b), axis=-1, keepdims=True)
    inv = 1.0 / norm
    kk = jnp.concatenate([h_b * inv, h_f * inv], axis=1)
    bits = pltpu.bitcast(kk, jnp.uint32)
    bf = (bits + jnp.uint32(0x7FFF) + ((bits >> 16) & jnp.uint32(1))) >> 16
    lane2 = lax.broadcasted_iota(jnp.int32, bf.shape, 1)
    prev = jnp.where(lane2 == 0, jnp.uint32(0), pltpu.roll(bf, 1, axis=1))
    kk_ref[...] = bf | (prev << 16)


def _hyena_filters(w1t, b1, fr, w2t, b2, w3t, L):
    small = lambda shape: pl.BlockSpec(shape, lambda o: (0,) * len(shape))
    return pl.pallas_call(
        _filter_kernel,
        grid=(2,),
        in_specs=[
            small((HY_HIDDEN, HY_HIDDEN)), small((HY_HIDDEN, 1)), small((HY_HIDDEN, 1)),
            small((HY_HIDDEN, HY_HIDDEN)), small((HY_HIDDEN, 1)),
            pl.BlockSpec((1, 2, HY_WIDTH, HY_HIDDEN), lambda o: (o, 0, 0, 0)),
        ],
        out_specs=pl.BlockSpec((HY_WIDTH, 2 * L), lambda o: (o, 0)),
        out_shape=jax.ShapeDtypeStruct((2 * HY_WIDTH, 2 * L), jnp.uint32),
        scratch_shapes=[pltpu.VMEM((2, HY_HIDDEN, L), F32)],
        compiler_params=_cparams(("arbitrary",)),
        name="hyena_filters",
    )(w1t, b1, fr, w2t, b2, w3t)


def _hyena_kernel(kk1_ref, kk2_ref, xv_ref, x1_ref, x2_ref, bias_ref, out_ref,
                  toep1_s, toep2_s, u2_s, x1u2_s, y2_s, *, B, BP):
    n_tiles = xv_ref.shape[0]
    L = n_tiles * LANES
    T = HY_BLOCK
    NB = L // T
    TPB = T // LANES

    @pl.when(pl.program_id(0) == 0)
    def _():
        toep1_s[...] = jnp.zeros_like(toep1_s)
        toep2_s[...] = jnp.zeros_like(toep2_s)
        u2_s[...] = jnp.zeros_like(u2_s)
        x1u2_s[...] = jnp.zeros_like(x1u2_s)

    def build_toeplitz(kk_ref, c, toep_s):
        row = jnp.broadcast_to(kk_ref[pl.ds(c, 1), :], (SUBLANES, 2 * L))
        for g in range(LANES // BF16_ROWS):
            rolled = pltpu.roll(row, BF16_ROWS * g, axis=1, stride=2, stride_axis=0)
            tile = pltpu.bitcast(rolled, BF16)
            toep_s[BF16_ROWS * g:BF16_ROWS * (g + 1), :] = tile

    def long_conv(toep_s, u_s, y_s):
        y_s[...] = jnp.zeros_like(y_s)
        for d in range(-(NB - 1), NB):
            n = NB - abs(d)
            src = max(0, -d) * BP
            dst = max(0, d) * BP
            w0 = L + d * T
            m = jnp.concatenate([toep_s[:, w0:w0 + T], toep_s[:, w0 - LANES:w0 - LANES + T]], axis=0)
            lhs = u_s[src:src + n * BP, :].astype(BF16)
            y_s[dst:dst + n * BP, :] += jnp.dot(lhs, m, preferred_element_type=F32)

    def channel(c, k):
        u_s, x1u_s, y_s = u2_s.at[k], x1u2_s.at[k], y2_s.at[k]
        rows = pl.ds(c, B, stride=SUBLANES)
        for j in range(n_tiles):
            J, l0 = j // TPB, (j % TPB) * LANES
            u_s[J * BP:J * BP + B, l0:l0 + LANES] = xv_ref.at[j][rows, :]
            x1u_s[J * BP:J * BP + B, l0:l0 + LANES] = x1_ref.at[j][rows, :]
        b0 = bias_ref[0, 0, pl.ds(c, 1), :]
        b1 = bias_ref[0, 1, pl.ds(c, 1), :]
        build_toeplitz(kk2_ref, c, toep2_s)
        long_conv(toep1_s, u_s, y_s)
        u_s[...] = x1u_s[...] * (y_s[...] + u_s[...] * b0)
        build_toeplitz(kk1_ref, jnp.minimum(c + 1, SUBLANES - 1), toep1_s)
        long_conv(toep2_s, u_s, y_s)
        for j in range(n_tiles):
            J, l0 = j // TPB, (j % TPB) * LANES
            yy = y_s[J * BP:J * BP + B, l0:l0 + LANES] + u_s[J * BP:J * BP + B, l0:l0 + LANES] * b1[:, :LANES]
            out_ref.at[j][rows, :] = x2_ref.at[j][rows, :] * yy

    def channel_pair(i, carry):
        channel(2 * i, 0)
        channel(2 * i + 1, 1)
        return carry

    build_toeplitz(kk1_ref, 0, toep1_s)
    lax.fori_loop(0, SUBLANES // 2, channel_pair, 0, unroll=HY_PAIR_UNROLL)


def _hyena(kk, hy_t, bias, B):
    n_in, n_tiles, R, _ = hy_t.shape
    L = n_tiles * LANES
    NB = L // HY_BLOCK
    BP = -(-B // SUBLANES) * SUBLANES
    n_c = HY_WIDTH // SUBLANES

    def sect(s):
        return pl.BlockSpec((None, n_tiles, R, LANES), lambda c: (s * n_c + c, 0, 0, 0))

    return pl.pallas_call(
        functools.partial(_hyena_kernel, B=B, BP=BP),
        grid=(n_c,),
        in_specs=[
            pl.BlockSpec((SUBLANES, 2 * L), lambda c: (c, 0)),
            pl.BlockSpec((SUBLANES, 2 * L), lambda c: (n_c + c, 0)),
            sect(0), sect(1), sect(2),
            pl.BlockSpec((1, 2, SUBLANES, HY_BLOCK), lambda c: (c, 0, 0, 0)),
        ],
        out_specs=pl.BlockSpec((None, n_tiles, R, LANES), lambda c: (c, 0, 0, 0)),
        out_shape=jax.ShapeDtypeStruct((n_c, n_tiles, R, LANES), F32),
        scratch_shapes=[
            pltpu.VMEM((LANES, 2 * L), BF16), pltpu.VMEM((LANES, 2 * L), BF16),
            pltpu.VMEM((2, NB * BP, HY_BLOCK), F32),
            pltpu.VMEM((2, NB * BP, HY_BLOCK), F32),
            pltpu.VMEM((2, NB * BP, HY_BLOCK), F32),
        ],
        compiler_params=_cparams(("arbitrary",)),
        name="hyena",
    )(kk, kk, hy_t, hy_t, hy_t, bias)


def _pool_kernel(u_ref, inv_ref, w_ref, sc_ref, o_ref):
    L = u_ref.shape[1]
    u = u_ref[0].astype(F32)
    pad = jnp.zeros((POOL_PAD, POOL_WIDTH), F32)
    up = jnp.concatenate([pad, u, pad], axis=0)
    n = L + 2 * POOL_PAD
    left = lax.broadcasted_iota(jnp.int32, (L, LANES), 1) < POOL_GROUP_DIM

    def both(x, k):
        return pltpu.roll(x, k, axis=0) + pltpu.roll(x, n - k, axis=0)

    def windows(x, levels):
        q = pltpu.roll(x, 1, axis=0) + x
        out = [q]
        for k in (1, 2, 4)[:levels - 1]:
            q = both(q, k)
            out.append(q)
        return [w[POOL_PAD:POOL_PAD + L] for w in out]

    p2, p4 = windows(up[:, :LANES], 2)
    _, _, p8, p16 = windows(up[:, LANES:], 4)
    total = jnp.concatenate([jnp.where(left, p2, p4), jnp.where(left, p8, p16)], axis=1)
    d = (total * inv_ref[...] - u).astype(BF16)
    o_ref[0] = (jnp.dot(d, w_ref[...], preferred_element_type=F32) * sc_ref[...]).astype(BF16)


def _pool_inv_counts(L):
    t = jnp.arange(L)[:, None]
    half = jnp.repeat(jnp.asarray([w // 2 for w in POOL_WINDOWS]), POOL_GROUP_DIM)[None, :]
    cnt = jnp.minimum(t + half - 1, L - 1) - jnp.maximum(t - half, 0) + 1
    return 1.0 / cnt.astype(F32)


def _pool(pool_in, w_bd, scale):
    B, L, _ = pool_in.shape
    return pl.pallas_call(
        _pool_kernel,
        grid=(B,),
        in_specs=[
            pl.BlockSpec((1, L, POOL_WIDTH), lambda b: (b, 0, 0)),
            pl.BlockSpec((L, POOL_WIDTH), lambda b: (0, 0)),
            pl.BlockSpec((POOL_WIDTH, POOL_WIDTH), lambda b: (0, 0)),
            pl.BlockSpec((1, POOL_WIDTH), lambda b: (0, 0)),
        ],
        out_specs=pl.BlockSpec((1, L, POOL_WIDTH), lambda b: (b, 0, 0)),
        out_shape=jax.ShapeDtypeStruct((B, L, POOL_WIDTH), BF16),
        compiler_params=_cparams(("parallel",)),
        name="pool_mixer",
    )(pool_in, _pool_inv_counts(L), w_bd, scale)


def _mix_out_kernel(xa_ref, xb_ref, ret_ref, hy_ref, pool_ref, wr_ref, wh_ref, wp_ref, g_ref, x1_ref, h2_ref, *, nbp):
    tm = xa_ref.shape[1]
    hy_t = jnp.concatenate([hy_ref[:, j].reshape(HY_WIDTH, LANES) for j in range(tm // LANES)],
                           axis=1).astype(BF16)
    mix = jnp.dot(ret_ref[0], wr_ref[...], preferred_element_type=F32)
    mix += lax.dot_general(hy_t, wh_ref[...], (((0,), (0,)), ((), ())), preferred_element_type=F32)
    mix += jnp.dot(pool_ref[0], wp_ref[...], preferred_element_type=F32)
    x1 = _pick(nbp, xa_ref, xb_ref) + mix
    x1_ref[0] = x1
    h2_ref[0] = _rms(x1, g_ref[...]).astype(BF16)


def _mix_out(x, ret, hy_o, pool_o, wr, wh, wp, g, tm):
    xa, xb, nbp, off = x
    B, L, _ = ret.shape
    nt = L // tm
    return pl.pallas_call(
        functools.partial(_mix_out_kernel, nbp=nbp),
        grid=(B, nt),
        in_specs=[
            *_pair_specs(tm, D_MODEL, nt, nbp, off),
            pl.BlockSpec((1, tm, RET_WIDTH), lambda b, i: (b, i, 0)),
            pl.BlockSpec((HY_WIDTH // SUBLANES, tm // LANES, SUBLANES, LANES), lambda b, i: (0, i, b, 0)),
            pl.BlockSpec((1, tm, POOL_WIDTH), lambda b, i: (b, i, 0)),
            pl.BlockSpec((RET_WIDTH, D_MODEL), lambda b, i: (0, 0)),
            pl.BlockSpec((HY_WIDTH, D_MODEL), lambda b, i: (0, 0)),
            pl.BlockSpec((POOL_WIDTH, D_MODEL), lambda b, i: (0, 0)),
            pl.BlockSpec((1, D_MODEL), lambda b, i: (0, 0)),
        ],
        out_specs=[
            pl.BlockSpec((1, tm, D_MODEL), lambda b, i: (b, i, 0)),
            pl.BlockSpec((1, tm, D_MODEL), lambda b, i: (b, i, 0)),
        ],
        out_shape=[
            jax.ShapeDtypeStruct((B, L, D_MODEL), F32),
            jax.ShapeDtypeStruct((B, L, D_MODEL), BF16),
        ],
        compiler_params=_cparams(("arbitrary", "arbitrary")),
        name="mix_out",
    )(xa, xb, ret, hy_o, pool_o, wr, wh, wp, g)


def _ffn_kernel(h_ref, hp_ref, hn_ref, x1_ref, wup_ref, cw_ref, wd_ref, pa_ref, pb_ref, pw_ref, pg_ref,
                pn_ref, fin_ref, *o_refs, nbp, final):
    tm = h_ref.shape[1]
    i = pl.program_id(1)
    nt = pl.num_programs(1)

    hx = jnp.concatenate([h_ref[0], hp_ref[0], hn_ref[0]], axis=0)
    row = lax.broadcasted_iota(jnp.int32, (tm, FFN_SUB), 0)
    n_sub = D_FF // FFN_SUB

    def up_proj(s):
        c0 = s * FFN_SUB
        gate = jnp.dot(hx, wup_ref[:, c0:c0 + FFN_SUB], preferred_element_type=F32)
        up = jnp.dot(hx[:tm], wup_ref[:, D_FF + c0:D_FF + c0 + FFN_SUB], preferred_element_type=F32)
        return gate, up

    acc = x1_ref[0]
    nxt = up_proj(0)
    group, g0 = [], 0
    for s in range(n_sub):
        c0 = s * FFN_SUB
        gx, up = nxt
        if s + 1 < n_sub:
            nxt = up_proj(s + 1)
        gate = gx[:tm]
        g_prev = jnp.where(i == 0, 0.0, gx[tm + BF16_ROWS - 1:tm + BF16_ROWS])
        g_next = jnp.where(i == nt - 1, 0.0, gx[tm + BF16_ROWS:tm + BF16_ROWS + 1])
        g_up = jnp.where(row == 0, g_prev, pltpu.roll(gate, 1, axis=0))
        g_dn = jnp.where(row == tm - 1, g_next, pltpu.roll(gate, tm - 1, axis=0))
        cw = cw_ref[:, c0:c0 + FFN_SUB]
        gc = g_up * cw[0:1] + gate * cw[1:2] + g_dn * cw[2:3]
        group.append((jax.nn.gelu(gc) * up).astype(BF16))
        if len(group) == FFN_DOWN_GROUP or s == n_sub - 1:
            a = jnp.concatenate(group, axis=1)
            acc = acc + jnp.dot(a, wd_ref[g0:c0 + FFN_SUB, :], preferred_element_type=F32)
            group, g0 = [], c0 + FFN_SUB

    x2 = acc
    p = _pick(nbp, pa_ref, pb_ref).astype(BF16)
    e = _rms(jnp.dot(p, pw_ref[...], preferred_element_type=F32), pn_ref[...])
    gt = jax.nn.sigmoid(jnp.dot(x2.astype(BF16), pg_ref[...], preferred_element_type=F32))
    x3 = x2 + gt * e
    if not final:
        o_refs[0][0] = x3
        return
    y = _rms(x3, fin_ref[...])
    b = pl.program_id(0)

    @pl.when(b < nbp)
    def _():
        o_refs[0][0] = y

    @pl.when(b >= nbp)
    def _():
        o_refs[1][0] = y


def _ffn(h2, x1, w_up, cw, w_down, p, layer, ple_w, ple_gate, ple_norm, fin, tm, final):
    pa, pb, nbp, off = p
    B, L, _ = x1.shape
    nt = L // tm
    halo = BF16_ROWS
    nh = L // halo
    per = tm // halo
    full = lambda shape: pl.BlockSpec(shape, lambda b, i: (0,) * len(shape), pipeline_mode=pl.Buffered(1))
    tile = pl.BlockSpec((1, tm, D_MODEL), lambda b, i: (b, i, 0))
    if final:
        out_specs = list(_pair_specs(tm, D_MODEL, nt, nbp, 0))
        out_shape = [jax.ShapeDtypeStruct((nbp, L, D_MODEL), F32), jax.ShapeDtypeStruct((B - nbp, L, D_MODEL), F32)]
    else:
        out_specs = [tile]
        out_shape = [jax.ShapeDtypeStruct((B, L, D_MODEL), F32)]
    return pl.pallas_call(
        functools.partial(_ffn_kernel, nbp=nbp, final=final),
        grid=(B, nt),
        in_specs=[
            tile,
            pl.BlockSpec((1, halo, D_MODEL), lambda b, i: (b, jnp.maximum(i * per - 1, 0), 0)),
            pl.BlockSpec((1, halo, D_MODEL), lambda b, i: (b, jnp.minimum((i + 1) * per, nh - 1), 0)),
            tile,
            full((D_MODEL, 2 * D_FF)), full((3, D_FF)), full((D_FF, D_MODEL)),
            *_pair_specs(tm, PLE_DIM, nt, nbp, off, lead=layer),
            full((PLE_DIM, D_MODEL)), full((D_MODEL, D_MODEL)), full((1, D_MODEL)), full((1, D_MODEL)),
        ],
        out_specs=out_specs,
        out_shape=out_shape,
        compiler_params=_cparams(("arbitrary", "arbitrary")),
        name="ffn_ple",
    )(h2, h2, h2, x1, w_up, cw, w_down, pa, pb, ple_w, ple_gate, ple_norm, fin)


def _mix_ffn_kernel(xa_ref, xb_ref, xpa_ref, xpb_ref, xna_ref, xnb_ref, ret_ref, retp_ref, retn_ref,
                    hy_ref, hyp_ref, hyn_ref, pool_ref, poolp_ref, pooln_ref, wr_ref, wh_ref, wp_ref, g_ref,
                    wup_ref, cw_ref, wd_ref, pa_ref, pb_ref, pw_ref, pg_ref, pn_ref, fin_ref, *o_refs, nbp, final):
    tm = xa_ref.shape[1]
    i = pl.program_id(1)
    nt = pl.num_programs(1)
    H = BF16_ROWS

    x_x = jnp.concatenate([_pick(nbp, xa_ref, xb_ref), _pick(nbp, xpa_ref, xpb_ref),
                           _pick(nbp, xna_ref, xnb_ref)], axis=0)
    ret_x = jnp.concatenate([ret_ref[0], retp_ref[0], retn_ref[0]], axis=0)
    pool_x = jnp.concatenate([pool_ref[0], poolp_ref[0], pooln_ref[0]], axis=0)
    hy_tiles = [hy_ref[:, j] for j in range(tm // LANES)] + [hyp_ref[:, 0], hyn_ref[:, 0]]
    hy_t = jnp.concatenate([t.reshape(HY_WIDTH, LANES) for t in hy_tiles], axis=1).astype(BF16)
    hy_mix = lax.dot_general(hy_t, wh_ref[...], (((0,), (0,)), ((), ())), preferred_element_type=F32)
    hy_mix = jnp.concatenate([hy_mix[:tm], hy_mix[tm + LANES - H:tm + LANES + H]], axis=0)
    mix = jnp.dot(ret_x, wr_ref[...], preferred_element_type=F32) + hy_mix
    mix += jnp.dot(pool_x, wp_ref[...], preferred_element_type=F32)
    x1_x = x_x + mix
    hx = _rms(x1_x, g_ref[...]).astype(BF16)
    row = lax.broadcasted_iota(jnp.int32, (tm, FFN_SUB), 0)
    n_sub = D_FF // FFN_SUB

    def up_proj(s):
        c0 = s * FFN_SUB
        gate = jnp.dot(hx, wup_ref[:, c0:c0 + FFN_SUB], preferred_element_type=F32)
        up = jnp.dot(hx[:tm], wup_ref[:, D_FF + c0:D_FF + c0 + FFN_SUB], preferred_element_type=F32)
        return gate, up

    acc = x1_x[:tm]
    nxt = up_proj(0)
    group, g0 = [], 0
    for s in range(n_sub):
        c0 = s * FFN_SUB
        gx, up = nxt
        if s + 1 < n_sub:
            nxt = up_proj(s + 1)
        gate = gx[:tm]
        g_prev = jnp.where(i == 0, 0.0, gx[tm + H - 1:tm + H])
        g_next = jnp.where(i == nt - 1, 0.0, gx[tm + H:tm + H + 1])
        g_up = jnp.where(row == 0, g_prev, pltpu.roll(gate, 1, axis=0))
        g_dn = jnp.where(row == tm - 1, g_next, pltpu.roll(gate, tm - 1, axis=0))
        cw = cw_ref[:, c0:c0 + FFN_SUB]
        gc = g_up * cw[0:1] + gate * cw[1:2] + g_dn * cw[2:3]
        group.append((jax.nn.gelu(gc) * up).astype(BF16))
        if len(group) == FFN_DOWN_GROUP or s == n_sub - 1:
            a = jnp.concatenate(group, axis=1)
            acc = acc + jnp.dot(a, wd_ref[g0:c0 + FFN_SUB, :], preferred_element_type=F32)
            group, g0 = [], c0 + FFN_SUB

    x2 = acc
    p = _pick(nbp, pa_ref, pb_ref).astype(BF16)
    e = _rms(jnp.dot(p, pw_ref[...], preferred_element_type=F32), pn_ref[...])
    gt = jax.nn.sigmoid(jnp.dot(x2.astype(BF16), pg_ref[...], preferred_element_type=F32))
    x3 = x2 + gt * e
    if not final:
        o_refs[0][0] = x3
        return
    y = _rms(x3, fin_ref[...])
    b = pl.program_id(0)

    @pl.when(b < nbp)
    def _():
        o_refs[0][0] = y

    @pl.when(b >= nbp)
    def _():
        o_refs[1][0] = y


def _mix_ffn(x, ret, hy_o, pool_o, wr, wh, wp, g, w_up, cw, w_down, p, layer, ple_w, ple_gate, ple_norm, fin, tm, final):
    xa, xb, nbp, off = x
    pa, pb, _, p_off = p
    B, L, _ = ret.shape
    nt = L // tm
    H = BF16_ROWS
    per, last = tm // H, L // H - 1
    tpt, last_tile = tm // LANES, L // LANES - 1
    full = lambda shape: pl.BlockSpec(shape, lambda b, i: (0,) * len(shape), pipeline_mode=pl.Buffered(1))

    def rows(width):
        return (pl.BlockSpec((1, tm, width), lambda b, i: (b, i, 0)),
                pl.BlockSpec((1, H, width), lambda b, i: (b, jnp.maximum(i * per - 1, 0), 0)),
                pl.BlockSpec((1, H, width), lambda b, i: (b, jnp.minimum((i + 1) * per, last), 0)))

    n_hy = HY_WIDTH // SUBLANES
    hy_specs = (pl.BlockSpec((n_hy, tpt, SUBLANES, LANES), lambda b, i: (0, i, b, 0)),
                pl.BlockSpec((n_hy, 1, SUBLANES, LANES), lambda b, i: (0, jnp.maximum(i * tpt - 1, 0), b, 0)),
                pl.BlockSpec((n_hy, 1, SUBLANES, LANES), lambda b, i: (0, jnp.minimum((i + 1) * tpt, last_tile), b, 0)))
    if final:
        out_specs = list(_pair_specs(tm, D_MODEL, nt, nbp, 0))
        out_shape = [jax.ShapeDtypeStruct((nbp, L, D_MODEL), F32), jax.ShapeDtypeStruct((B - nbp, L, D_MODEL), F32)]
    else:
        out_specs = [pl.BlockSpec((1, tm, D_MODEL), lambda b, i: (b, i, 0))]
        out_shape = [jax.ShapeDtypeStruct((B, L, D_MODEL), F32)]
    return pl.pallas_call(
        functools.partial(_mix_ffn_kernel, nbp=nbp, final=final),
        grid=(B, nt),
        in_specs=[
            *_pair_specs(tm, D_MODEL, nt, nbp, off),
            *_pair_halo_specs(H, D_MODEL, tm, L, nbp, off),
            *rows(RET_WIDTH), *hy_specs, *rows(POOL_WIDTH),
            full((RET_WIDTH, D_MODEL)), full((HY_WIDTH, D_MODEL)), full((POOL_WIDTH, D_MODEL)), full((1, D_MODEL)),
            full((D_MODEL, 2 * D_FF)), full((3, D_FF)), full((D_FF, D_MODEL)),
            *_pair_specs(tm, PLE_DIM, nt, nbp, p_off, lead=layer),
            full((PLE_DIM, D_MODEL)), full((D_MODEL, D_MODEL)), full((1, D_MODEL)), full((1, D_MODEL)),
        ],
        out_specs=out_specs,
        out_shape=out_shape,
        compiler_params=_cparams(("arbitrary", "arbitrary")),
        name="mix_ffn",
    )(xa, xb, xa, xb, xa, xb, ret, ret, ret, hy_o, hy_o, hy_o, pool_o, pool_o, pool_o, wr, wh, wp, g,
      w_up, cw, w_down, pa, pb, ple_w, ple_gate, ple_norm, fin)


def _rope_tables(L):
    half = HEAD_DIM // 2
    inv = ROPE_THETA ** (-jnp.arange(half, dtype=F32) / half)
    ang = jnp.arange(L, dtype=F32)[:, None] * inv[None, :]
    cos, sin = jnp.cos(ang), jnp.sin(ang)
    return jnp.concatenate([cos, cos], axis=-1), jnp.concatenate([-sin, sin], axis=-1)


def _block_diag(pool_w):
    out = jnp.zeros((POOL_WIDTH, POOL_WIDTH), pool_w.dtype)
    for g in range(len(POOL_WINDOWS)):
        sl = slice(g * POOL_GROUP_DIM, (g + 1) * POOL_GROUP_DIM)
        out = out.at[sl, sl].set(pool_w[g])
    return out


def _layer(x, B, p, layer, lw, cos, sin, fin, tm, final):
    (norm_mix, w_in, dec_f, dec_b, ret_gn, hy_conv, hy_w1, hy_b1, hy_freq, hy_w2, hy_b2, hy_w3, hy_bias,
     pool_w, pool_scale, w_out, norm_ffn, w_up, ffn_conv, w_down, ple_w, ple_gate, ple_norm) = lw
    L = x[0].shape[1]
    R = RET_WIDTH

    w_q, w_k, w_v, w_g, w_hy, w_pool = jnp.split(w_in, [R, 2 * R, 3 * R, 4 * R, 4 * R + HY_IN_WIDTH], axis=1)
    wa = jnp.concatenate([w_q, w_v, w_g, w_pool], axis=1).astype(BF16)
    qvg, kt, hy_t, pool_in = _in_proj(x, B, norm_mix[None], wa, w_k.T.astype(BF16), w_hy.astype(BF16), hy_conv,
                                      cos, sin, cos.T, sin.T, WIDE_TILE * tm)

    dec = jnp.broadcast_to(jnp.stack([dec_f, dec_b], axis=1)[:, :, None], (RET_HEADS, 2, LANES)).astype(F32)
    ret = _retention(qvg, kt, dec, ret_gn[None])

    w1t = jnp.zeros((HY_HIDDEN, HY_HIDDEN), F32).at[:, :hy_w1.shape[0]].set(hy_w1.T)
    w3t = hy_w3.T.reshape(2, 2, HY_WIDTH, HY_HIDDEN)
    kk = _hyena_filters(w1t, hy_b1[:, None], hy_freq[:, None], hy_w2.T, hy_b2[:, None], w3t, L)
    bias = jnp.broadcast_to(hy_bias.reshape(2, HY_WIDTH // SUBLANES, SUBLANES).transpose(1, 0, 2)[..., None],
                            (HY_WIDTH // SUBLANES, 2, SUBLANES, HY_BLOCK))
    hy_o = _hyena(kk, hy_t, bias, B)

    pool_o = _pool(pool_in, _block_diag(pool_w).astype(BF16), pool_scale[None])

    wo = w_out.astype(BF16)
    return _mix_ffn(x, ret, hy_o, pool_o, wo[:R], wo[R:R + HY_WIDTH], wo[R + HY_WIDTH:], norm_ffn[None],
                    w_up.astype(BF16), ffn_conv, w_down.astype(BF16), p, layer, ple_w.astype(BF16),
                    ple_gate.astype(BF16), ple_norm[None], fin, tm, final)


def _trunk(x_a, x_b, p_a, p_b, layer_weights, norm_final, tm):
    depth = p_a.shape[0]
    nbp, L = x_a.shape[0], x_a.shape[1]
    B = nbp + x_b.shape[0]
    cos, sin = _rope_tables(L)
    fin = norm_final[None]
    x = (x_a, x_b, nbp, 0)
    p = (p_a, p_b, nbp, 0)
    for i in range(depth):
        out = _layer(x, B, p, i, [w[i] for w in layer_weights], cos, sin, fin, tm, i == depth - 1)
        x = (out[0], out[0], nbp, nbp)
    return out[0], out[1]


def kernel(x_prompt, x_sample, p_prompt, p_sample, norm_mix, w_in, ret_decay_fwd, ret_decay_bwd, ret_gn,
           hy_short_conv, hy_w1, hy_b1, hy_freq, hy_w2, hy_b2, hy_w3, hy_bias, pool_w, pool_scale, w_out,
           norm_ffn, ffn_w_up, ffn_conv, ffn_w_down, ple_w, ple_gate_w, ple_norm, norm_final):
    layer_weights = (norm_mix, w_in, ret_decay_fwd, ret_decay_bwd, ret_gn, hy_short_conv, hy_w1, hy_b1,
                     hy_freq, hy_w2, hy_b2, hy_w3, hy_bias, pool_w, pool_scale, w_out, norm_ffn, ffn_w_up,
                     ffn_conv, ffn_w_down, ple_w, ple_gate_w, ple_norm)
    return _trunk(x_prompt, x_sample, p_prompt, p_sample, layer_weights, norm_final, ROW_TILE)
```
